```python
import math
import jax
import jax.numpy as jnp
from jax import lax
import numpy as np

D_MODEL = 1024
BATCH = 1
SEQ = 16384
DEPTH = 4

N_MIXERS = 4
GRID_W = 64
EPS = 1e-6
D_FF = 4 * D_MODEL
ROPE_THETA = 10000.0

NA_HEADS = 16
NA_HEAD_DIM = D_MODEL // NA_HEADS
NA_WIN_ROWS = 8
NA_WIN_COLS = 16
NA_Q_ROWS = 2
NA_COL_BLOCK = NA_WIN_COLS
NA_COL_BAND = 2 * NA_WIN_COLS

GDN_HEADS = 8
GDN_DK = D_MODEL // GDN_HEADS
GDN_DV = D_MODEL // GDN_HEADS
GDN_CONV = 5
GDN_CHUNK = 64

MLA_HEADS = 16
MLA_Q_RANK = 3 * D_MODEL // 4
MLA_KV_RANK = D_MODEL // 4
MLA_NOPE = 64
MLA_ROPE = 32
MLA_V = 64
MLA_Q_BLOCK = 128

RET_HEADS = 8
RET_DK = D_MODEL // RET_HEADS
RET_DV = 2 * D_MODEL // RET_HEADS
RET_CHUNK = 128

kernel_name = 'hybrid_bidir_na_gdn_mla_retention'


def rmsnorm(x, w):
    xf = x.astype(jnp.float32)
    y = xf * lax.rsqrt(jnp.mean(xf * xf, axis=-1, keepdims=True) + EPS)
    return (y * w.astype(jnp.float32)).astype(x.dtype)


def _l2norm(x):
    xf = x.astype(jnp.float32)
    return xf * lax.rsqrt(jnp.sum(xf * xf, axis=-1, keepdims=True) + EPS)


def rotary(x, pos):
    half = x.shape[-1] // 2
    inv = 1.0 / (ROPE_THETA ** (jnp.arange(half, dtype=jnp.float32) / half))
    ang = pos.astype(jnp.float32)[:, None] * inv[None, :]
    cos, sin = jnp.cos(ang), jnp.sin(ang)
    xf = x.astype(jnp.float32)
    x1, x2 = xf[..., :half], xf[..., half:]
    return jnp.concatenate([x1 * cos - x2 * sin, x2 * cos + x1 * sin], axis=-1).astype(x.dtype)


def centred_depthwise_conv(x, w):
    c, kw = w.shape
    rhs = jnp.transpose(w)[:, None, :].astype(x.dtype)
    return lax.conv_general_dilated(x, rhs, window_strides=(1,), padding=[(kw // 2, kw // 2)],
                                    dimension_numbers=('NWC', 'WIO', 'NWC'), feature_group_count=c)


def sq_relu_mlp(h, w1, w2):
    a = jax.nn.relu(h @ w1)
    return (a * a) @ w2


def neighbourhood_attention(h, w_qkv, rpb, w_o):
    B, S, _ = h.shape
    H, dh = NA_HEADS, NA_HEAD_DIM
    rows = S // GRID_W
    kr = min(NA_WIN_ROWS, rows)
    nbr = min(NA_Q_ROWS + kr - 1, rows)
    n_qblk = rows // NA_Q_ROWS
    n_cblk = GRID_W // NA_COL_BLOCK
    scale = dh ** -0.5
    qkv = (h @ w_qkv).reshape(B, rows, GRID_W, 3, H, dh)
    q, k, v = qkv[:, :, :, 0], qkv[:, :, :, 1], qkv[:, :, :, 2]
    cb = np.arange(n_cblk)
    kc0 = np.clip(cb * NA_COL_BLOCK - NA_WIN_COLS // 2, 0, GRID_W - NA_COL_BAND)
    kcols = kc0[:, None] + np.arange(NA_COL_BAND)[None, :]
    qcols = cb[:, None] * NA_COL_BLOCK + np.arange(NA_COL_BLOCK)[None, :]
    qc0 = np.clip(qcols - NA_WIN_COLS // 2, 0, GRID_W - NA_WIN_COLS)
    col_ok = (kcols[:, None, :] >= qc0[:, :, None]) & (kcols[:, None, :] < qc0[:, :, None] + NA_WIN_COLS)
    dcol = np.clip(kcols[:, None, :] - qcols[:, :, None] + NA_WIN_COLS - 1, 0, 2 * NA_WIN_COLS - 2)

    def row_block(i):
        r0 = i * NA_Q_ROWS
        rb = jnp.clip(r0 - kr // 2, 0, rows - nbr)
        qb = lax.dynamic_slice_in_dim(q, r0, NA_Q_ROWS, axis=1).reshape(
            B, NA_Q_ROWS, n_cblk, NA_COL_BLOCK, H, dh)
        kb = lax.dynamic_slice_in_dim(k, rb, nbr, axis=1)[:, :, kcols]
        vb = lax.dynamic_slice_in_dim(v, rb, nbr, axis=1)[:, :, kcols]
        qrows = r0 + jnp.arange(NA_Q_ROWS)
        krows = rb + jnp.arange(nbr)
        qr0 = jnp.clip(qrows - kr // 2, 0, rows - kr)
        row_ok = (krows[None, :] >= qr0[:, None]) & (krows[None, :] < qr0[:, None] + kr)
        drow = jnp.clip(krows[None, :] - qrows[:, None] + NA_WIN_ROWS - 1, 0, 2 * NA_WIN_ROWS - 2)
        bias = rpb[:, drow[None, :, None, :, None], dcol[:, None, :, None, :]].astype(jnp.float32)
        mask = row_ok[None, :, None, :, None] & col_ok[:, None, :, None, :]
        s = jnp.einsum('brjchd,bnjkhd->bhjrcnk', qb, kb).astype(jnp.float32) * scale + bias
        p = jax.nn.softmax(jnp.where(mask, s, -jnp.inf), axis=(-2, -1)).astype(vb.dtype)
        o = jnp.einsum('bhjrcnk,bnjkhd->brjchd', p, vb)
        return o.reshape(B, NA_Q_ROWS * GRID_W, H * dh)

    o = lax.map(row_block, jnp.arange(n_qblk))
    o = jnp.moveaxis(o, 0, 1).reshape(B, S, H * dh)
    return (o @ w_o).astype(h.dtype)


def gated_delta_rule_chunked(q, k, v, g, beta):
    f32 = jnp.float32
    B, H, S, dk = k.shape
    dv = v.shape[-1]
    C = GDN_CHUNK
    N = S // C
    q, k, v = [t.astype(f32).reshape(B, H, N, C, t.shape[-1]) for t in (q, k, v)]
    g = g.astype(f32).reshape(B, H, N, C)
    beta = beta.astype(f32).reshape(B, H, N, C)
    G = jnp.cumsum(g, axis=-1)
    incl = np.tril(np.ones((C, C), dtype=bool))
    strict = np.tril(np.ones((C, C), dtype=bool), -1)
    decay = jnp.where(incl, jnp.exp(jnp.where(incl, G[..., :, None] - G[..., None, :], 0.0)), 0.0)
    kb = k * beta[..., None]
    lmat = jnp.where(strict, jnp.einsum('bhncd,bhnmd->bhncm', kb, k) * decay, 0.0)
    tmat = lmat + jnp.eye(C, dtype=f32)
    rhs = jnp.concatenate([v * beta[..., None], kb * jnp.exp(G)[..., None]], axis=-1)
    sol = lax.linalg.triangular_solve(tmat, rhs, left_side=True, lower=True)
    u, w = sol[..., :dv], sol[..., dv:]
    a_intra = jnp.where(incl, jnp.einsum('bhncd,bhnmd->bhncm', q, k) * decay, 0.0)
    q_dec = q * jnp.exp(G)[..., None]
    k_dec = k * jnp.exp(G[..., -1:] - G)[..., None]
    g_last = jnp.exp(G[..., -1])

    def step(state, xs):
        u_c, w_c, qd_c, kd_c, a_c, gl_c = xs
        v_new = u_c - jnp.einsum('bhck,bhkv->bhcv', w_c, state)
        o_c = jnp.einsum('bhck,bhkv->bhcv', qd_c, state) + jnp.einsum('bhcm,bhmv->bhcv', a_c, v_new)
        state = state * gl_c[..., None, None] + jnp.einsum('bhck,bhcv->bhkv', kd_c, v_new)
        return state, o_c

    xs = tuple(jnp.moveaxis(t, 2, 0) for t in (u, w, q_dec, k_dec, a_intra, g_last))
    _, o = lax.scan(step, jnp.zeros((B, H, dk, dv), f32), xs)
    return jnp.moveaxis(o, 0, 2).reshape(B, H, S, dv)


def gated_deltanet(h, w_in, conv_w, a_log_f, a_log_b, dt_bias_f, dt_bias_b, o_norm, w_o):
    f32 = jnp.float32
    B, S, _ = h.shape
    H, dk, dv = GDN_HEADS, GDN_DK, GDN_DV
    n_qkv = 2 * H * dk + H * dv
    proj = h @ w_in
    qkv = jax.nn.silu(centred_depthwise_conv(proj[..., :n_qkv], conv_w))
    z = proj[..., n_qkv:n_qkv + H * dv].reshape(B, S, H, dv)
    b_f, b_b, a_f, a_b = jnp.split(proj[..., n_qkv + H * dv:].astype(f32), 4, axis=-1)

    def heads(t, d):
        return t.reshape(B, S, H, d).transpose(0, 2, 1, 3)

    q = _l2norm(heads(qkv[..., :H * dk], dk)) * dk ** -0.5
    k = _l2norm(heads(qkv[..., H * dk:2 * H * dk], dk))
    v = heads(qkv[..., 2 * H * dk:], dv)

    def direction(a, b, a_log, dt_bias, reverse):
        g = (-jnp.exp(a_log.astype(f32)) * jax.nn.softplus(a + dt_bias.astype(f32))).transpose(0, 2, 1)
        beta = jax.nn.sigmoid(b).transpose(0, 2, 1)
        if reverse:
            qq, kk, vv, gg, bb = [jnp.flip(t, axis=2) for t in (q, k, v, g, beta)]
            return jnp.flip(gated_delta_rule_chunked(qq, kk, vv, gg, bb), axis=2)
        return gated_delta_rule_chunked(q, k, v, g, beta)

    o = direction(a_f, b_f, a_log_f, dt_bias_f, False) + direction(a_b, b_b, a_log_b, dt_bias_b, True)
    o = rmsnorm(o.transpose(0, 2, 1, 3), o_norm) * jax.nn.silu(z.astype(f32))
    return (o.reshape(B, S, H * dv) @ w_o).astype(h.dtype)


def latent_attention(h, w_in, q_norm, w_uq, kv_norm, w_ukv, w_o, pos):
    B, S, _ = h.shape
    H = MLA_HEADS
    dqk = MLA_NOPE + MLA_ROPE
    proj = h @ w_in
    c_q = rmsnorm(proj[..., :MLA_Q_RANK], q_norm)
    c_kv = rmsnorm(proj[..., MLA_Q_RANK:MLA_Q_RANK + MLA_KV_RANK], kv_norm)
    k_rope = rotary(proj[..., MLA_Q_RANK + MLA_KV_RANK:], pos)
    q = (c_q @ w_uq).reshape(B, S, H, dqk).transpose(0, 2, 1, 3)
    q = jnp.concatenate([q[..., :MLA_NOPE], rotary(q[..., MLA_NOPE:], pos)], axis=-1) * dqk ** -0.5
    kv = (c_kv @ w_ukv).reshape(B, S, H, MLA_NOPE + MLA_V).transpose(0, 2, 1, 3)
    k = jnp.concatenate([kv[..., :MLA_NOPE],
                         jnp.broadcast_to(k_rope[:, None], (B, H, S, MLA_ROPE)).astype(kv.dtype)], axis=-1)
    v = kv[..., MLA_NOPE:]
    nb = S // MLA_Q_BLOCK
    qb = q.reshape(B, H, nb, MLA_Q_BLOCK, dqk).transpose(2, 0, 1, 3, 4)

    def attend(q_blk):
        s = jnp.einsum('bhqd,bhkd->bhqk', q_blk, k).astype(jnp.float32)
        p = jax.nn.softmax(s, axis=-1).astype(v.dtype)
        return jnp.einsum('bhqk,bhkd->bhqd', p, v)

    o = lax.map(attend, qb)
    o = o.transpose(1, 0, 3, 2, 4).reshape(B, S, H * MLA_V)
    return (o @ w_o).astype(h.dtype)


def retention_chunked(q, k, v, log_gamma, include_diag):
    f32 = jnp.float32
    B, H, S, dk = q.shape
    dv = v.shape[-1]
    C = RET_CHUNK
    N = S // C
    q, k, v = [jnp.moveaxis(t.astype(f32).reshape(B, H, N, C, t.shape[-1]), 2, 0) for t in (q, k, v)]
    idx = np.arange(C)
    diff = idx[:, None] - idx[None, :]
    lower = diff >= 0 if include_diag else diff > 0
    lg = log_gamma.astype(f32)
    dmask = jnp.where(lower, jnp.exp(jnp.where(lower, diff, 0).astype(f32)[None] * lg[:, None, None]), 0.0)
    idxf = jnp.arange(C, dtype=f32)
    q_dec = jnp.exp((idxf[None, :] + 1.0) * lg[:, None])
    k_dec = jnp.exp((C - 1.0 - idxf[None, :]) * lg[:, None])
    g_chunk = jnp.exp(C * lg)

    def step(state, xs):
        q_c, k_c, v_c = xs
        inner = jnp.einsum('bhcd,bhmd->bhcm', q_c, k_c) * dmask
        o_c = (jnp.einsum('bhcm,bhmv->bhcv', inner, v_c)
               + jnp.einsum('bhcd,bhdv->bhcv', q_c * q_dec[..., None], state))
        state = state * g_chunk[:, None, None] + jnp.einsum('bhmd,bhmv->bhdv', k_c * k_dec[..., None], v_c)
        return state, o_c

    _, o = lax.scan(step, jnp.zeros((B, H, dk, dv), f32), (q, k, v))
    return jnp.moveaxis(o, 0, 2).reshape(B, H, S, dv)


def retention(h, w_in, gn_w, w_o, pos):
    f32 = jnp.float32
    B, S, _ = h.shape
    H, dk, dv = RET_HEADS, RET_DK, RET_DV
    proj = h @ w_in
    q, k, v, gate = jnp.split(proj, [H * dk, 2 * H * dk, 2 * H * dk + H * dv], axis=-1)

    def heads(t, d):
        return t.reshape(B, S, H, d).transpose(0, 2, 1, 3)

    q = rotary(heads(q, dk), pos)
    k = rotary(heads(k, dk), pos) * dk ** -0.5
    v = heads(v, dv)
    log_gamma_f = jnp.log1p(-jnp.exp2(-5.0 - jnp.arange(H, dtype=f32)))
    log_gamma_b = log_gamma_f[::-1]
    o_f = retention_chunked(q, k, v, log_gamma_f, True)
    o_b = jnp.flip(retention_chunked(jnp.flip(q, axis=2), jnp.flip(k, axis=2), jnp.flip(v, axis=2),
                                     log_gamma_b, False), axis=2)
    o = (o_f + o_b).transpose(0, 2, 1, 3)
    mu = jnp.mean(o, axis=-1, keepdims=True)
    var = jnp.mean(jnp.square(o - mu), axis=-1, keepdims=True)
    o = (o - mu) * lax.rsqrt(var + EPS) * gn_w.astype(f32).reshape(H, dv)
    y = jax.nn.silu(gate.astype(f32)) * o.reshape(B, S, H * dv)
    return (y @ w_o).astype(h.dtype)


def setup_inputs(seed: int = 0) -> dict:
    key = jax.random.key(seed)
    ks = list(jax.random.split(key, 32))
    f32 = jnp.float32
    n_na, n_gdn, n_mla, n_ret = [len(range(m, DEPTH, N_MIXERS)) for m in range(N_MIXERS)]

    def dense(k, n, fan_in, fan_out):
        return jax.random.normal(k, (n, fan_in, fan_out), f32) * fan_in ** -0.5

    def gain(k, shape):
        return 1.0 + 0.02 * jax.random.normal(k, shape, f32)

    def a_log(k, n):
        return jnp.log(jax.random.uniform(k, (n, GDN_HEADS), f32, 1.0, 16.0))

    def dt_bias(k, n):
        dt = jnp.exp(jax.random.uniform(k, (n, GDN_HEADS), f32, math.log(1e-3), math.log(1e-1)))
        return dt + jnp.log(-jnp.expm1(-dt))

    gdn_qkv = 2 * GDN_HEADS * GDN_DK + GDN_HEADS * GDN_DV
    return {
        'x': jax.random.normal(ks[0], (BATCH, SEQ, D_MODEL), f32),
        'na_norm': gain(ks[1], (n_na, D_MODEL)),
        'na_w_qkv': dense(ks[2], n_na, D_MODEL, 3 * NA_HEADS * NA_HEAD_DIM),
        'na_rpb': 0.05 * jax.random.normal(ks[3], (n_na, NA_HEADS, 2 * NA_WIN_ROWS - 1, 2 * NA_WIN_COLS - 1), f32),
        'na_w_o': dense(ks[4], n_na, NA_HEADS * NA_HEAD_DIM, D_MODEL),
        'gdn_norm': gain(ks[5], (n_gdn, D_MODEL)),
        'gdn_w_in': dense(ks[6], n_gdn, D_MODEL, gdn_qkv + GDN_HEADS * GDN_DV + 4 * GDN_HEADS),
        'gdn_conv': jax.random.normal(ks[7], (n_gdn, gdn_qkv, GDN_CONV), f32) * GDN_CONV ** -0.5,
        'gdn_a_log_f': a_log(ks[8], n_gdn),
        'gdn_a_log_b': a_log(ks[9], n_gdn),
        'gdn_dt_bias_f': dt_bias(ks[10], n_gdn),
        'gdn_dt_bias_b': dt_bias(ks[11], n_gdn),
        'gdn_o_norm': gain(ks[12], (n_gdn, GDN_DV)),
        'gdn_w_o': dense(ks[13], n_gdn, GDN_HEADS * GDN_DV, D_MODEL),
        'mla_norm': gain(ks[14], (n_mla, D_MODEL)),
        'mla_w_in': dense(ks[15], n_mla, D_MODEL, MLA_Q_RANK + MLA_KV_RANK + MLA_ROPE),
        'mla_q_norm': gain(ks[16], (n_mla, MLA_Q_RANK)),
        'mla_w_uq': dense(ks[17], n_mla, MLA_Q_RANK, MLA_HEADS * (MLA_NOPE + MLA_ROPE)),
        'mla_kv_norm': gain(ks[18], (n_mla, MLA_KV_RANK)),
        'mla_w_ukv': dense(ks[19], n_mla, MLA_KV_RANK, MLA_HEADS * (MLA_NOPE + MLA_V)),
        'mla_w_o': dense(ks[20], n_mla, MLA_HEADS * MLA_V, D_MODEL),
        'ret_norm': gain(ks[21], (n_ret, D_MODEL)),
        'ret_w_in': dense(ks[22], n_ret, D_MODEL, 2 * RET_HEADS * RET_DK + 2 * RET_HEADS * RET_DV),
        'ret_gn': gain(ks[23], (n_ret, RET_HEADS * RET_DV)),
        'ret_w_o': dense(ks[24], n_ret, RET_HEADS * RET_DV, D_MODEL),
        'mlp_norm': gain(ks[25], (DEPTH, D_MODEL)),
        'mlp_w1': dense(ks[26], DEPTH, D_MODEL, D_FF),
        'mlp_w2': dense(ks[27], DEPTH, D_FF, D_MODEL),
        'final_norm': gain(ks[28], (D_MODEL,)),
    }


def reference(x, na_norm, na_w_qkv, na_rpb, na_w_o,
              gdn_norm, gdn_w_in, gdn_conv, gdn_a_log_f, gdn_a_log_b, gdn_dt_bias_f, gdn_dt_bias_b,
              gdn_o_norm, gdn_w_o,
              mla_norm, mla_w_in, mla_q_norm, mla_w_uq, mla_kv_norm, mla_w_ukv, mla_w_o,
              ret_norm, ret_w_in, ret_gn, ret_w_o,
              mlp_norm, mlp_w1, mlp_w2, final_norm):
    S = x.shape[1]
    pos = jnp.arange(S, dtype=jnp.int32)
    h = x
    for i in range(DEPTH):
        m, j = i % N_MIXERS, i // N_MIXERS
        if m == 0:
            h = h + neighbourhood_attention(rmsnorm(h, na_norm[j]), na_w_qkv[j], na_rpb[j], na_w_o[j])
        elif m == 1:
            h = h + gated_deltanet(rmsnorm(h, gdn_norm[j]), gdn_w_in[j], gdn_conv[j], gdn_a_log_f[j],
                                   gdn_a_log_b[j], gdn_dt_bias_f[j], gdn_dt_bias_b[j], gdn_o_norm[j], gdn_w_o[j])
        elif m == 2:
            h = h + latent_attention(rmsnorm(h, mla_norm[j]), mla_w_in[j], mla_q_norm[j], mla_w_uq[j],
                                     mla_kv_norm[j], mla_w_ukv[j], mla_w_o[j], pos)
        else:
            h = h + retention(rmsnorm(h, ret_norm[j]), ret_w_in[j], ret_gn[j], ret_w_o[j], pos)
        h = h + sq_relu_mlp(rmsnorm(h, mlp_norm[i]), mlp_w1[i], mlp_w2[i]).astype(h.dtype)
    return rmsnorm(h, final_norm)
```

```python
import functools
import math

import numpy as np
import jax
import jax.numpy as jnp
from jax import lax
from jax.experimental import pallas as pl
from jax.experimental.pallas import tpu as pltpu

F32 = jnp.float32
BF16 = jnp.bfloat16
EPS = 1e-6
ROPE_THETA = 10000.0
GRID_W = 64
NEG_BIG = -1e30
HI = lax.Precision.HIGHEST

NA_HEADS, NA_DH, NA_WIN_ROWS, NA_WIN_COLS, NA_Q_ROWS = 16, 64, 8, 16, 2
NA_KBLK = 5
GDN_HEADS, GDN_DK, GDN_CONV, GDN_CHUNK = 8, 128, 5, 64
MLA_HEADS, MLA_Q_RANK, MLA_KV_RANK, MLA_NOPE, MLA_ROPE, MLA_V = 16, 768, 256, 64, 32, 64
RET_HEADS, RET_DK, RET_DV, RET_CHUNK = 8, 128, 256, 128

VMEM_LIMIT = 52 * 1024 * 1024


def _cparams(*sem):
    return pltpu.CompilerParams(dimension_semantics=sem, vmem_limit_bytes=VMEM_LIMIT)


def _rms(x, g):
    return x * lax.rsqrt(jnp.mean(x * x, axis=-1, keepdims=True) + EPS) * g


def _sigmoid(x):
    return 1.0 / (1.0 + jnp.exp(-x))


def _dot(a, b, **kw):
    return jnp.dot(a, b, preferred_element_type=F32, **kw)


def _dot_nt(a, b, **kw):
    return lax.dot_general(a, b, (((1,), (1,)), ((), ())), preferred_element_type=F32, **kw)


def _dot_tn(a, b, **kw):
    return lax.dot_general(a, b, (((0,), (0,)), ((), ())), preferred_element_type=F32, **kw)


def _norm_matmul_body(x_ref, g_ref, w_ref, o_ref, xn_ref):
    @pl.when(pl.program_id(1) == 0)
    def _():
        xn_ref[...] = _rms(x_ref[...], g_ref[...]).astype(BF16)

    o_ref[...] = _dot(xn_ref[...], w_ref[...]).astype(o_ref.dtype)


def norm_matmul(x, g, w, out_dtype, tm, tn, name):
    s, d = x.shape
    n = w.shape[1]
    return pl.pallas_call(
        _norm_matmul_body,
        grid=(s // tm, n // tn),
        in_specs=[pl.BlockSpec((tm, d), lambda i, j: (i, 0)),
                  pl.BlockSpec((1, d), lambda i, j: (0, 0)),
                  pl.BlockSpec((d, tn), lambda i, j: (0, j))],
        out_specs=pl.BlockSpec((tm, tn), lambda i, j: (i, j)),
        out_shape=jax.ShapeDtypeStruct((s, n), out_dtype),
        scratch_shapes=[pltpu.VMEM((tm, d), BF16)],
        compiler_params=_cparams("arbitrary", "arbitrary"),
        name=name,
    )(x, g.reshape(1, d), w)


def _matmul_res_body(a_ref, w_ref, r_ref, o_ref):
    o_ref[...] = r_ref[...] + _dot(a_ref[...], w_ref[...])


def matmul_residual(a, w, res, tm, name):
    s, k = a.shape
    n = w.shape[1]
    return pl.pallas_call(
        _matmul_res_body,
        grid=(s // tm,),
        in_specs=[pl.BlockSpec((tm, k), lambda i: (i, 0)),
                  pl.BlockSpec((k, n), lambda i: (0, 0)),
                  pl.BlockSpec((tm, n), lambda i: (i, 0))],
        out_specs=pl.BlockSpec((tm, n), lambda i: (i, 0)),
        out_shape=jax.ShapeDtypeStruct((s, n), F32),
        compiler_params=_cparams("arbitrary"),
        name=name,
    )(a, w, res)


def _mlp_body(x_ref, g_ref, w1_ref, w2_ref, fg_ref, o_ref, xn_ref, acc_ref, *, nk, final):
    k = pl.program_id(1)

    @pl.when(k == 0)
    def _():
        xn_ref[...] = _rms(x_ref[...], g_ref[...]).astype(BF16)
        acc_ref[...] = jnp.zeros_like(acc_ref)

    a = jnp.maximum(_dot(xn_ref[...], w1_ref[...]), 0.0)
    acc_ref[...] += _dot((a * a).astype(BF16), w2_ref[...])

    @pl.when(k == nk - 1)
    def _():
        y = x_ref[...] + acc_ref[...]
        if final:
            y = _rms(y, fg_ref[...])
        o_ref[...] = y


def mlp_block(x, g, w1, w2, final_g, final, tm, tf, name):
    s, d = x.shape
    f = w1.shape[1]
    nk = f // tf
    return pl.pallas_call(
        functools.partial(_mlp_body, nk=nk, final=final),
        grid=(s // tm, nk),
        in_specs=[pl.BlockSpec((tm, d), lambda i, k: (i, 0)),
                  pl.BlockSpec((1, d), lambda i, k: (0, 0)),
                  pl.BlockSpec((d, tf), lambda i, k: (0, k)),
                  pl.BlockSpec((tf, d), lambda i, k: (k, 0)),
                  pl.BlockSpec((1, d), lambda i, k: (0, 0))],
        out_specs=pl.BlockSpec((tm, d), lambda i, k: (i, 0)),
        out_shape=jax.ShapeDtypeStruct((s, d), F32),
        scratch_shapes=[pltpu.VMEM((tm, d), BF16), pltpu.VMEM((tm, d), F32)],
        compiler_params=_cparams("arbitrary", "arbitrary"),
        name=name,
    )(x, g.reshape(1, d), w1, w2, final_g.reshape(1, d))


def _na_bias_table(rpb, rows):
    n_qblk = rows // NA_Q_ROWS
    reps = {4: 0, 3: 1, 2: 2, 1: n_qblk - 2, 0: n_qblk - 1}
    nq = NA_Q_ROWS * GRID_W
    nk = NA_KBLK * NA_Q_ROWS * GRID_W
    qi, ki = np.arange(nq), np.arange(nk)
    drow_l, dcol_l, ok_l = [], [], []
    for t in range(5):
        i = reps[t]
        base = int(np.clip(i - 2, 0, n_qblk - NA_KBLK))
        qrow = NA_Q_ROWS * i + qi // GRID_W
        qcol = qi % GRID_W
        krow = NA_Q_ROWS * base + ki // GRID_W
        kcol = ki % GRID_W
        qr0 = np.clip(qrow - NA_WIN_ROWS // 2, 0, rows - NA_WIN_ROWS)
        qc0 = np.clip(qcol - NA_WIN_COLS // 2, 0, GRID_W - NA_WIN_COLS)
        ok = ((krow[None, :] >= qr0[:, None]) & (krow[None, :] < qr0[:, None] + NA_WIN_ROWS)
              & (kcol[None, :] >= qc0[:, None]) & (kcol[None, :] < qc0[:, None] + NA_WIN_COLS))
        drow = np.clip(krow[None, :] - qrow[:, None] + NA_WIN_ROWS - 1, 0, 2 * NA_WIN_ROWS - 2)
        dcol = np.clip(kcol[None, :] - qcol[:, None] + NA_WIN_COLS - 1, 0, 2 * NA_WIN_COLS - 2)
        drow_l.append(drow), dcol_l.append(dcol), ok_l.append(ok)
    drow, dcol, ok = np.stack(drow_l), np.stack(dcol_l), np.stack(ok_l)
    bias = rpb.astype(F32)[:, drow, dcol]
    bias = jnp.where(ok[None], bias, NEG_BIG)
    return jnp.transpose(bias, (1, 0, 2, 3))


def _na_body(q_ref, k0, k1, k2, k3, k4, v0, v1, v2, v3, v4, b_ref, o_ref):
    krefs = (k0, k1, k2, k3, k4)
    vrefs = (v0, v1, v2, v3, v4)
    scale = NA_DH ** -0.5
    nkb = NA_Q_ROWS * GRID_W
    for h in range(NA_HEADS):
        sl = slice(h * NA_DH, (h + 1) * NA_DH)
        qh = q_ref[:, sl]
        s = jnp.concatenate([_dot_nt(qh, kr[:, sl]) for kr in krefs], axis=1)
        s = s * scale + b_ref[0, h]
        m = jnp.max(s, axis=1, keepdims=True)
        p = jnp.exp(s - m)
        l = jnp.sum(p, axis=1, keepdims=True)
        pb = p.astype(BF16)
        o = _dot(pb[:, 0:nkb], vrefs[0][:, sl])
        for d in range(1, NA_KBLK):
            o = o + _dot(pb[:, d * nkb:(d + 1) * nkb], vrefs[d][:, sl])
        o_ref[:, sl] = (o / l).astype(o_ref.dtype)


def na_attention(qkv, bias_tab):
    s = qkv.shape[0]
    d = NA_HEADS * NA_DH
    tq = NA_Q_ROWS * GRID_W
    n_qblk = s // tq
    top = n_qblk - NA_KBLK

    def base(i):
        return jnp.clip(i - 2, 0, top)

    kv_specs = [pl.BlockSpec((tq, d), functools.partial(lambda i, dd, c: (base(i) + dd, c), dd=dd, c=c))
                for c in (1, 2) for dd in range(NA_KBLK)]
    return pl.pallas_call(
        _na_body,
        grid=(n_qblk,),
        in_specs=[pl.BlockSpec((tq, d), lambda i: (i, 0))] + kv_specs
                 + [pl.BlockSpec((1, NA_HEADS, tq, NA_KBLK * tq), lambda i: (base(i) - i + 4, 0, 0, 0))],
        out_specs=pl.BlockSpec((tq, d), lambda i: (i, 0)),
        out_shape=jax.ShapeDtypeStruct((s, d), BF16),
        compiler_params=_cparams("arbitrary"),
        name="na_attention",
    )(qkv, *([qkv] * (2 * NA_KBLK)), bias_tab)


def _gdn_conv_body(x_ref, xp_ref, xn_ref, w_ref, o_ref, *, tm, nt):
    i = pl.program_id(0)
    j = pl.program_id(1)
    prev = jnp.where(i > 0, xp_ref[...], 0.0)
    nxt = jnp.where(i < nt - 1, xn_ref[...], 0.0)
    xe = jnp.concatenate([prev, x_ref[...], nxt], axis=0)
    w = w_ref[...]
    half = GDN_CONV // 2
    acc = xe[8 - half:8 - half + tm] * w[0:1]
    for t in range(1, GDN_CONV):
        acc = acc + xe[8 - half + t:8 - half + t + tm] * w[t:t + 1]
    y = acc * _sigmoid(acc)
    for h in range(GDN_HEADS):
        sl = slice(h * GDN_DK, (h + 1) * GDN_DK)
        yh = y[:, sl]
        nrm = lax.rsqrt(jnp.sum(yh * yh, axis=1, keepdims=True) + EPS)
        fac = jnp.where(j == 0, nrm * GDN_DK ** -0.5, jnp.where(j == 1, nrm, 1.0))
        o_ref[:, sl] = (yh * fac).astype(o_ref.dtype)


def gdn_conv(proj, conv_w_t, tm):
    s = proj.shape[0]
    d = GDN_HEADS * GDN_DK
    nt = s // tm
    r8 = tm // 8
    return pl.pallas_call(
        functools.partial(_gdn_conv_body, tm=tm, nt=nt),
        grid=(nt, 3),
        in_specs=[pl.BlockSpec((tm, d), lambda i, j: (i, j)),
                  pl.BlockSpec((8, d), lambda i, j: (jnp.maximum(i * r8 - 1, 0), j)),
                  pl.BlockSpec((8, d), lambda i, j: (jnp.minimum((i + 1) * r8, nt * r8 - 1), j)),
                  pl.BlockSpec((GDN_CONV, d), lambda i, j: (0, j))],
        out_specs=pl.BlockSpec((tm, d), lambda i, j: (i, j)),
        out_shape=jax.ShapeDtypeStruct((s, 3 * d), BF16),
        compiler_params=_cparams("arbitrary", "arbitrary"),
        name="gdn_conv",
    )(proj, proj, proj, conv_w_t)


def _gdn_gates_body(gp_ref, alog_ref, dtb_ref, col_ref, row_ref, *, tm):
    c = GDN_CHUNK
    gp = gp_ref[...]
    lane = lax.broadcasted_iota(jnp.int32, (tm, 128), 1)
    is_f = (lane >= 16) & (lane < 24)
    is_b = (lane >= 24) & (lane < 32)
    beta = _sigmoid(gp)
    z = gp + dtb_ref[...]
    softplus = jnp.maximum(z, 0.0) + jnp.log1p(jnp.exp(-jnp.abs(z)))
    g = jnp.where(is_f | is_b, -jnp.exp(alog_ref[...]) * softplus, 0.0)
    r = lax.broadcasted_iota(jnp.int32, (c, c), 0)
    cc = lax.broadcasted_iota(jnp.int32, (c, c), 1)
    tri_lo = (r >= cc).astype(F32)
    tri_up = (r <= cc).astype(F32)
    lane_c = lax.broadcasted_iota(jnp.int32, (c, 128), 1)
    is_f_c = (lane_c >= 16) & (lane_c < 24)
    parts = []
    for ch in range(tm // c):
        gc = g[ch * c:(ch + 1) * c]
        parts.append(jnp.where(is_f_c, _dot(tri_lo, gc, precision=HI), _dot(tri_up, gc, precision=HI)))
    cum = jnp.concatenate(parts, axis=0)
    col = jnp.where(is_f | is_b, cum, beta)
    col_ref[...] = col
    row_ref[...] = col.T


def gdn_gates(gp, alog_vec, dtb_vec, tm):
    s = gp.shape[0]
    return pl.pallas_call(
        functools.partial(_gdn_gates_body, tm=tm),
        grid=(s // tm,),
        in_specs=[pl.BlockSpec((tm, 128), lambda i: (i, 0)),
                  pl.BlockSpec((1, 128), lambda i: (0, 0)),
                  pl.BlockSpec((1, 128), lambda i: (0, 0))],
        out_specs=[pl.BlockSpec((tm, 128), lambda i: (i, 0)),
                   pl.BlockSpec((128, tm), lambda i: (0, i))],
        out_shape=[jax.ShapeDtypeStruct((s, 128), F32), jax.ShapeDtypeStruct((128, s), F32)],
        compiler_params=_cparams("arbitrary"),
        name="gdn_gates",
    )(gp, alog_vec, dtb_vec)


def _unit_tri_inverse(lmat, r, cc):
    eye = (r == cc).astype(F32)
    diag_blk = (r // 16) == (cc // 16)
    ld = jnp.where(diag_blk, lmat, 0.0)
    lo = lmat - ld
    x = eye - ld
    p = _dot(ld, ld)
    x = x + _dot(x, p)
    p = _dot(p, p)
    x = x + _dot(x, p)
    p = _dot(p, p)
    dinv = x + _dot(x, p)
    n1 = _dot(dinv, lo)
    n2 = _dot(n1, n1)
    y = eye - n1 + n2 - _dot(n1, n2)
    return _dot(y, dinv)


def _gdn_direction(q_ref, k_ref, v_ref, col_ref, row_ref, o_ref, st_ref, h, tb, reverse):
    c = GDN_CHUNK
    lane = lax.broadcasted_iota(jnp.int32, (tb, 128), 1)
    col = col_ref[...]
    b_lane = h + (8 if reverse else 0)
    g_lane = h + (24 if reverse else 16)
    beta_col = jnp.sum(jnp.where(lane == b_lane, col, 0.0), axis=1, keepdims=True)
    g_col = jnp.sum(jnp.where(lane == g_lane, col, 0.0), axis=1, keepdims=True)
    g_row = row_ref[pl.ds(g_lane, 1), :]
    r = lax.broadcasted_iota(jnp.int32, (c, c), 0)
    cc = lax.broadcasted_iota(jnp.int32, (c, c), 1)
    incl = (r <= cc) if reverse else (r >= cc)
    strict = (r < cc) if reverse else (r > cc)
    state = st_ref[...]
    nc = tb // c
    for ch in (range(nc - 1, -1, -1) if reverse else range(nc)):
        sl = slice(ch * c, (ch + 1) * c)
        last = ch * c if reverse else ch * c + c - 1
        qc = q_ref[sl, :].astype(F32)
        kc = k_ref[sl, :].astype(F32)
        vc = v_ref[sl, :].astype(F32)
        bc, gc, gr = beta_col[sl], g_col[sl], g_row[:, sl]
        g_last = g_row[:, last:last + 1]
        decay = jnp.where(incl, jnp.exp(jnp.where(incl, gc - gr, 0.0)), 0.0)
        eg = jnp.exp(gc)
        kb = kc * bc
        kc16 = kc.astype(BF16)
        lmat = jnp.where(strict, _dot_nt(kb.astype(BF16), kc16) * decay, 0.0)
        tinv = _unit_tri_inverse(lmat, r, cc)
        rhs = jnp.concatenate([vc * bc, kb * eg], axis=1)
        sol = _dot(tinv, rhs)
        u, w = sol[:, :GDN_DK], sol[:, GDN_DK:]
        a_intra = jnp.where(incl, _dot_nt(qc.astype(BF16), kc16) * decay, 0.0)
        q_dec = (qc * eg).astype(BF16)
        k_dec = (kc * jnp.exp(g_last - gc)).astype(BF16)
        st16 = state.astype(BF16)
        v_new = u - _dot(w.astype(BF16), st16)
        vn16 = v_new.astype(BF16)
        o_ref[sl, :] = _dot(q_dec, st16) + _dot(a_intra.astype(BF16), vn16)
        state = state * jnp.exp(g_last) + _dot_tn(k_dec, vn16)
    st_ref[...] = state


def _gdn_scan_body(qf, kf, vf, colf, rowf, qb, kb, vb, colb, rowb, of_ref, ob_ref, sf_ref, sb_ref, *, tb):
    h = pl.program_id(0)

    @pl.when(pl.program_id(1) == 0)
    def _():
        sf_ref[...] = jnp.zeros_like(sf_ref)
        sb_ref[...] = jnp.zeros_like(sb_ref)

    _gdn_direction(qf, kf, vf, colf, rowf, of_ref, sf_ref, h, tb, False)
    _gdn_direction(qb, kb, vb, colb, rowb, ob_ref, sb_ref, h, tb, True)


def gdn_scan(qkv, gcol, grow, tb):
    s = qkv.shape[0]
    hh, dk = GDN_HEADS, GDN_DK
    nb = s // tb

    def dir_specs(blk):
        return [pl.BlockSpec((tb, dk), lambda h, t: (blk(t), h)),
                pl.BlockSpec((tb, dk), lambda h, t: (blk(t), hh + h)),
                pl.BlockSpec((tb, dk), lambda h, t: (blk(t), 2 * hh + h)),
                pl.BlockSpec((tb, 128), lambda h, t: (blk(t), 0)),
                pl.BlockSpec((128, tb), lambda h, t: (0, blk(t)))]

    fwd = lambda t: t
    bwd = lambda t: nb - 1 - t
    return pl.pallas_call(
        functools.partial(_gdn_scan_body, tb=tb),
        grid=(hh, nb),
        in_specs=dir_specs(fwd) + dir_specs(bwd),
        out_specs=[pl.BlockSpec((tb, dk), lambda h, t: (t, h)),
                   pl.BlockSpec((tb, dk), lambda h, t: (nb - 1 - t, h))],
        out_shape=[jax.ShapeDtypeStruct((s, hh * dk), F32)] * 2,
        scratch_shapes=[pltpu.VMEM((dk, dk), F32), pltpu.VMEM((dk, dk), F32)],
        compiler_params=_cparams("arbitrary", "arbitrary"),
        name="gdn_scan",
    )(qkv, qkv, qkv, gcol, grow, qkv, qkv, qkv, gcol, grow)


def _gdn_post_body(of_ref, ob_ref, z_ref, on_ref, w_ref, r_ref, o_ref):
    on = on_ref[...]
    parts = []
    for h in range(GDN_HEADS):
        sl = slice(h * GDN_DK, (h + 1) * GDN_DK)
        o = of_ref[:, sl] + ob_ref[:, sl]
        z = z_ref[:, sl]
        parts.append((_rms(o, on) * (z * _sigmoid(z))).astype(BF16))
    y = jnp.concatenate(parts, axis=1)
    o_ref[...] = r_ref[...] + _dot(y, w_ref[...])


def gdn_post(o_f, o_b, proj, o_norm, w_o, res, tm):
    s, d = o_f.shape
    return pl.pallas_call(
        _gdn_post_body,
        grid=(s // tm,),
        in_specs=[pl.BlockSpec((tm, d), lambda i: (i, 0)),
                  pl.BlockSpec((tm, d), lambda i: (i, 0)),
                  pl.BlockSpec((tm, d), lambda i: (i, 3)),
                  pl.BlockSpec((1, GDN_DK), lambda i: (0, 0)),
                  pl.BlockSpec((d, d), lambda i: (0, 0)),
                  pl.BlockSpec((tm, d), lambda i: (i, 0))],
        out_specs=pl.BlockSpec((tm, d), lambda i: (i, 0)),
        out_shape=jax.ShapeDtypeStruct((s, d), F32),
        compiler_params=_cparams("arbitrary"),
        name="gdn_post",
    )(o_f, o_b, proj, o_norm.reshape(1, GDN_DK), w_o, res)


def _mla_pre_body(x_ref, g_ref, win_ref, qn_ref, kvn_ref, wuq_ref, wuk_ref, wuv_ref, ct_ref, sn_ref,
                  q_ref, k_ref, v_ref):
    xn = _rms(x_ref[...], g_ref[...]).astype(BF16)
    proj = _dot(xn, win_ref[...])
    cq = _rms(proj[:, :MLA_Q_RANK], qn_ref[...]).astype(BF16)
    ckv = _rms(proj[:, MLA_Q_RANK:MLA_Q_RANK + MLA_KV_RANK], kvn_ref[...]).astype(BF16)
    pr = proj[:, MLA_Q_RANK + MLA_KV_RANK:]
    ct, sn = ct_ref[...], sn_ref[...]
    scale = (MLA_NOPE + MLA_ROPE) ** -0.5
    q = _dot(cq, wuq_ref[...])
    for h in range(MLA_HEADS):
        sl = slice(h * 128, (h + 1) * 128)
        qh = q[:, sl]
        q_ref[:, sl] = ((qh * ct + pltpu.roll(qh, 64, 1) * sn) * scale).astype(BF16)
    kr = pr * ct + pltpu.roll(pr, 64, 1) * sn
    k = _dot(ckv, wuk_ref[...])
    for h in range(MLA_HEADS):
        sl = slice(h * 128, (h + 1) * 128)
        k_ref[:, sl] = (k[:, sl] + kr).astype(BF16)
    v_ref[...] = _dot(ckv, wuv_ref[...]).astype(BF16)


def mla_pre(x, g, win_p, q_norm, kv_norm, wuq_p, wuk_p, wuv, ct, sn, tm):
    s, d = x.shape
    hp = MLA_HEADS * 128
    hv = MLA_HEADS * 128
    full = lambda a: pl.BlockSpec(a.shape, lambda i: (0,) * a.ndim)
    g2, qn2, kvn2 = g.reshape(1, d), q_norm.reshape(1, -1), kv_norm.reshape(1, -1)
    return pl.pallas_call(
        _mla_pre_body,
        grid=(s // tm,),
        in_specs=[pl.BlockSpec((tm, d), lambda i: (i, 0)), full(g2), full(win_p), full(qn2), full(kvn2),
                  full(wuq_p), full(wuk_p), full(wuv),
                  pl.BlockSpec((tm, 128), lambda i: (i, 0)), pl.BlockSpec((tm, 128), lambda i: (i, 0))],
        out_specs=[pl.BlockSpec((tm, hp), lambda i: (i, 0)), pl.BlockSpec((tm, hp), lambda i: (i, 0)),
                   pl.BlockSpec((tm, hv), lambda i: (i, 0))],
        out_shape=[jax.ShapeDtypeStruct((s, hp), BF16), jax.ShapeDtypeStruct((s, hp), BF16),
                   jax.ShapeDtypeStruct((s, hv), BF16)],
        compiler_params=_cparams("arbitrary"),
        name="mla_pre",
    )(x, g2, win_p, qn2, kvn2, wuq_p, wuk_p, wuv, ct, sn)


def _mla_attn_body(q_ref, k_ref, v_ref, o_ref, m_ref, l_ref, acc_ref, *, tq, tk, nk):
    m_ref[...] = jnp.full_like(m_ref, -jnp.inf)
    l_ref[...] = jnp.zeros_like(l_ref)
    acc_ref[...] = jnp.zeros_like(acc_ref)
    first = lax.broadcasted_iota(jnp.int32, (1, 128), 1) < MLA_V

    def body(j, carry):
        off = pl.multiple_of(j * tk, tk)
        pv = []
        alphas = []
        for hh in range(2):
            sl = slice(hh * 128, (hh + 1) * 128)
            s = _dot_nt(q_ref[:, sl], k_ref[pl.ds(off, tk), sl])
            m_old = m_ref[hh]
            m_new = jnp.maximum(m_old, jnp.max(s, axis=1, keepdims=True))
            alpha = jnp.exp(m_old - m_new)
            p = jnp.exp(s - m_new)
            l_ref[hh] = alpha * l_ref[hh] + jnp.sum(p, axis=1, keepdims=True)
            m_ref[hh] = m_new
            pv.append(_dot(p.astype(BF16), v_ref[pl.ds(off, tk), sl]))
            alphas.append(alpha)
        acc_ref[...] = acc_ref[...] * jnp.where(first, alphas[0], alphas[1]) + pv[0] + pv[1]
        return carry

    lax.fori_loop(0, nk, body, 0)
    o_ref[...] = (acc_ref[...] / jnp.where(first, l_ref[0], l_ref[1])).astype(o_ref.dtype)


def mla_attention(q, k, v, tq, tk):
    s = q.shape[0]
    npair = MLA_HEADS // 2
    return pl.pallas_call(
        functools.partial(_mla_attn_body, tq=tq, tk=tk, nk=s // tk),
        grid=(npair, s // tq),
        in_specs=[pl.BlockSpec((tq, 256), lambda p, i: (i, p)),
                  pl.BlockSpec((s, 256), lambda p, i: (0, p)),
                  pl.BlockSpec((s, 256), lambda p, i: (0, p))],
        out_specs=pl.BlockSpec((tq, 128), lambda p, i: (i, p)),
        out_shape=jax.ShapeDtypeStruct((s, MLA_HEADS * MLA_V), BF16),
        scratch_shapes=[pltpu.VMEM((2, tq, 1), F32), pltpu.VMEM((2, tq, 1), F32), pltpu.VMEM((tq, 128), F32)],
        compiler_params=_cparams("arbitrary", "arbitrary"),
        name="mla_attention",
    )(q, k, v)


def _ret_direction(q_ref, k_ref, v_ref, ct_ref, sn_ref, dm_ref, qd_ref, kd_ref, gc_ref, o_ref, st_ref):
    ct, sn = ct_ref[...], sn_ref[...]
    for h in range(RET_HEADS):
        sk = slice(h * RET_DK, (h + 1) * RET_DK)
        sv = slice(h * RET_DV, (h + 1) * RET_DV)
        qh = q_ref[:, sk]
        kh = k_ref[:, sk]
        qr = qh * ct + pltpu.roll(qh, 64, 1) * sn
        kr = (kh * ct + pltpu.roll(kh, 64, 1) * sn) * RET_DK ** -0.5
        v16 = v_ref[:, sv].astype(BF16)
        state = st_ref[h]
        inner = _dot_nt(qr.astype(BF16), kr.astype(BF16)) * dm_ref[h]
        o_ref[:, sv] = (_dot(inner.astype(BF16), v16)
                        + _dot((qr * qd_ref[h]).astype(BF16), state.astype(BF16)))
        st_ref[h] = state * gc_ref[h] + _dot_tn((kr * kd_ref[h]).astype(BF16), v16)


def _ret_scan_body(qf, kf, vf, ctf, snf, qb, kb, vb, ctb, snb,
                   dmf, qdf, kdf, gcf, dmb, qdb, kdb, gcb, of_ref, ob_ref, sf_ref, sb_ref):
    @pl.when(pl.program_id(0) == 0)
    def _():
        sf_ref[...] = jnp.zeros_like(sf_ref)
        sb_ref[...] = jnp.zeros_like(sb_ref)

    _ret_direction(qf, kf, vf, ctf, snf, dmf, qdf, kdf, gcf, of_ref, sf_ref)
    _ret_direction(qb, kb, vb, ctb, snb, dmb, qdb, kdb, gcb, ob_ref, sb_ref)


def ret_scan(proj, ct, sn, tabs_f, tabs_b):
    s = proj.shape[0]
    c = RET_CHUNK
    nb = s // c
    dq = RET_HEADS * RET_DK
    dv = RET_HEADS * RET_DV

    def dir_specs(blk):
        return [pl.BlockSpec((c, dq), lambda t: (blk(t), 0)),
                pl.BlockSpec((c, dq), lambda t: (blk(t), 1)),
                pl.BlockSpec((c, dv), lambda t: (blk(t), 1)),
                pl.BlockSpec((c, 128), lambda t: (blk(t), 0)),
                pl.BlockSpec((c, 128), lambda t: (blk(t), 0))]

    full = lambda a: pl.BlockSpec(a.shape, lambda t: (0,) * a.ndim)
    fwd = lambda t: t
    bwd = lambda t: nb - 1 - t
    return pl.pallas_call(
        _ret_scan_body,
        grid=(nb,),
        in_specs=dir_specs(fwd) + dir_specs(bwd) + [full(a) for a in tabs_f] + [full(a) for a in tabs_b],
        out_specs=[pl.BlockSpec((c, dv), lambda t: (t, 0)), pl.BlockSpec((c, dv), lambda t: (nb - 1 - t, 0))],
        out_shape=[jax.ShapeDtypeStruct((s, dv), F32)] * 2,
        scratch_shapes=[pltpu.VMEM((RET_HEADS, RET_DK, RET_DV), F32)] * 2,
        compiler_params=_cparams("arbitrary"),
        name="ret_scan",
    )(proj, proj, proj, ct, sn, proj, proj, proj, ct, sn, *tabs_f, *tabs_b)


def _ret_post_body(of_ref, ob_ref, gate_ref, gn_ref, w_ref, r_ref, o_ref):
    parts = []
    for h in range(RET_HEADS):
        sv = slice(h * RET_DV, (h + 1) * RET_DV)
        o = of_ref[:, sv] + ob_ref[:, sv]
        mu = jnp.mean(o, axis=1, keepdims=True)
        oc = o - mu
        var = jnp.mean(oc * oc, axis=1, keepdims=True)
        gate = gate_ref[:, sv]
        parts.append((gate * _sigmoid(gate) * (oc * lax.rsqrt(var + EPS) * gn_ref[:, sv])).astype(BF16))
    y = jnp.concatenate(parts, axis=1)
    o_ref[...] = r_ref[...] + _dot(y, w_ref[...])


def ret_post(o_f, o_b, proj, gn_w, w_o, res, tm):
    s, dv = o_f.shape
    d = res.shape[1]
    return pl.pallas_call(
        _ret_post_body,
        grid=(s // tm,),
        in_specs=[pl.BlockSpec((tm, dv), lambda i: (i, 0)),
                  pl.BlockSpec((tm, dv), lambda i: (i, 0)),
                  pl.BlockSpec((tm, dv), lambda i: (i, 2)),
                  pl.BlockSpec((1, dv), lambda i: (0, 0)),
                  pl.BlockSpec((dv, d), lambda i: (0, 0)),
                  pl.BlockSpec((tm, d), lambda i: (i, 0))],
        out_specs=pl.BlockSpec((tm, d), lambda i: (i, 0)),
        out_shape=jax.ShapeDtypeStruct((s, d), F32),
        compiler_params=_cparams("arbitrary"),
        name="ret_post",
    )(o_f, o_b, proj, gn_w.reshape(1, dv), w_o, res)


def _na_layer(h, norm, w_qkv, rpb, w_o):
    s = h.shape[0]
    qkv = norm_matmul(h, norm, w_qkv.astype(BF16), BF16, tm=512, tn=1024, name="na_qkv")
    o = na_attention(qkv, _na_bias_table(rpb, s // GRID_W))
    return matmul_residual(o, w_o.astype(BF16), h, tm=512, name="na_out")


def _gdn_layer(h, norm, w_in, conv_w, a_log_f, a_log_b, dt_bias_f, dt_bias_b, o_norm, w_o):
    d = GDN_HEADS * GDN_DK
    n_main = 4 * d
    proj = norm_matmul(h, norm, w_in[:, :n_main].astype(BF16), F32, tm=512, tn=1024, name="gdn_in")
    w_gate = jnp.pad(w_in[:, n_main:], ((0, 0), (0, 128 - 4 * GDN_HEADS))).astype(BF16)
    gp = norm_matmul(h, norm, w_gate, F32, tm=512, tn=128, name="gdn_in_gates")
    z8 = jnp.zeros((2 * GDN_HEADS,), F32)
    pad = jnp.zeros((128 - 4 * GDN_HEADS,), F32)
    alog_vec = jnp.concatenate([z8, a_log_f.astype(F32), a_log_b.astype(F32), pad]).reshape(1, 128)
    dtb_vec = jnp.concatenate([z8, dt_bias_f.astype(F32), dt_bias_b.astype(F32), pad]).reshape(1, 128)
    gcol, grow = gdn_gates(gp, alog_vec, dtb_vec, tm=512)
    qkv = gdn_conv(proj, jnp.transpose(conv_w).astype(F32), tm=256)
    o_f, o_b = gdn_scan(qkv, gcol, grow, tb=256)
    return gdn_post(o_f, o_b, proj, o_norm, w_o.astype(BF16), h, tm=512)


def _mla_lane_maps():
    lane = np.arange(128)
    rope1 = lane < 16
    rope2 = (lane >= 64) & (lane < 80)
    nope_a = (lane >= 16) & (lane < 64)
    nope_b = (lane >= 80) & (lane < 96)
    q_dim = np.where(rope1, MLA_NOPE + lane, np.where(nope_a, lane - 16, np.where(rope2, lane + 16, lane - 32)))
    q_ok = lane < 96
    nope_dim = np.where(nope_a, lane - 16, lane - 32)
    nope_ok = nope_a | nope_b
    rope_dim = np.where(rope1, lane, lane - 64 + 16)
    rope_ok = rope1 | rope2
    return q_dim, q_ok, nope_dim, nope_ok, rope_dim, rope_ok


def _mla_layer(h, norm, w_in, q_norm, w_uq, kv_norm, w_ukv, w_o):
    s = h.shape[0]
    hh = MLA_HEADS
    dqk = MLA_NOPE + MLA_ROPE
    q_dim, q_ok, nope_dim, nope_ok, rope_dim, rope_ok = _mla_lane_maps()
    head = np.arange(hh)[:, None]
    q_cols = (head * dqk + np.where(q_ok, q_dim, 0)[None, :]).reshape(-1)
    wuq_p = jnp.where(np.tile(q_ok, hh)[None, :], w_uq[:, q_cols], 0.0).astype(BF16)
    k_cols = (head * (MLA_NOPE + MLA_V) + np.where(nope_ok, nope_dim, 0)[None, :]).reshape(-1)
    wuk_p = jnp.where(np.tile(nope_ok, hh)[None, :], w_ukv[:, k_cols], 0.0).astype(BF16)
    lane = np.arange(128)[None, :]
    v_ok = (lane // MLA_V) == (head % 2)
    v_cols = (head * (MLA_NOPE + MLA_V) + MLA_NOPE + lane % MLA_V).reshape(-1)
    wuv = jnp.where(v_ok.reshape(-1)[None, :], w_ukv[:, v_cols], 0.0).astype(BF16)
    n_c = MLA_Q_RANK + MLA_KV_RANK
    w_rope = jnp.where(rope_ok[None, :], w_in[:, n_c + np.where(rope_ok, rope_dim, 0)], 0.0)
    win_p = jnp.concatenate([w_in[:, :n_c], w_rope], axis=1).astype(BF16)
    half = MLA_ROPE // 2
    inv = 1.0 / (ROPE_THETA ** (jnp.arange(half, dtype=F32) / half))
    ang = jnp.arange(s, dtype=jnp.int32).astype(F32)[:, None] * inv[None, :]
    cos, sin = jnp.cos(ang), jnp.sin(ang)
    ones = jnp.ones((s, 48), F32)
    zeros = jnp.zeros((s, 48), F32)
    ct = jnp.concatenate([cos, ones, cos, ones], axis=1)
    sn = jnp.concatenate([-sin, zeros, sin, zeros], axis=1)
    q, k, v = mla_pre(h, norm, win_p, q_norm, kv_norm, wuq_p, wuk_p, wuv, ct, sn, tm=256)
    o = mla_attention(q, k, v, tq=min(512, s), tk=min(512, s))
    return matmul_residual(o, w_o.astype(BF16), h, tm=512, name="mla_out")


def _ret_tables(reverse):
    c = RET_CHUNK
    lg = jnp.log1p(-jnp.exp2(-5.0 - jnp.arange(RET_HEADS, dtype=F32)))
    if reverse:
        lg = lg[::-1]
    idx = np.arange(c)
    diff = idx[:, None] - idx[None, :]
    if reverse:
        keep, expo = diff < 0, -diff
        q_pow, k_pow = c - idx, idx
    else:
        keep, expo = diff >= 0, diff
        q_pow, k_pow = idx + 1, c - 1 - idx
    dmask = jnp.where(keep, jnp.exp(jnp.where(keep, expo, 0).astype(F32)[None] * lg[:, None, None]), 0.0)
    q_dec = jnp.exp(jnp.asarray(q_pow, F32)[None, :] * lg[:, None])
    k_dec = jnp.exp(jnp.asarray(k_pow, F32)[None, :] * lg[:, None])
    g_chunk = jnp.exp(c * lg)
    bc = lambda t: jnp.broadcast_to(t[:, :, None], (RET_HEADS, c, RET_DK))
    return [dmask, bc(q_dec), bc(k_dec), jnp.broadcast_to(g_chunk[:, None, None], (RET_HEADS, 1, RET_DV))]


def _ret_layer(h, norm, w_in, gn_w, w_o):
    s = h.shape[0]
    proj = norm_matmul(h, norm, w_in.astype(BF16), F32, tm=512, tn=1024, name="ret_in")
    half = RET_DK // 2
    inv = 1.0 / (ROPE_THETA ** (jnp.arange(half, dtype=F32) / half))
    ang = jnp.arange(s, dtype=jnp.int32).astype(F32)[:, None] * inv[None, :]
    cos, sin = jnp.cos(ang), jnp.sin(ang)
    ct = jnp.concatenate([cos, cos], axis=1)
    sn = jnp.concatenate([-sin, sin], axis=1)
    o_f, o_b = ret_scan(proj, ct, sn, _ret_tables(False), _ret_tables(True))
    return ret_post(o_f, o_b, proj, gn_w, w_o.astype(BF16), h, tm=256)


def kernel(x, na_norm, na_w_qkv, na_rpb, na_w_o, gdn_norm, gdn_w_in, gdn_conv, gdn_a_log_f, gdn_a_log_b, gdn_dt_bias_f, gdn_dt_bias_b, gdn_o_norm, gdn_w_o, mla_norm, mla_w_in, mla_q_norm, mla_w_uq, mla_kv_norm, mla_w_ukv, mla_w_o, ret_norm, ret_w_in, ret_gn, ret_w_o, mlp_norm, mlp_w1, mlp_w2, final_norm):
    b, s, d = x.shape
    depth = mlp_norm.shape[0]
    outs = []
    for bi in range(b):
        h = x[bi].astype(F32)
        for i in range(depth):
            m, j = i % 4, i // 4
            if m == 0:
                h = _na_layer(h, na_norm[j], na_w_qkv[j], na_rpb[j], na_w_o[j])
            elif m == 1:
                h = _gdn_layer(h, gdn_norm[j], gdn_w_in[j], gdn_conv[j], gdn_a_log_f[j], gdn_a_log_b[j],
                               gdn_dt_bias_f[j], gdn_dt_bias_b[j], gdn_o_norm[j], gdn_w_o[j])
            elif m == 2:
                h = _mla_layer(h, mla_norm[j], mla_w_in[j], mla_q_norm[j], mla_w_uq[j], mla_kv_norm[j],
                               mla_w_ukv[j], mla_w_o[j])
            else:
                h = _ret_layer(h, ret_norm[j], ret_w_in[j], ret_gn[j], ret_w_o[j])
            h = mlp_block(h, mlp_norm[i], mlp_w1[i].astype(BF16), mlp_w2[i].astype(BF16), final_norm,
                          final=(i == depth - 1), tm=512, tf=1024, name=f"mlp_{i}")
        outs.append(h)
    return jnp.stack(outs).astype(x.dtype)
```

```python
import functools
import math

import numpy as np
import jax
import jax.numpy as jnp
from jax import lax
from jax.experimental import pallas as pl
from jax.experimental.pallas import tpu as pltpu

F32 = jnp.float32
BF16 = jnp.bfloat16
EPS = 1e-6
ROPE_THETA = 10000.0
GRID_W = 64
NEG_BIG = -1e30
HI = lax.Precision.HIGHEST

NA_HEADS, NA_DH, NA_WIN_ROWS, NA_WIN_COLS, NA_Q_ROWS = 16, 64, 8, 16, 2
NA_KBLK = 5
GDN_HEADS, GDN_DK, GDN_CONV, GDN_CHUNK = 8, 128, 5, 64
MLA_HEADS, MLA_Q_RANK, MLA_KV_RANK, MLA_NOPE, MLA_ROPE, MLA_V = 16, 768, 256, 64, 32, 64
MLA_VX = 80
RET_HEADS, RET_DK, RET_DV, RET_CHUNK = 8, 128, 256, 128

VMEM_LIMIT = 52 * 1024 * 1024


def _cparams(*sem):
    return pltpu.CompilerParams(dimension_semantics=sem, vmem_limit_bytes=VMEM_LIMIT)


def _rms(x, g):
    return x * lax.rsqrt(jnp.mean(x * x, axis=-1, keepdims=True) + EPS) * g


def _sigmoid(x):
    return 1.0 / (1.0 + jnp.exp(-x))


def _dot(a, b, **kw):
    return jnp.dot(a, b, preferred_element_type=F32, **kw)


def _dot_nt(a, b, **kw):
    return lax.dot_general(a, b, (((1,), (1,)), ((), ())), preferred_element_type=F32, **kw)


def _dot_tn(a, b, **kw):
    return lax.dot_general(a, b, (((0,), (0,)), ((), ())), preferred_element_type=F32, **kw)


def _norm_matmul_body(x_ref, g_ref, w_ref, o_ref, xn_ref):
    @pl.when(pl.program_id(1) == 0)
    def _():
        xn_ref[...] = _rms(x_ref[...], g_ref[...]).astype(BF16)

    o_ref[...] = _dot(xn_ref[...], w_ref[...]).astype(o_ref.dtype)


def norm_matmul(x, g, w, out_dtype, tm, tn, name):
    s, d = x.shape
    n = w.shape[1]
    return pl.pallas_call(
        _norm_matmul_body,
        grid=(s // tm, n // tn),
        in_specs=[pl.BlockSpec((tm, d), lambda i, j: (i, 0)),
                  pl.BlockSpec((1, d), lambda i, j: (0, 0)),
                  pl.BlockSpec((d, tn), lambda i, j: (0, j))],
        out_specs=pl.BlockSpec((tm, tn), lambda i, j: (i, j)),
        out_shape=jax.ShapeDtypeStruct((s, n), out_dtype),
        scratch_shapes=[pltpu.VMEM((tm, d), BF16)],
        compiler_params=_cparams("arbitrary", "arbitrary"),
        name=name,
    )(x, g.reshape(1, d), w)


def _matmul_res_body(a_ref, w_ref, r_ref, o_ref):
    o_ref[...] = r_ref[...] + _dot(a_ref[...], w_ref[...])


def matmul_residual(a, w, res, tm, name):
    s, k = a.shape
    n = w.shape[1]
    return pl.pallas_call(
        _matmul_res_body,
        grid=(s // tm,),
        in_specs=[pl.BlockSpec((tm, k), lambda i: (i, 0)),
                  pl.BlockSpec((k, n), lambda i: (0, 0)),
                  pl.BlockSpec((tm, n), lambda i: (i, 0))],
        out_specs=pl.BlockSpec((tm, n), lambda i: (i, 0)),
        out_shape=jax.ShapeDtypeStruct((s, n), F32),
        compiler_params=_cparams("arbitrary"),
        name=name,
    )(a, w, res)


def _mlp_body(x_ref, g_ref, w1_ref, w2_ref, fg_ref, o_ref, xn_ref, acc_ref, *, nk, final):
    k = pl.program_id(1)

    @pl.when(k == 0)
    def _():
        xn_ref[...] = _rms(x_ref[...], g_ref[...]).astype(BF16)
        acc_ref[...] = jnp.zeros_like(acc_ref)

    a = jnp.maximum(_dot(xn_ref[...], w1_ref[...]), 0.0)
    acc_ref[...] += _dot((a * a).astype(BF16), w2_ref[...])

    @pl.when(k == nk - 1)
    def _():
        y = x_ref[...] + acc_ref[...]
        if final:
            y = _rms(y, fg_ref[...])
        o_ref[...] = y


def mlp_block(x, g, w1, w2, final_g, final, tm, tf, name):
    s, d = x.shape
    f = w1.shape[1]
    nk = f // tf
    return pl.pallas_call(
        functools.partial(_mlp_body, nk=nk, final=final),
        grid=(s // tm, nk),
        in_specs=[pl.BlockSpec((tm, d), lambda i, k: (i, 0)),
                  pl.BlockSpec((1, d), lambda i, k: (0, 0)),
                  pl.BlockSpec((d, tf), lambda i, k: (0, k)),
                  pl.BlockSpec((tf, d), lambda i, k: (k, 0)),
                  pl.BlockSpec((1, d), lambda i, k: (0, 0))],
        out_specs=pl.BlockSpec((tm, d), lambda i, k: (i, 0)),
        out_shape=jax.ShapeDtypeStruct((s, d), F32),
        scratch_shapes=[pltpu.VMEM((tm, d), BF16), pltpu.VMEM((tm, d), F32)],
        compiler_params=_cparams("arbitrary", "arbitrary"),
        name=name,
    )(x, g.reshape(1, d), w1, w2, final_g.reshape(1, d))


def _na_bias_table(rpb, rows):
    n_qblk = rows // NA_Q_ROWS
    reps = {4: 0, 3: 1, 2: 2, 1: n_qblk - 2, 0: n_qblk - 1}
    w, wc, wr = GRID_W, NA_WIN_COLS, NA_WIN_ROWS
    pcol = jnp.pad(rpb.astype(F32), ((0, 0), (0, 0), (w - wc, w - wc)), mode="edge")
    e = jnp.stack([pcol[:, :, w - 1 - qc:2 * w - 1 - qc] for qc in range(w)], axis=2)
    qc = np.arange(w)
    qc0 = np.clip(qc - wc // 2, 0, w - wc)
    col_ok = (qc[None, :] >= qc0[:, None]) & (qc[None, :] < qc0[:, None] + wc)
    tabs, oks = [], []
    for t in range(5):
        i = reps[t]
        base = int(np.clip(i - 2, 0, n_qblk - NA_KBLK))
        per_qr, ok_qr = [], []
        for qr in range(NA_Q_ROWS):
            qrow = NA_Q_ROWS * i + qr
            qr0 = int(np.clip(qrow - wr // 2, 0, rows - wr))
            krows = NA_Q_ROWS * base + np.arange(NA_KBLK * NA_Q_ROWS)
            drow = np.clip(krows - qrow + wr - 1, 0, 2 * wr - 2)
            row_ok = (krows >= qr0) & (krows < qr0 + wr)
            per_qr.append(jnp.concatenate([e[:, int(d)] for d in drow], axis=-1))
            ok_qr.append(np.concatenate([col_ok & bool(r) for r in row_ok], axis=-1))
        tabs.append(jnp.concatenate(per_qr, axis=1))
        oks.append(np.concatenate(ok_qr, axis=0))
    return jnp.where(np.stack(oks)[:, None], jnp.stack(tabs), NEG_BIG)


def _na_body(q_ref, k0, k1, k2, k3, k4, v0, v1, v2, v3, v4, b_ref, o_ref):
    krefs = (k0, k1, k2, k3, k4)
    vrefs = (v0, v1, v2, v3, v4)
    scale = NA_DH ** -0.5
    nkb = NA_Q_ROWS * GRID_W
    for h in range(NA_HEADS):
        sl = slice(h * NA_DH, (h + 1) * NA_DH)
        qh = q_ref[:, sl]
        s = jnp.concatenate([_dot_nt(qh, kr[:, sl]) for kr in krefs], axis=1)
        s = s * scale + b_ref[0, h]
        m = jnp.max(s, axis=1, keepdims=True)
        p = jnp.exp(s - m)
        l = jnp.sum(p, axis=1, keepdims=True)
        pb = p.astype(BF16)
        o = _dot(pb[:, 0:nkb], vrefs[0][:, sl])
        for d in range(1, NA_KBLK):
            o = o + _dot(pb[:, d * nkb:(d + 1) * nkb], vrefs[d][:, sl])
        o_ref[:, sl] = (o / l).astype(o_ref.dtype)


def na_attention(qkv, bias_tab):
    s = qkv.shape[0]
    d = NA_HEADS * NA_DH
    tq = NA_Q_ROWS * GRID_W
    n_qblk = s // tq
    top = n_qblk - NA_KBLK

    def base(i):
        return jnp.clip(i - 2, 0, top)

    kv_specs = [pl.BlockSpec((tq, d), functools.partial(lambda i, dd, c: (base(i) + dd, c), dd=dd, c=c))
                for c in (1, 2) for dd in range(NA_KBLK)]
    return pl.pallas_call(
        _na_body,
        grid=(n_qblk,),
        in_specs=[pl.BlockSpec((tq, d), lambda i: (i, 0))] + kv_specs
                 + [pl.BlockSpec((1, NA_HEADS, tq, NA_KBLK * tq), lambda i: (base(i) - i + 4, 0, 0, 0))],
        out_specs=pl.BlockSpec((tq, d), lambda i: (i, 0)),
        out_shape=jax.ShapeDtypeStruct((s, d), BF16),
        compiler_params=_cparams("arbitrary"),
        name="na_attention",
    )(qkv, *([qkv] * (2 * NA_KBLK)), bias_tab)


def _gdn_conv_body(x_ref, xp_ref, xn_ref, w_ref, o_ref, *, tm, nt):
    i = pl.program_id(0)
    j = pl.program_id(1)
    prev = jnp.where(i > 0, xp_ref[...], 0.0)
    nxt = jnp.where(i < nt - 1, xn_ref[...], 0.0)
    xe = jnp.concatenate([prev, x_ref[...], nxt], axis=0)
    w = w_ref[...]
    half = GDN_CONV // 2
    acc = xe[8 - half:8 - half + tm] * w[0:1]
    for t in range(1, GDN_CONV):
        acc = acc + xe[8 - half + t:8 - half + t + tm] * w[t:t + 1]
    y = acc * _sigmoid(acc)
    for h in range(GDN_HEADS):
        sl = slice(h * GDN_DK, (h + 1) * GDN_DK)
        yh = y[:, sl]
        nrm = lax.rsqrt(jnp.sum(yh * yh, axis=1, keepdims=True) + EPS)
        fac = jnp.where(j == 0, nrm * GDN_DK ** -0.5, jnp.where(j == 1, nrm, 1.0))
        o_ref[:, sl] = (yh * fac).astype(o_ref.dtype)


def gdn_conv(proj, conv_w_t, tm):
    s = proj.shape[0]
    d = GDN_HEADS * GDN_DK
    nt = s // tm
    r8 = tm // 8
    return pl.pallas_call(
        functools.partial(_gdn_conv_body, tm=tm, nt=nt),
        grid=(nt, 3),
        in_specs=[pl.BlockSpec((tm, d), lambda i, j: (i, j)),
                  pl.BlockSpec((8, d), lambda i, j: (jnp.maximum(i * r8 - 1, 0), j)),
                  pl.BlockSpec((8, d), lambda i, j: (jnp.minimum((i + 1) * r8, nt * r8 - 1), j)),
                  pl.BlockSpec((GDN_CONV, d), lambda i, j: (0, j))],
        out_specs=pl.BlockSpec((tm, d), lambda i, j: (i, j)),
        out_shape=jax.ShapeDtypeStruct((s, 3 * d), BF16),
        compiler_params=_cparams("arbitrary", "arbitrary"),
        name="gdn_conv",
    )(proj, proj, proj, conv_w_t)


def _gdn_gates_body(gp_ref, alog_ref, dtb_ref, col_ref, row_ref, *, tm):
    c = GDN_CHUNK
    gp = gp_ref[...]
    lane = lax.broadcasted_iota(jnp.int32, (tm, 128), 1)
    is_f = (lane >= 16) & (lane < 24)
    is_b = (lane >= 24) & (lane < 32)
    beta = _sigmoid(gp)
    z = gp + dtb_ref[...]
    softplus = jnp.maximum(z, 0.0) + jnp.log1p(jnp.exp(-jnp.abs(z)))
    g = jnp.where(is_f | is_b, -jnp.exp(alog_ref[...]) * softplus, 0.0)
    r = lax.broadcasted_iota(jnp.int32, (c, c), 0)
    cc = lax.broadcasted_iota(jnp.int32, (c, c), 1)
    tri_lo = (r >= cc).astype(F32)
    tri_up = (r <= cc).astype(F32)
    lane_c = lax.broadcasted_iota(jnp.int32, (c, 128), 1)
    is_f_c = (lane_c >= 16) & (lane_c < 24)
    parts = []
    for ch in range(tm // c):
        gc = g[ch * c:(ch + 1) * c]
        parts.append(jnp.where(is_f_c, _dot(tri_lo, gc, precision=HI), _dot(tri_up, gc, precision=HI)))
    cum = jnp.concatenate(parts, axis=0)
    col = jnp.where(is_f | is_b, cum, beta)
    col_ref[...] = col
    row_ref[...] = col.T


def gdn_gates(gp, alog_vec, dtb_vec, tm):
    s = gp.shape[0]
    return pl.pallas_call(
        functools.partial(_gdn_gates_body, tm=tm),
        grid=(s // tm,),
        in_specs=[pl.BlockSpec((tm, 128), lambda i: (i, 0)),
                  pl.BlockSpec((1, 128), lambda i: (0, 0)),
                  pl.BlockSpec((1, 128), lambda i: (0, 0))],
        out_specs=[pl.BlockSpec((tm, 128), lambda i: (i, 0)),
                   pl.BlockSpec((128, tm), lambda i: (0, i))],
        out_shape=[jax.ShapeDtypeStruct((s, 128), F32), jax.ShapeDtypeStruct((128, s), F32)],
        compiler_params=_cparams("arbitrary"),
        name="gdn_gates",
    )(gp, alog_vec, dtb_vec)


def _unit_tri_inverse(lmat, r, cc):
    eye = (r == cc).astype(F32)
    diag_blk = (r // 16) == (cc // 16)
    ld = jnp.where(diag_blk, lmat, 0.0)
    lo = lmat - ld
    x = eye - ld
    p = _dot(ld, ld)
    x = x + _dot(x, p)
    p = _dot(p, p)
    x = x + _dot(x, p)
    p = _dot(p, p)
    dinv = x + _dot(x, p)
    n1 = _dot(dinv, lo)
    n2 = _dot(n1, n1)
    y = eye - n1 + n2 - _dot(n1, n2)
    return _dot(y, dinv)


def _gdn_direction(q_ref, k_ref, v_ref, col_ref, row_ref, o_ref, st_ref, h, tb, reverse):
    c = GDN_CHUNK
    lane = lax.broadcasted_iota(jnp.int32, (tb, 128), 1)
    col = col_ref[...]
    b_lane = h + (8 if reverse else 0)
    g_lane = h + (24 if reverse else 16)
    beta_col = jnp.sum(jnp.where(lane == b_lane, col, 0.0), axis=1, keepdims=True)
    g_col = jnp.sum(jnp.where(lane == g_lane, col, 0.0), axis=1, keepdims=True)
    g_row = row_ref[pl.ds(g_lane, 1), :]
    r = lax.broadcasted_iota(jnp.int32, (c, c), 0)
    cc = lax.broadcasted_iota(jnp.int32, (c, c), 1)
    incl = (r <= cc) if reverse else (r >= cc)
    strict = (r < cc) if reverse else (r > cc)
    state = st_ref[...]
    nc = tb // c
    for ch in (range(nc - 1, -1, -1) if reverse else range(nc)):
        sl = slice(ch * c, (ch + 1) * c)
        last = ch * c if reverse else ch * c + c - 1
        qc = q_ref[sl, :].astype(F32)
        kc = k_ref[sl, :].astype(F32)
        vc = v_ref[sl, :].astype(F32)
        bc, gc, gr = beta_col[sl], g_col[sl], g_row[:, sl]
        g_last = g_row[:, last:last + 1]
        decay = jnp.where(incl, jnp.exp(jnp.where(incl, gc - gr, 0.0)), 0.0)
        eg = jnp.exp(gc)
        kb = kc * bc
        kc16 = kc.astype(BF16)
        lmat = jnp.where(strict, _dot_nt(kb.astype(BF16), kc16) * decay, 0.0)
        tinv = _unit_tri_inverse(lmat, r, cc)
        rhs = jnp.concatenate([vc * bc, kb * eg], axis=1)
        sol = _dot(tinv, rhs)
        u, w = sol[:, :GDN_DK], sol[:, GDN_DK:]
        a_intra = jnp.where(incl, _dot_nt(qc.astype(BF16), kc16) * decay, 0.0)
        q_dec = (qc * eg).astype(BF16)
        k_dec = (kc * jnp.exp(g_last - gc)).astype(BF16)
        st16 = state.astype(BF16)
        v_new = u - _dot(w.astype(BF16), st16)
        vn16 = v_new.astype(BF16)
        o_ref[sl, :] = _dot(q_dec, st16) + _dot(a_intra.astype(BF16), vn16)
        state = state * jnp.exp(g_last) + _dot_tn(k_dec, vn16)
    st_ref[...] = state


def _gdn_scan_body(qf, kf, vf, colf, rowf, qb, kb, vb, colb, rowb, of_ref, ob_ref, sf_ref, sb_ref, *, tb):
    h = pl.program_id(0)

    @pl.when(pl.program_id(1) == 0)
    def _():
        sf_ref[...] = jnp.zeros_like(sf_ref)
        sb_ref[...] = jnp.zeros_like(sb_ref)

    _gdn_direction(qf, kf, vf, colf, rowf, of_ref, sf_ref, h, tb, False)
    _gdn_direction(qb, kb, vb, colb, rowb, ob_ref, sb_ref, h, tb, True)


def gdn_scan(qkv, gcol, grow, tb):
    s = qkv.shape[0]
    hh, dk = GDN_HEADS, GDN_DK
    nb = s // tb

    def dir_specs(blk):
        return [pl.BlockSpec((tb, dk), lambda h, t: (blk(t), h)),
                pl.BlockSpec((tb, dk), lambda h, t: (blk(t), hh + h)),
                pl.BlockSpec((tb, dk), lambda h, t: (blk(t), 2 * hh + h)),
                pl.BlockSpec((tb, 128), lambda h, t: (blk(t), 0)),
                pl.BlockSpec((128, tb), lambda h, t: (0, blk(t)))]

    fwd = lambda t: t
    bwd = lambda t: nb - 1 - t
    return pl.pallas_call(
        functools.partial(_gdn_scan_body, tb=tb),
        grid=(hh, nb),
        in_specs=dir_specs(fwd) + dir_specs(bwd),
        out_specs=[pl.BlockSpec((tb, dk), lambda h, t: (t, h)),
                   pl.BlockSpec((tb, dk), lambda h, t: (nb - 1 - t, h))],
        out_shape=[jax.ShapeDtypeStruct((s, hh * dk), F32)] * 2,
        scratch_shapes=[pltpu.VMEM((dk, dk), F32), pltpu.VMEM((dk, dk), F32)],
        compiler_params=_cparams("arbitrary", "arbitrary"),
        name="gdn_scan",
    )(qkv, qkv, qkv, gcol, grow, qkv, qkv, qkv, gcol, grow)


def _gdn_post_body(of_ref, ob_ref, z_ref, on_ref, w_ref, r_ref, o_ref):
    on = on_ref[...]
    parts = []
    for h in range(GDN_HEADS):
        sl = slice(h * GDN_DK, (h + 1) * GDN_DK)
        o = of_ref[:, sl] + ob_ref[:, sl]
        z = z_ref[:, sl]
        parts.append((_rms(o, on) * (z * _sigmoid(z))).astype(BF16))
    y = jnp.concatenate(parts, axis=1)
    o_ref[...] = r_ref[...] + _dot(y, w_ref[...])


def gdn_post(o_f, o_b, proj, o_norm, w_o, res, tm):
    s, d = o_f.shape
    return pl.pallas_call(
        _gdn_post_body,
        grid=(s // tm,),
        in_specs=[pl.BlockSpec((tm, d), lambda i: (i, 0)),
                  pl.BlockSpec((tm, d), lambda i: (i, 0)),
                  pl.BlockSpec((tm, d), lambda i: (i, 3)),
                  pl.BlockSpec((1, GDN_DK), lambda i: (0, 0)),
                  pl.BlockSpec((d, d), lambda i: (0, 0)),
                  pl.BlockSpec((tm, d), lambda i: (i, 0))],
        out_specs=pl.BlockSpec((tm, d), lambda i: (i, 0)),
        out_shape=jax.ShapeDtypeStruct((s, d), F32),
        compiler_params=_cparams("arbitrary"),
        name="gdn_post",
    )(o_f, o_b, proj, o_norm.reshape(1, GDN_DK), w_o, res)


def _mla_pre_body(x_ref, g_ref, win_ref, qn_ref, kvn_ref, wuq_ref, wuk_ref, wuv_ref, vone_ref, ct_ref, sn_ref,
                  q_ref, k_ref, v_ref):
    xn = _rms(x_ref[...], g_ref[...]).astype(BF16)
    proj = _dot(xn, win_ref[...])
    cq = _rms(proj[:, :MLA_Q_RANK], qn_ref[...]).astype(BF16)
    ckv = _rms(proj[:, MLA_Q_RANK:MLA_Q_RANK + MLA_KV_RANK], kvn_ref[...]).astype(BF16)
    pr = proj[:, MLA_Q_RANK + MLA_KV_RANK:]
    ct, sn = ct_ref[...], sn_ref[...]
    scale = (MLA_NOPE + MLA_ROPE) ** -0.5 * math.log2(math.e)
    q = _dot(cq, wuq_ref[...])
    for h in range(MLA_HEADS):
        sl = slice(h * 128, (h + 1) * 128)
        qh = q[:, sl]
        q_ref[:, sl] = ((qh * ct + pltpu.roll(qh, 64, 1) * sn) * scale).astype(BF16)
    kr = pr * ct + pltpu.roll(pr, 64, 1) * sn
    k = _dot(ckv, wuk_ref[...])
    for h in range(MLA_HEADS):
        sl = slice(h * 128, (h + 1) * 128)
        k_ref[:, sl] = (k[:, sl] + kr).astype(BF16)
    v_t = (_dot_nt(wuv_ref[...], ckv) + vone_ref[...]).astype(BF16)
    v_ref[...] = v_t.reshape(v_ref.shape)


def mla_pre(x, g, win_p, q_norm, kv_norm, wuq_p, wuk_p, wuv_t, vone, ct, sn, tm):
    s, d = x.shape
    hp = MLA_HEADS * 128
    npair = MLA_HEADS // 2
    full = lambda a: pl.BlockSpec(a.shape, lambda i: (0,) * a.ndim)
    g2, qn2, kvn2 = g.reshape(1, d), q_norm.reshape(1, -1), kv_norm.reshape(1, -1)
    return pl.pallas_call(
        _mla_pre_body,
        grid=(s // tm,),
        in_specs=[pl.BlockSpec((tm, d), lambda i: (i, 0)), full(g2), full(win_p), full(qn2), full(kvn2),
                  full(wuq_p), full(wuk_p), full(wuv_t), full(vone),
                  pl.BlockSpec((tm, 128), lambda i: (i, 0)), pl.BlockSpec((tm, 128), lambda i: (i, 0))],
        out_specs=[pl.BlockSpec((tm, hp), lambda i: (i, 0)), pl.BlockSpec((tm, hp), lambda i: (i, 0)),
                   pl.BlockSpec((npair, 1, 2 * MLA_VX, tm), lambda i: (0, i, 0, 0))],
        out_shape=[jax.ShapeDtypeStruct((s, hp), BF16), jax.ShapeDtypeStruct((s, hp), BF16),
                   jax.ShapeDtypeStruct((npair, s // tm, 2 * MLA_VX, tm), BF16)],
        compiler_params=_cparams("arbitrary"),
        name="mla_pre",
    )(x, g2, win_p, qn2, kvn2, wuq_p, wuk_p, wuv_t, vone, ct, sn)


def _mla_attn_body(q_ref, k_ref, vt_ref, o_ref, acc_ref, s_ref, *, tq, tk, nk):
    acc_ref[...] = jnp.zeros_like(acc_ref)

    def scores(j, slot):
        off = pl.multiple_of(j * tk, tk)
        for hh in range(2):
            sl = slice(hh * 128, (hh + 1) * 128)
            s_ref[slot, hh] = _dot_nt(k_ref[pl.ds(off, tk), sl], q_ref[:, sl])

    def consume(j, slot, carry):
        new = []
        for hh in range(2):
            m_old = carry[hh]
            rows = slice(hh * MLA_VX, (hh + 1) * MLA_VX)
            s = s_ref[slot, hh]
            m_new = jnp.maximum(m_old, jnp.max(s, axis=0, keepdims=True))
            p = jnp.exp2((s - m_new).astype(BF16))
            acc_ref[rows, :] = acc_ref[rows, :] * jnp.exp2(m_old - m_new) + _dot(vt_ref[0, j, rows, :], p)
            new.append(m_new)
        return tuple(new)

    scores(0, 0)

    def body(jj, carry):
        j = 2 * jj
        scores(j + 1, 1)
        carry = consume(j, 0, carry)
        scores(jnp.minimum(j + 2, nk - 1), 0)
        return consume(j + 1, 1, carry)

    lax.fori_loop(0, nk // 2, body, (jnp.full((1, tq), -jnp.inf, F32),) * 2)
    o_t = jnp.concatenate([acc_ref[hh * MLA_VX:hh * MLA_VX + MLA_V, :]
                           / acc_ref[hh * MLA_VX + MLA_V:hh * MLA_VX + MLA_V + 1, :] for hh in range(2)], axis=0)
    o_ref[...] = o_t.T.astype(o_ref.dtype)


def mla_attention(q, k, v_t, tq):
    s = q.shape[0]
    npair, nk, _, tk = v_t.shape
    return pl.pallas_call(
        functools.partial(_mla_attn_body, tq=tq, tk=tk, nk=nk),
        grid=(npair, s // tq),
        in_specs=[pl.BlockSpec((tq, 256), lambda p, i: (i, p)),
                  pl.BlockSpec((s, 256), lambda p, i: (0, p)),
                  pl.BlockSpec((1, nk, 2 * MLA_VX, tk), lambda p, i: (p, 0, 0, 0))],
        out_specs=pl.BlockSpec((tq, 128), lambda p, i: (i, p)),
        out_shape=jax.ShapeDtypeStruct((s, MLA_HEADS * MLA_V), BF16),
        scratch_shapes=[pltpu.VMEM((2 * MLA_VX, tq), F32), pltpu.VMEM((2, 2, tk, tq), F32)],
        compiler_params=_cparams("arbitrary", "arbitrary"),
        name="mla_attention",
    )(q, k, v_t)


def _ret_direction(q_ref, k_ref, v_ref, ct_ref, sn_ref, dm_ref, qd_ref, kd_ref, gc_ref, o_ref, st_ref):
    ct, sn = ct_ref[...], sn_ref[...]
    for h in range(RET_HEADS):
        sk = slice(h * RET_DK, (h + 1) * RET_DK)
        sv = slice(h * RET_DV, (h + 1) * RET_DV)
        qh = q_ref[:, sk]
        kh = k_ref[:, sk]
        qr = qh * ct + pltpu.roll(qh, 64, 1) * sn
        kr = (kh * ct + pltpu.roll(kh, 64, 1) * sn) * RET_DK ** -0.5
        v16 = v_ref[:, sv].astype(BF16)
        state = st_ref[h]
        inner = _dot_nt(qr.astype(BF16), kr.astype(BF16)) * dm_ref[h]
        o_ref[:, sv] = (_dot(inner.astype(BF16), v16)
                        + _dot((qr * qd_ref[h]).astype(BF16), state.astype(BF16)))
        st_ref[h] = state * gc_ref[h] + _dot_tn((kr * kd_ref[h]).astype(BF16), v16)


def _ret_scan_body(qf, kf, vf, ctf, snf, qb, kb, vb, ctb, snb,
                   dmf, qdf, kdf, gcf, dmb, qdb, kdb, gcb, of_ref, ob_ref, sf_ref, sb_ref):
    @pl.when(pl.program_id(0) == 0)
    def _():
        sf_ref[...] = jnp.zeros_like(sf_ref)
        sb_ref[...] = jnp.zeros_like(sb_ref)

    _ret_direction(qf, kf, vf, ctf, snf, dmf, qdf, kdf, gcf, of_ref, sf_ref)
    _ret_direction(qb, kb, vb, ctb, snb, dmb, qdb, kdb, gcb, ob_ref, sb_ref)


def ret_scan(proj, ct, sn, tabs_f, tabs_b):
    s = proj.shape[0]
    c = RET_CHUNK
    nb = s // c
    dq = RET_HEADS * RET_DK
    dv = RET_HEADS * RET_DV

    def dir_specs(blk):
        return [pl.BlockSpec((c, dq), lambda t: (blk(t), 0)),
                pl.BlockSpec((c, dq), lambda t: (blk(t), 1)),
                pl.BlockSpec((c, dv), lambda t: (blk(t), 1)),
                pl.BlockSpec((c, 128), lambda t: (blk(t), 0)),
                pl.BlockSpec((c, 128), lambda t: (blk(t), 0))]

    full = lambda a: pl.BlockSpec(a.shape, lambda t: (0,) * a.ndim)
    fwd = lambda t: t
    bwd = lambda t: nb - 1 - t
    return pl.pallas_call(
        _ret_scan_body,
        grid=(nb,),
        in_specs=dir_specs(fwd) + dir_specs(bwd) + [full(a) for a in tabs_f] + [full(a) for a in tabs_b],
        out_specs=[pl.BlockSpec((c, dv), lambda t: (t, 0)), pl.BlockSpec((c, dv), lambda t: (nb - 1 - t, 0))],
        out_shape=[jax.ShapeDtypeStruct((s, dv), F32)] * 2,
        scratch_shapes=[pltpu.VMEM((RET_HEADS, RET_DK, RET_DV), F32)] * 2,
        compiler_params=_cparams("arbitrary"),
        name="ret_scan",
    )(proj, proj, proj, ct, sn, proj, proj, proj, ct, sn, *tabs_f, *tabs_b)


def _ret_post_body(of_ref, ob_ref, gate_ref, gn_ref, w_ref, r_ref, o_ref):
    parts = []
    for h in range(RET_HEADS):
        sv = slice(h * RET_DV, (h + 1) * RET_DV)
        o = of_ref[:, sv] + ob_ref[:, sv]
        mu = jnp.mean(o, axis=1, keepdims=True)
        oc = o - mu
        var = jnp.mean(oc * oc, axis=1, keepdims=True)
        gate = gate_ref[:, sv]
        parts.append((gate * _sigmoid(gate) * (oc * lax.rsqrt(var + EPS) * gn_ref[:, sv])).astype(BF16))
    y = jnp.concatenate(parts, axis=1)
    o_ref[...] = r_ref[...] + _dot(y, w_ref[...])


def ret_post(o_f, o_b, proj, gn_w, w_o, res, tm):
    s, dv = o_f.shape
    d = res.shape[1]
    return pl.pallas_call(
        _ret_post_body,
        grid=(s // tm,),
        in_specs=[pl.BlockSpec((tm, dv), lambda i: (i, 0)),
                  pl.BlockSpec((tm, dv), lambda i: (i, 0)),
                  pl.BlockSpec((tm, dv), lambda i: (i, 2)),
                  pl.BlockSpec((1, dv), lambda i: (0, 0)),
                  pl.BlockSpec((dv, d), lambda i: (0, 0)),
                  pl.BlockSpec((tm, d), lambda i: (i, 0))],
        out_specs=pl.BlockSpec((tm, d), lambda i: (i, 0)),
        out_shape=jax.ShapeDtypeStruct((s, d), F32),
        compiler_params=_cparams("arbitrary"),
        name="ret_post",
    )(o_f, o_b, proj, gn_w.reshape(1, dv), w_o, res)


def _na_layer(h, norm, w_qkv, rpb, w_o):
    s = h.shape[0]
    qkv = norm_matmul(h, norm, w_qkv.astype(BF16), BF16, tm=512, tn=1024, name="na_qkv")
    o = na_attention(qkv, _na_bias_table(rpb, s // GRID_W))
    return matmul_residual(o, w_o.astype(BF16), h, tm=512, name="na_out")


def _gdn_layer(h, norm, w_in, conv_w, a_log_f, a_log_b, dt_bias_f, dt_bias_b, o_norm, w_o):
    d = GDN_HEADS * GDN_DK
    n_main = 4 * d
    proj = norm_matmul(h, norm, w_in[:, :n_main].astype(BF16), F32, tm=512, tn=1024, name="gdn_in")
    w_gate = jnp.pad(w_in[:, n_main:], ((0, 0), (0, 128 - 4 * GDN_HEADS))).astype(BF16)
    gp = norm_matmul(h, norm, w_gate, F32, tm=512, tn=128, name="gdn_in_gates")
    z8 = jnp.zeros((2 * GDN_HEADS,), F32)
    pad = jnp.zeros((128 - 4 * GDN_HEADS,), F32)
    alog_vec = jnp.concatenate([z8, a_log_f.astype(F32), a_log_b.astype(F32), pad]).reshape(1, 128)
    dtb_vec = jnp.concatenate([z8, dt_bias_f.astype(F32), dt_bias_b.astype(F32), pad]).reshape(1, 128)
    gcol, grow = gdn_gates(gp, alog_vec, dtb_vec, tm=512)
    qkv = gdn_conv(proj, jnp.transpose(conv_w).astype(F32), tm=256)
    o_f, o_b = gdn_scan(qkv, gcol, grow, tb=256)
    return gdn_post(o_f, o_b, proj, o_norm, w_o.astype(BF16), h, tm=512)


def _mla_lane_maps():
    lane = np.arange(128)
    rope1 = lane < 16
    rope2 = (lane >= 64) & (lane < 80)
    nope_a = (lane >= 16) & (lane < 64)
    nope_b = (lane >= 80) & (lane < 96)
    q_dim = np.where(rope1, MLA_NOPE + lane, np.where(nope_a, lane - 16, np.where(rope2, lane + 16, lane - 32)))
    q_ok = lane < 96
    nope_dim = np.where(nope_a, lane - 16, lane - 32)
    nope_ok = nope_a | nope_b
    rope_dim = np.where(rope1, lane, lane - 64 + 16)
    rope_ok = rope1 | rope2
    return q_dim, q_ok, nope_dim, nope_ok, rope_dim, rope_ok


def _mla_layer(h, norm, w_in, q_norm, w_uq, kv_norm, w_ukv, w_o):
    s = h.shape[0]
    hh = MLA_HEADS
    dqk = MLA_NOPE + MLA_ROPE
    q_dim, q_ok, nope_dim, nope_ok, rope_dim, rope_ok = _mla_lane_maps()
    head = np.arange(hh)[:, None]
    q_cols = (head * dqk + np.where(q_ok, q_dim, 0)[None, :]).reshape(-1)
    wuq_p = jnp.where(np.tile(q_ok, hh)[None, :], w_uq[:, q_cols], 0.0).astype(BF16)
    k_cols = (head * (MLA_NOPE + MLA_V) + np.where(nope_ok, nope_dim, 0)[None, :]).reshape(-1)
    wuk_p = jnp.where(np.tile(nope_ok, hh)[None, :], w_ukv[:, k_cols], 0.0).astype(BF16)
    vx = np.arange(MLA_VX)[None, :]
    v_cols = (head * (MLA_NOPE + MLA_V) + MLA_NOPE + np.minimum(vx, MLA_V - 1)).reshape(-1)
    v_ok = np.broadcast_to(vx < MLA_V, (hh, MLA_VX)).reshape(-1)
    wuv_t = jnp.where(v_ok[:, None], jnp.transpose(w_ukv[:, v_cols]), 0.0).astype(BF16)
    vone = jnp.asarray(np.broadcast_to(vx == MLA_V, (hh, MLA_VX)).reshape(-1, 1), F32)
    n_c = MLA_Q_RANK + MLA_KV_RANK
    w_rope = jnp.where(rope_ok[None, :], w_in[:, n_c + np.where(rope_ok, rope_dim, 0)], 0.0)
    win_p = jnp.concatenate([w_in[:, :n_c], w_rope], axis=1).astype(BF16)
    half = MLA_ROPE // 2
    inv = 1.0 / (ROPE_THETA ** (jnp.arange(half, dtype=F32) / half))
    ang = jnp.arange(s, dtype=jnp.int32).astype(F32)[:, None] * inv[None, :]
    cos, sin = jnp.cos(ang), jnp.sin(ang)
    ones = jnp.ones((s, 48), F32)
    zeros = jnp.zeros((s, 48), F32)
    ct = jnp.concatenate([cos, ones, cos, ones], axis=1)
    sn = jnp.concatenate([-sin, zeros, sin, zeros], axis=1)
    q, k, v_t = mla_pre(h, norm, win_p, q_norm, kv_norm, wuq_p, wuk_p, wuv_t, vone, ct, sn, tm=256)
    o = mla_attention(q, k, v_t, tq=512)
    return matmul_residual(o, w_o.astype(BF16), h, tm=512, name="mla_out")


def _ret_tables(reverse):
    c = RET_CHUNK
    lg = jnp.log1p(-jnp.exp2(-5.0 - jnp.arange(RET_HEADS, dtype=F32)))
    if reverse:
        lg = lg[::-1]
    idx = np.arange(c)
    diff = idx[:, None] - idx[None, :]
    if reverse:
        keep, expo = diff < 0, -diff
        q_pow, k_pow = c - idx, idx
    else:
        keep, expo = diff >= 0, diff
        q_pow, k_pow = idx + 1, c - 1 - idx
    dmask = jnp.where(keep, jnp.exp(jnp.where(keep, expo, 0).astype(F32)[None] * lg[:, None, None]), 0.0)
    q_dec = jnp.exp(jnp.asarray(q_pow, F32)[None, :] * lg[:, None])
    k_dec = jnp.exp(jnp.asarray(k_pow, F32)[None, :] * lg[:, None])
    g_chunk = jnp.exp(c * lg)
    bc = lambda t: jnp.broadcast_to(t[:, :, None], (RET_HEADS, c, RET_DK))
    return [dmask, bc(q_dec), bc(k_dec), jnp.broadcast_to(g_chunk[:, None, None], (RET_HEADS, 1, RET_DV))]


def _ret_layer(h, norm, w_in, gn_w, w_o):
    s = h.shape[0]
    proj = norm_matmul(h, norm, w_in.astype(BF16), F32, tm=512, tn=1024, name="ret_in")
    half = RET_DK // 2
    inv = 1.0 / (ROPE_THETA ** (jnp.arange(half, dtype=F32) / half))
    ang = jnp.arange(s, dtype=jnp.int32).astype(F32)[:, None] * inv[None, :]
    cos, sin = jnp.cos(ang), jnp.sin(ang)
    ct = jnp.concatenate([cos, cos], axis=1)
    sn = jnp.concatenate([-sin, sin], axis=1)
    o_f, o_b = ret_scan(proj, ct, sn, _ret_tables(False), _ret_tables(True))
    return ret_post(o_f, o_b, proj, gn_w, w_o.astype(BF16), h, tm=256)


def kernel(x, na_norm, na_w_qkv, na_rpb, na_w_o, gdn_norm, gdn_w_in, gdn_conv, gdn_a_log_f, gdn_a_log_b, gdn_dt_bias_f, gdn_dt_bias_b, gdn_o_norm, gdn_w_o, mla_norm, mla_w_in, mla_q_norm, mla_w_uq, mla_kv_norm, mla_w_ukv, mla_w_o, ret_norm, ret_w_in, ret_gn, ret_w_o, mlp_norm, mlp_w1, mlp_w2, final_norm):
    b, s, d = x.shape
    depth = mlp_norm.shape[0]
    outs = []
    for bi in range(b):
        h = x[bi].astype(F32)
        for i in range(depth):
            m, j = i % 4, i // 4
            if m == 0:
                h = _na_layer(h, na_norm[j], na_w_qkv[j], na_rpb[j], na_w_o[j])
            elif m == 1:
                h = _gdn_layer(h, gdn_norm[j], gdn_w_in[j], gdn_conv[j], gdn_a_log_f[j], gdn_a_log_b[j],
                               gdn_dt_bias_f[j], gdn_dt_bias_b[j], gdn_o_norm[j], gdn_w_o[j])
            elif m == 2:
                h = _mla_layer(h, mla_norm[j], mla_w_in[j], mla_q_norm[j], mla_w_uq[j], mla_kv_norm[j],
                               mla_w_ukv[j], mla_w_o[j])
            else:
                h = _ret_layer(h, ret_norm[j], ret_w_in[j], ret_gn[j], ret_w_o[j])
            h = mlp_block(h, mlp_norm[i], mlp_w1[i].astype(BF16), mlp_w2[i].astype(BF16), final_norm,
                          final=(i == depth - 1), tm=512, tf=1024, name=f"mlp_{i}")
        outs.append(h)
    return jnp.stack(outs).astype(x.dtype)
```

```python
import functools
import math

import numpy as np
import jax
import jax.numpy as jnp
from jax import lax
from jax.experimental import pallas as pl
from jax.experimental.pallas import tpu as pltpu

F32 = jnp.float32
BF16 = jnp.bfloat16
EPS = 1e-6
ROPE_THETA = 10000.0
GRID_W = 64
NEG_BIG = -1e30
HI = lax.Precision.HIGHEST

NA_HEADS, NA_DH, NA_WIN_ROWS, NA_WIN_COLS, NA_Q_ROWS = 16, 64, 8, 16, 2
NA_KBLK = 5
GDN_HEADS, GDN_DK, GDN_CONV, GDN_CHUNK = 8, 128, 5, 64
GDN_GATE_LANES = 32
MLA_HEADS, MLA_Q_RANK, MLA_KV_RANK, MLA_NOPE, MLA_ROPE, MLA_V = 16, 768, 256, 64, 32, 64
MLA_VX = 80
RET_HEADS, RET_DK, RET_DV, RET_CHUNK = 8, 128, 256, 128

VMEM_LIMIT = 52 * 1024 * 1024


def _cparams(*sem):
    return pltpu.CompilerParams(dimension_semantics=sem, vmem_limit_bytes=VMEM_LIMIT)


def _rms(x, g):
    return x * lax.rsqrt(jnp.mean(x * x, axis=-1, keepdims=True) + EPS) * g


def _sigmoid(x):
    return 1.0 / (1.0 + jnp.exp(-x))


def _dot(a, b, **kw):
    return jnp.dot(a, b, preferred_element_type=F32, **kw)


def _dot_nt(a, b, **kw):
    return lax.dot_general(a, b, (((1,), (1,)), ((), ())), preferred_element_type=F32, **kw)


def _dot_tn(a, b, **kw):
    return lax.dot_general(a, b, (((0,), (0,)), ((), ())), preferred_element_type=F32, **kw)


def _norm_matmul_body(x_ref, g_ref, w_ref, o_ref, xn_ref):
    @pl.when(pl.program_id(1) == 0)
    def _():
        xn_ref[...] = _rms(x_ref[...], g_ref[...]).astype(BF16)

    o_ref[...] = _dot(xn_ref[...], w_ref[...]).astype(o_ref.dtype)


def norm_matmul(x, g, w, out_dtype, tm, tn, name):
    s, d = x.shape
    n = w.shape[1]
    return pl.pallas_call(
        _norm_matmul_body,
        grid=(s // tm, n // tn),
        in_specs=[pl.BlockSpec((tm, d), lambda i, j: (i, 0)),
                  pl.BlockSpec((1, d), lambda i, j: (0, 0)),
                  pl.BlockSpec((d, tn), lambda i, j: (0, j))],
        out_specs=pl.BlockSpec((tm, tn), lambda i, j: (i, j)),
        out_shape=jax.ShapeDtypeStruct((s, n), out_dtype),
        scratch_shapes=[pltpu.VMEM((tm, d), BF16)],
        compiler_params=_cparams("arbitrary", "arbitrary"),
        name=name,
    )(x, g.reshape(1, d), w)


def _matmul_res_body(a_ref, w_ref, r_ref, o_ref):
    o_ref[...] = r_ref[...] + _dot(a_ref[...], w_ref[...])


def matmul_residual(a, w, res, tm, name):
    s, k = a.shape
    n = w.shape[1]
    return pl.pallas_call(
        _matmul_res_body,
        grid=(s // tm,),
        in_specs=[pl.BlockSpec((tm, k), lambda i: (i, 0)),
                  pl.BlockSpec((k, n), lambda i: (0, 0)),
                  pl.BlockSpec((tm, n), lambda i: (i, 0))],
        out_specs=pl.BlockSpec((tm, n), lambda i: (i, 0)),
        out_shape=jax.ShapeDtypeStruct((s, n), F32),
        compiler_params=_cparams("arbitrary"),
        name=name,
    )(a, w, res)


def _mlp_body(x_ref, g_ref, w1_ref, w2_ref, fg_ref, o_ref, xn_ref, acc_ref, *, nk, final):
    k = pl.program_id(1)

    @pl.when(k == 0)
    def _():
        xn_ref[...] = _rms(x_ref[...], g_ref[...]).astype(BF16)
        acc_ref[...] = jnp.zeros_like(acc_ref)

    a = jnp.maximum(_dot(xn_ref[...], w1_ref[...]), 0.0)
    acc_ref[...] += _dot((a * a).astype(BF16), w2_ref[...])

    @pl.when(k == nk - 1)
    def _():
        y = x_ref[...] + acc_ref[...]
        if final:
            y = _rms(y, fg_ref[...])
        o_ref[...] = y


def mlp_block(x, g, w1, w2, final_g, final, tm, tf, name):
    s, d = x.shape
    f = w1.shape[1]
    nk = f // tf
    return pl.pallas_call(
        functools.partial(_mlp_body, nk=nk, final=final),
        grid=(s // tm, nk),
        in_specs=[pl.BlockSpec((tm, d), lambda i, k: (i, 0)),
                  pl.BlockSpec((1, d), lambda i, k: (0, 0)),
                  pl.BlockSpec((d, tf), lambda i, k: (0, k)),
                  pl.BlockSpec((tf, d), lambda i, k: (k, 0)),
                  pl.BlockSpec((1, d), lambda i, k: (0, 0))],
        out_specs=pl.BlockSpec((tm, d), lambda i, k: (i, 0)),
        out_shape=jax.ShapeDtypeStruct((s, d), F32),
        scratch_shapes=[pltpu.VMEM((tm, d), BF16), pltpu.VMEM((tm, d), F32)],
        compiler_params=_cparams("arbitrary", "arbitrary"),
        name=name,
    )(x, g.reshape(1, d), w1, w2, final_g.reshape(1, d))


def _na_bias_table(rpb, rows):
    n_qblk = rows // NA_Q_ROWS
    reps = {4: 0, 3: 1, 2: 2, 1: n_qblk - 2, 0: n_qblk - 1}
    w, wc, wr = GRID_W, NA_WIN_COLS, NA_WIN_ROWS
    pcol = jnp.pad(rpb.astype(F32), ((0, 0), (0, 0), (w - wc, w - wc)), mode="edge")
    e = jnp.stack([pcol[:, :, w - 1 - qc:2 * w - 1 - qc] for qc in range(w)], axis=2)
    qc = np.arange(w)
    qc0 = np.clip(qc - wc // 2, 0, w - wc)
    col_ok = (qc[None, :] >= qc0[:, None]) & (qc[None, :] < qc0[:, None] + wc)
    tabs, oks = [], []
    for t in range(5):
        i = reps[t]
        base = int(np.clip(i - 2, 0, n_qblk - NA_KBLK))
        per_qr, ok_qr = [], []
        for qr in range(NA_Q_ROWS):
            qrow = NA_Q_ROWS * i + qr
            qr0 = int(np.clip(qrow - wr // 2, 0, rows - wr))
            krows = NA_Q_ROWS * base + np.arange(NA_KBLK * NA_Q_ROWS)
            drow = np.clip(krows - qrow + wr - 1, 0, 2 * wr - 2)
            row_ok = (krows >= qr0) & (krows < qr0 + wr)
            per_qr.append(jnp.concatenate([e[:, int(d)] for d in drow], axis=-1))
            ok_qr.append(np.concatenate([col_ok & bool(r) for r in row_ok], axis=-1))
        tabs.append(jnp.concatenate(per_qr, axis=1))
        oks.append(np.concatenate(ok_qr, axis=0))
    return jnp.where(np.stack(oks)[:, None], jnp.stack(tabs), NEG_BIG)


def _na_body(q_ref, k0, k1, k2, k3, k4, v0, v1, v2, v3, v4, b_ref, o_ref):
    krefs = (k0, k1, k2, k3, k4)
    vrefs = (v0, v1, v2, v3, v4)
    scale = NA_DH ** -0.5
    nkb = NA_Q_ROWS * GRID_W
    for h in range(NA_HEADS):
        sl = slice(h * NA_DH, (h + 1) * NA_DH)
        qh = q_ref[:, sl]
        s = jnp.concatenate([_dot_nt(qh, kr[:, sl]) for kr in krefs], axis=1)
        s = s * scale + b_ref[0, h]
        m = jnp.max(s, axis=1, keepdims=True)
        p = jnp.exp(s - m)
        l = jnp.sum(p, axis=1, keepdims=True)
        pb = p.astype(BF16)
        o = _dot(pb[:, 0:nkb], vrefs[0][:, sl])
        for d in range(1, NA_KBLK):
            o = o + _dot(pb[:, d * nkb:(d + 1) * nkb], vrefs[d][:, sl])
        o_ref[:, sl] = (o / l).astype(o_ref.dtype)


def na_attention(qkv, bias_tab):
    s = qkv.shape[0]
    d = NA_HEADS * NA_DH
    tq = NA_Q_ROWS * GRID_W
    n_qblk = s // tq
    top = n_qblk - NA_KBLK

    def base(i):
        return jnp.clip(i - 2, 0, top)

    kv_specs = [pl.BlockSpec((tq, d), functools.partial(lambda i, dd, c: (base(i) + dd, c), dd=dd, c=c))
                for c in (1, 2) for dd in range(NA_KBLK)]
    return pl.pallas_call(
        _na_body,
        grid=(n_qblk,),
        in_specs=[pl.BlockSpec((tq, d), lambda i: (i, 0))] + kv_specs
                 + [pl.BlockSpec((1, NA_HEADS, tq, NA_KBLK * tq), lambda i: (base(i) - i + 4, 0, 0, 0))],
        out_specs=pl.BlockSpec((tq, d), lambda i: (i, 0)),
        out_shape=jax.ShapeDtypeStruct((s, d), BF16),
        compiler_params=_cparams("arbitrary"),
        name="na_attention",
    )(qkv, *([qkv] * (2 * NA_KBLK)), bias_tab)


def _gdn_conv_body(x_ref, xp_ref, xn_ref, w_ref, o_ref, *, tm, nt):
    i = pl.program_id(0)
    j = pl.program_id(1)
    prev = jnp.where(i > 0, xp_ref[...], 0.0)
    nxt = jnp.where(i < nt - 1, xn_ref[...], 0.0)
    xe = jnp.concatenate([prev, x_ref[...], nxt], axis=0)
    w = w_ref[...]
    half = GDN_CONV // 2
    acc = xe[8 - half:8 - half + tm] * w[0:1]
    for t in range(1, GDN_CONV):
        acc = acc + xe[8 - half + t:8 - half + t + tm] * w[t:t + 1]
    y = acc * _sigmoid(acc)
    for h in range(GDN_HEADS):
        sl = slice(h * GDN_DK, (h + 1) * GDN_DK)
        yh = y[:, sl]
        nrm = lax.rsqrt(jnp.sum(yh * yh, axis=1, keepdims=True) + EPS)
        fac = jnp.where(j == 0, nrm * GDN_DK ** -0.5, jnp.where(j == 1, nrm, 1.0))
        o_ref[:, sl] = (yh * fac).astype(o_ref.dtype)


def gdn_conv(proj, conv_w_t, tm):
    s = proj.shape[0]
    d = GDN_HEADS * GDN_DK
    nt = s // tm
    r8 = tm // 8
    return pl.pallas_call(
        functools.partial(_gdn_conv_body, tm=tm, nt=nt),
        grid=(nt, 3),
        in_specs=[pl.BlockSpec((tm, d), lambda i, j: (i, j)),
                  pl.BlockSpec((8, d), lambda i, j: (jnp.maximum(i * r8 - 1, 0), j)),
                  pl.BlockSpec((8, d), lambda i, j: (jnp.minimum((i + 1) * r8, nt * r8 - 1), j)),
                  pl.BlockSpec((GDN_CONV, d), lambda i, j: (0, j))],
        out_specs=pl.BlockSpec((tm, d), lambda i, j: (i, j)),
        out_shape=jax.ShapeDtypeStruct((s, 3 * d), BF16),
        compiler_params=_cparams("arbitrary", "arbitrary"),
        name="gdn_conv",
    )(proj, proj, proj, conv_w_t)


def _gdn_gates_body(gp_ref, alog_ref, dtb_ref, col_ref, row_ref, *, tm):
    c = GDN_CHUNK
    gp = gp_ref[...]
    lane = lax.broadcasted_iota(jnp.int32, (tm, 128), 1)
    is_f = (lane >= 16) & (lane < 24)
    is_b = (lane >= 24) & (lane < 32)
    beta = _sigmoid(gp)
    z = gp + dtb_ref[...]
    softplus = jnp.maximum(z, 0.0) + jnp.log1p(jnp.exp(-jnp.abs(z)))
    g = jnp.where(is_f | is_b, -jnp.exp(alog_ref[...]) * softplus, 0.0)
    r = lax.broadcasted_iota(jnp.int32, (c, c), 0)
    cc = lax.broadcasted_iota(jnp.int32, (c, c), 1)
    tri_lo = (r >= cc).astype(F32)
    tri_up = (r <= cc).astype(F32)
    lane_c = lax.broadcasted_iota(jnp.int32, (c, 128), 1)
    is_f_c = (lane_c >= 16) & (lane_c < 24)
    is_g_c = (lane_c >= 16) & (lane_c < 32)
    for ch in range(tm // c):
        gc = g[ch * c:(ch + 1) * c]
        cum = jnp.where(is_f_c, _dot(tri_lo, gc, precision=HI), _dot(tri_up, gc, precision=HI))
        colc = jnp.where(is_g_c, cum, beta[ch * c:(ch + 1) * c])
        col_ref[ch * c:(ch + 1) * c, :] = colc
        row_ref[ch] = colc.T[:GDN_GATE_LANES, :]


def gdn_gates(gp, alog_vec, dtb_vec, tm):
    s = gp.shape[0]
    c = GDN_CHUNK
    return pl.pallas_call(
        functools.partial(_gdn_gates_body, tm=tm),
        grid=(s // tm,),
        in_specs=[pl.BlockSpec((tm, 128), lambda i: (i, 0)),
                  pl.BlockSpec((1, 128), lambda i: (0, 0)),
                  pl.BlockSpec((1, 128), lambda i: (0, 0))],
        out_specs=[pl.BlockSpec((tm, 128), lambda i: (i, 0)),
                   pl.BlockSpec((tm // c, GDN_GATE_LANES, c), lambda i: (i, 0, 0))],
        out_shape=[jax.ShapeDtypeStruct((s, 128), F32), jax.ShapeDtypeStruct((s // c, GDN_GATE_LANES, c), F32)],
        compiler_params=_cparams("arbitrary"),
        name="gdn_gates",
    )(gp, alog_vec, dtb_vec)


def _bmm(a, b):
    return jnp.einsum("nik,nkj->nij", a, b, preferred_element_type=F32)


def _bmm_nt(a, b):
    return jnp.einsum("nik,njk->nij", a, b, preferred_element_type=F32)


def _unit_tri_inverse(lmat, r, cc):
    eye = (r == cc).astype(F32)
    diag_blk = (r // 16) == (cc // 16)
    ld = jnp.where(diag_blk, lmat, 0.0)
    lo = lmat - ld
    x = eye - ld
    p = _bmm(ld, ld)
    x = x + _bmm(x, p)
    p = _bmm(p, p)
    x = x + _bmm(x, p)
    p = _bmm(p, p)
    dinv = x + _bmm(x, p)
    n1 = _bmm(dinv, lo)
    n2 = _bmm(n1, n1)
    y = eye - n1 + n2 - _bmm(n1, n2)
    return _bmm(y, dinv)


def _gdn_local_body(qkv_ref, col_ref, row_ref, *out_refs, tb):
    c, dk, hh = GDN_CHUNK, GDN_DK, GDN_HEADS
    nc = tb // c
    r = lax.broadcasted_iota(jnp.int32, (nc, c, c), 1)
    cc = lax.broadcasted_iota(jnp.int32, (nc, c, c), 2)
    col3 = col_ref[...].reshape(nc, c, 128)
    for d in range(2):
        reverse = d == 1
        u_ref, w_ref, qd_ref, kd_ref, a_ref, gl_ref = out_refs[6 * d:6 * d + 6]
        incl = (r <= cc) if reverse else (r >= cc)
        strict = (r < cc) if reverse else (r > cc)
        for h in range(hh):
            sl = slice(h * dk, (h + 1) * dk)
            b_lane = h + (8 if reverse else 0)
            g_lane = h + (24 if reverse else 16)
            qc = qkv_ref[:, sl].astype(F32).reshape(nc, c, dk)
            kc = qkv_ref[:, hh * dk + h * dk:hh * dk + (h + 1) * dk].astype(F32).reshape(nc, c, dk)
            vc = qkv_ref[:, 2 * hh * dk + h * dk:2 * hh * dk + (h + 1) * dk].astype(F32).reshape(nc, c, dk)
            beta = col3[:, :, b_lane:b_lane + 1]
            gcol = col3[:, :, g_lane:g_lane + 1]
            grow = row_ref[:, g_lane:g_lane + 1, :]
            g_last = gcol[:, 0:1, :] if reverse else gcol[:, c - 1:c, :]
            decay = jnp.where(incl, jnp.exp(jnp.where(incl, gcol - grow, 0.0)), 0.0)
            eg = jnp.exp(gcol)
            kb = kc * beta
            kc16 = kc.astype(BF16)
            lmat = jnp.where(strict, _bmm_nt(kb.astype(BF16), kc16) * decay, 0.0)
            tinv = _unit_tri_inverse(lmat, r, cc)
            sol = _bmm(tinv, jnp.concatenate([vc * beta, kb * eg], axis=2))
            a_intra = jnp.where(incl, _bmm_nt(qc.astype(BF16), kc16) * decay, 0.0)
            u_ref[:, sl] = sol[:, :, :dk].reshape(tb, dk)
            w_ref[:, sl] = sol[:, :, dk:].reshape(tb, dk).astype(BF16)
            qd_ref[:, sl] = (qc * eg).reshape(tb, dk).astype(BF16)
            kd_ref[:, sl] = (kc * jnp.exp(g_last - gcol)).reshape(tb, dk).astype(BF16)
            a_ref[:, h * c:(h + 1) * c] = a_intra.reshape(tb, c).astype(BF16)
            gl_ref[:, h:h + 1, :] = jnp.broadcast_to(jnp.exp(g_last), (nc, 1, dk))


def gdn_local(qkv, gcol, grow, tb):
    s = qkv.shape[0]
    c, d = GDN_CHUNK, GDN_HEADS * GDN_DK
    nc = tb // c
    row = lambda w: pl.BlockSpec((tb, w), lambda i: (i, 0))
    per_dir_specs = [row(d), row(d), row(d), row(d), row(GDN_HEADS * c),
                     pl.BlockSpec((nc, GDN_HEADS, GDN_DK), lambda i: (i, 0, 0))]
    per_dir_shapes = [jax.ShapeDtypeStruct((s, d), F32)] + [jax.ShapeDtypeStruct((s, d), BF16)] * 3 + [
        jax.ShapeDtypeStruct((s, GDN_HEADS * c), BF16), jax.ShapeDtypeStruct((s // c, GDN_HEADS, GDN_DK), F32)]
    return pl.pallas_call(
        functools.partial(_gdn_local_body, tb=tb),
        grid=(s // tb,),
        in_specs=[row(3 * d), row(128), pl.BlockSpec((nc, GDN_GATE_LANES, c), lambda i: (i, 0, 0))],
        out_specs=per_dir_specs * 2,
        out_shape=per_dir_shapes * 2,
        compiler_params=_cparams("arbitrary"),
        name="gdn_local",
    )(qkv, gcol, grow)


def _gdn_scan_body(*refs, tb):
    c, dk, hh = GDN_CHUNK, GDN_DK, GDN_HEADS
    nc = tb // c
    ins, (of_ref, ob_ref, st_ref) = refs[:12], refs[12:]

    @pl.when(pl.program_id(0) == 0)
    def _():
        st_ref[...] = jnp.zeros_like(st_ref)

    for step in range(nc):
        for d in range(2):
            u_ref, w_ref, qd_ref, kd_ref, a_ref, gl_ref = ins[6 * d:6 * d + 6]
            o_ref = (of_ref, ob_ref)[d]
            ch = nc - 1 - step if d == 1 else step
            rows = slice(ch * c, (ch + 1) * c)
            for h in range(hh):
                sl = slice(h * dk, (h + 1) * dk)
                state = st_ref[d, h]
                st16 = state.astype(BF16)
                v_new = u_ref[rows, sl] - _dot(w_ref[rows, sl], st16)
                vn16 = v_new.astype(BF16)
                o_ref[rows, sl] = _dot(qd_ref[rows, sl], st16) + _dot(a_ref[rows, h * c:(h + 1) * c], vn16)
                st_ref[d, h] = state * gl_ref[ch, h:h + 1, :] + _dot_tn(kd_ref[rows, sl], vn16)


def gdn_scan(local_out, tb):
    s, d = local_out[0].shape
    c = GDN_CHUNK
    nb = s // tb
    nc = tb // c

    def dir_specs(blk):
        row = lambda w: pl.BlockSpec((tb, w), lambda t: (blk(t), 0))
        return [row(d), row(d), row(d), row(d), row(GDN_HEADS * c),
                pl.BlockSpec((nc, GDN_HEADS, GDN_DK), lambda t: (blk(t), 0, 0))]

    fwd = lambda t: t
    bwd = lambda t: nb - 1 - t
    return pl.pallas_call(
        functools.partial(_gdn_scan_body, tb=tb),
        grid=(nb,),
        in_specs=dir_specs(fwd) + dir_specs(bwd),
        out_specs=[pl.BlockSpec((tb, d), lambda t: (t, 0)), pl.BlockSpec((tb, d), lambda t: (nb - 1 - t, 0))],
        out_shape=[jax.ShapeDtypeStruct((s, d), F32)] * 2,
        scratch_shapes=[pltpu.VMEM((2, GDN_HEADS, GDN_DK, GDN_DK), F32)],
        compiler_params=_cparams("arbitrary"),
        name="gdn_scan",
    )(*local_out)


def _gdn_post_body(of_ref, ob_ref, z_ref, on_ref, w_ref, r_ref, o_ref):
    on = on_ref[...]
    parts = []
    for h in range(GDN_HEADS):
        sl = slice(h * GDN_DK, (h + 1) * GDN_DK)
        o = of_ref[:, sl] + ob_ref[:, sl]
        z = z_ref[:, sl]
        parts.append((_rms(o, on) * (z * _sigmoid(z))).astype(BF16))
    y = jnp.concatenate(parts, axis=1)
    o_ref[...] = r_ref[...] + _dot(y, w_ref[...])


def gdn_post(o_f, o_b, proj, o_norm, w_o, res, tm):
    s, d = o_f.shape
    return pl.pallas_call(
        _gdn_post_body,
        grid=(s // tm,),
        in_specs=[pl.BlockSpec((tm, d), lambda i: (i, 0)),
                  pl.BlockSpec((tm, d), lambda i: (i, 0)),
                  pl.BlockSpec((tm, d), lambda i: (i, 3)),
                  pl.BlockSpec((1, GDN_DK), lambda i: (0, 0)),
                  pl.BlockSpec((d, d), lambda i: (0, 0)),
                  pl.BlockSpec((tm, d), lambda i: (i, 0))],
        out_specs=pl.BlockSpec((tm, d), lambda i: (i, 0)),
        out_shape=jax.ShapeDtypeStruct((s, d), F32),
        compiler_params=_cparams("arbitrary"),
        name="gdn_post",
    )(o_f, o_b, proj, o_norm.reshape(1, GDN_DK), w_o, res)


def _mla_pre_body(x_ref, g_ref, win_ref, qn_ref, kvn_ref, wuq_ref, wuk_ref, wuv_ref, vone_ref, ct_ref, sn_ref,
                  q_ref, k_ref, v_ref):
    xn = _rms(x_ref[...], g_ref[...]).astype(BF16)
    proj = _dot(xn, win_ref[...])
    cq = _rms(proj[:, :MLA_Q_RANK], qn_ref[...]).astype(BF16)
    ckv = _rms(proj[:, MLA_Q_RANK:MLA_Q_RANK + MLA_KV_RANK], kvn_ref[...]).astype(BF16)
    pr = proj[:, MLA_Q_RANK + MLA_KV_RANK:]
    ct, sn = ct_ref[...], sn_ref[...]
    scale = (MLA_NOPE + MLA_ROPE) ** -0.5 * math.log2(math.e)
    q = _dot(cq, wuq_ref[...])
    for h in range(MLA_HEADS):
        sl = slice(h * 128, (h + 1) * 128)
        qh = q[:, sl]
        q_ref[:, sl] = ((qh * ct + pltpu.roll(qh, 64, 1) * sn) * scale).astype(BF16)
    kr = pr * ct + pltpu.roll(pr, 64, 1) * sn
    k = _dot(ckv, wuk_ref[...])
    for h in range(MLA_HEADS):
        sl = slice(h * 128, (h + 1) * 128)
        k_ref[:, sl] = (k[:, sl] + kr).astype(BF16)
    v_t = (_dot_nt(wuv_ref[...], ckv) + vone_ref[...]).astype(BF16)
    v_ref[...] = v_t.reshape(v_ref.shape)


def mla_pre(x, g, win_p, q_norm, kv_norm, wuq_p, wuk_p, wuv_t, vone, ct, sn, tm):
    s, d = x.shape
    hp = MLA_HEADS * 128
    npair = MLA_HEADS // 2
    full = lambda a: pl.BlockSpec(a.shape, lambda i: (0,) * a.ndim)
    g2, qn2, kvn2 = g.reshape(1, d), q_norm.reshape(1, -1), kv_norm.reshape(1, -1)
    return pl.pallas_call(
        _mla_pre_body,
        grid=(s // tm,),
        in_specs=[pl.BlockSpec((tm, d), lambda i: (i, 0)), full(g2), full(win_p), full(qn2), full(kvn2),
                  full(wuq_p), full(wuk_p), full(wuv_t), full(vone),
                  pl.BlockSpec((tm, 128), lambda i: (i, 0)), pl.BlockSpec((tm, 128), lambda i: (i, 0))],
        out_specs=[pl.BlockSpec((tm, hp), lambda i: (i, 0)), pl.BlockSpec((tm, hp), lambda i: (i, 0)),
                   pl.BlockSpec((npair, 1, 2 * MLA_VX, tm), lambda i: (0, i, 0, 0))],
        out_shape=[jax.ShapeDtypeStruct((s, hp), BF16), jax.ShapeDtypeStruct((s, hp), BF16),
                   jax.ShapeDtypeStruct((npair, s // tm, 2 * MLA_VX, tm), BF16)],
        compiler_params=_cparams("arbitrary"),
        name="mla_pre",
    )(x, g2, win_p, qn2, kvn2, wuq_p, wuk_p, wuv_t, vone, ct, sn)


def _mla_attn_body(q_ref, k_ref, vt_ref, o_ref, acc_ref, s_ref, *, tq, tk, nk):
    acc_ref[...] = jnp.zeros_like(acc_ref)

    def scores(j, slot):
        off = pl.multiple_of(j * tk, tk)
        for hh in range(2):
            sl = slice(hh * 128, (hh + 1) * 128)
            s_ref[slot, hh] = _dot_nt(k_ref[pl.ds(off, tk), sl], q_ref[:, sl])

    def consume(j, slot, carry):
        new = []
        for hh in range(2):
            m_old = carry[hh]
            rows = slice(hh * MLA_VX, (hh + 1) * MLA_VX)
            s = s_ref[slot, hh]
            m_new = jnp.maximum(m_old, jnp.max(s, axis=0, keepdims=True))
            p = jnp.exp2((s - m_new).astype(BF16))
            acc_ref[rows, :] = acc_ref[rows, :] * jnp.exp2(m_old - m_new) + _dot(vt_ref[0, j, rows, :], p)
            new.append(m_new)
        return tuple(new)

    scores(0, 0)

    def body(jj, carry):
        j = 2 * jj
        scores(j + 1, 1)
        carry = consume(j, 0, carry)
        scores(jnp.minimum(j + 2, nk - 1), 0)
        return consume(j + 1, 1, carry)

    lax.fori_loop(0, nk // 2, body, (jnp.full((1, tq), -jnp.inf, F32),) * 2)
    o_t = jnp.concatenate([acc_ref[hh * MLA_VX:hh * MLA_VX + MLA_V, :]
                           / acc_ref[hh * MLA_VX + MLA_V:hh * MLA_VX + MLA_V + 1, :] for hh in range(2)], axis=0)
    o_ref[...] = o_t.T.astype(o_ref.dtype)


def mla_attention(q, k, v_t, tq):
    s = q.shape[0]
    npair, nk, _, tk = v_t.shape
    return pl.pallas_call(
        functools.partial(_mla_attn_body, tq=tq, tk=tk, nk=nk),
        grid=(npair, s // tq),
        in_specs=[pl.BlockSpec((tq, 256), lambda p, i: (i, p)),
                  pl.BlockSpec((s, 256), lambda p, i: (0, p)),
                  pl.BlockSpec((1, nk, 2 * MLA_VX, tk), lambda p, i: (p, 0, 0, 0))],
        out_specs=pl.BlockSpec((tq, 128), lambda p, i: (i, p)),
        out_shape=jax.ShapeDtypeStruct((s, MLA_HEADS * MLA_V), BF16),
        scratch_shapes=[pltpu.VMEM((2 * MLA_VX, tq), F32), pltpu.VMEM((2, 2, tk, tq), F32)],
        compiler_params=_cparams("arbitrary", "arbitrary"),
        name="mla_attention",
    )(q, k, v_t)


def _ret_direction(q_ref, k_ref, v_ref, ct_ref, sn_ref, dm_ref, qd_ref, kd_ref, gc_ref, o_ref, st_ref):
    ct, sn = ct_ref[...], sn_ref[...]
    for h in range(RET_HEADS):
        sk = slice(h * RET_DK, (h + 1) * RET_DK)
        sv = slice(h * RET_DV, (h + 1) * RET_DV)
        qh = q_ref[:, sk]
        kh = k_ref[:, sk]
        qr = qh * ct + pltpu.roll(qh, 64, 1) * sn
        kr = (kh * ct + pltpu.roll(kh, 64, 1) * sn) * RET_DK ** -0.5
        v16 = v_ref[:, sv].astype(BF16)
        state = st_ref[h]
        inner = _dot_nt(qr.astype(BF16), kr.astype(BF16)) * dm_ref[h]
        o_ref[:, sv] = (_dot(inner.astype(BF16), v16)
                        + _dot((qr * qd_ref[h]).astype(BF16), state.astype(BF16)))
        st_ref[h] = state * gc_ref[h] + _dot_tn((kr * kd_ref[h]).astype(BF16), v16)


def _ret_scan_body(qf, kf, vf, ctf, snf, qb, kb, vb, ctb, snb,
                   dmf, qdf, kdf, gcf, dmb, qdb, kdb, gcb, of_ref, ob_ref, sf_ref, sb_ref):
    @pl.when(pl.program_id(0) == 0)
    def _():
        sf_ref[...] = jnp.zeros_like(sf_ref)
        sb_ref[...] = jnp.zeros_like(sb_ref)

    _ret_direction(qf, kf, vf, ctf, snf, dmf, qdf, kdf, gcf, of_ref, sf_ref)
    _ret_direction(qb, kb, vb, ctb, snb, dmb, qdb, kdb, gcb, ob_ref, sb_ref)


def ret_scan(proj, ct, sn, tabs_f, tabs_b):
    s = proj.shape[0]
    c = RET_CHUNK
    nb = s // c
    dq = RET_HEADS * RET_DK
    dv = RET_HEADS * RET_DV

    def dir_specs(blk):
        return [pl.BlockSpec((c, dq), lambda t: (blk(t), 0)),
                pl.BlockSpec((c, dq), lambda t: (blk(t), 1)),
                pl.BlockSpec((c, dv), lambda t: (blk(t), 1)),
                pl.BlockSpec((c, 128), lambda t: (blk(t), 0)),
                pl.BlockSpec((c, 128), lambda t: (blk(t), 0))]

    full = lambda a: pl.BlockSpec(a.shape, lambda t: (0,) * a.ndim)
    fwd = lambda t: t
    bwd = lambda t: nb - 1 - t
    return pl.pallas_call(
        _ret_scan_body,
        grid=(nb,),
        in_specs=dir_specs(fwd) + dir_specs(bwd) + [full(a) for a in tabs_f] + [full(a) for a in tabs_b],
        out_specs=[pl.BlockSpec((c, dv), lambda t: (t, 0)), pl.BlockSpec((c, dv), lambda t: (nb - 1 - t, 0))],
        out_shape=[jax.ShapeDtypeStruct((s, dv), F32)] * 2,
        scratch_shapes=[pltpu.VMEM((RET_HEADS, RET_DK, RET_DV), F32)] * 2,
        compiler_params=_cparams("arbitrary"),
        name="ret_scan",
    )(proj, proj, proj, ct, sn, proj, proj, proj, ct, sn, *tabs_f, *tabs_b)


def _ret_post_body(of_ref, ob_ref, gate_ref, gn_ref, w_ref, r_ref, o_ref):
    parts = []
    for h in range(RET_HEADS):
        sv = slice(h * RET_DV, (h + 1) * RET_DV)
        o = of_ref[:, sv] + ob_ref[:, sv]
        mu = jnp.mean(o, axis=1, keepdims=True)
        oc = o - mu
        var = jnp.mean(oc * oc, axis=1, keepdims=True)
        gate = gate_ref[:, sv]
        parts.append((gate * _sigmoid(gate) * (oc * lax.rsqrt(var + EPS) * gn_ref[:, sv])).astype(BF16))
    y = jnp.concatenate(parts, axis=1)
    o_ref[...] = r_ref[...] + _dot(y, w_ref[...])


def ret_post(o_f, o_b, proj, gn_w, w_o, res, tm):
    s, dv = o_f.shape
    d = res.shape[1]
    return pl.pallas_call(
        _ret_post_body,
        grid=(s // tm,),
        in_specs=[pl.BlockSpec((tm, dv), lambda i: (i, 0)),
                  pl.BlockSpec((tm, dv), lambda i: (i, 0)),
                  pl.BlockSpec((tm, dv), lambda i: (i, 2)),
                  pl.BlockSpec((1, dv), lambda i: (0, 0)),
                  pl.BlockSpec((dv, d), lambda i: (0, 0)),
                  pl.BlockSpec((tm, d), lambda i: (i, 0))],
        out_specs=pl.BlockSpec((tm, d), lambda i: (i, 0)),
        out_shape=jax.ShapeDtypeStruct((s, d), F32),
        compiler_params=_cparams("arbitrary"),
        name="ret_post",
    )(o_f, o_b, proj, gn_w.reshape(1, dv), w_o, res)


def _na_layer(h, norm, w_qkv, rpb, w_o):
    s = h.shape[0]
    qkv = norm_matmul(h, norm, w_qkv.astype(BF16), BF16, tm=512, tn=1024, name="na_qkv")
    o = na_attention(qkv, _na_bias_table(rpb, s // GRID_W))
    return matmul_residual(o, w_o.astype(BF16), h, tm=512, name="na_out")


def _gdn_layer(h, norm, w_in, conv_w, a_log_f, a_log_b, dt_bias_f, dt_bias_b, o_norm, w_o):
    d = GDN_HEADS * GDN_DK
    n_main = 4 * d
    proj = norm_matmul(h, norm, w_in[:, :n_main].astype(BF16), F32, tm=512, tn=1024, name="gdn_in")
    w_gate = jnp.pad(w_in[:, n_main:], ((0, 0), (0, 128 - 4 * GDN_HEADS))).astype(BF16)
    gp = norm_matmul(h, norm, w_gate, F32, tm=512, tn=128, name="gdn_in_gates")
    z8 = jnp.zeros((2 * GDN_HEADS,), F32)
    pad = jnp.zeros((128 - 4 * GDN_HEADS,), F32)
    alog_vec = jnp.concatenate([z8, a_log_f.astype(F32), a_log_b.astype(F32), pad]).reshape(1, 128)
    dtb_vec = jnp.concatenate([z8, dt_bias_f.astype(F32), dt_bias_b.astype(F32), pad]).reshape(1, 128)
    gcol, grow = gdn_gates(gp, alog_vec, dtb_vec, tm=512)
    qkv = gdn_conv(proj, jnp.transpose(conv_w).astype(F32), tm=256)
    o_f, o_b = gdn_scan(gdn_local(qkv, gcol, grow, tb=512), tb=256)
    return gdn_post(o_f, o_b, proj, o_norm, w_o.astype(BF16), h, tm=512)


def _mla_lane_maps():
    lane = np.arange(128)
    rope1 = lane < 16
    rope2 = (lane >= 64) & (lane < 80)
    nope_a = (lane >= 16) & (lane < 64)
    nope_b = (lane >= 80) & (lane < 96)
    q_dim = np.where(rope1, MLA_NOPE + lane, np.where(nope_a, lane - 16, np.where(rope2, lane + 16, lane - 32)))
    q_ok = lane < 96
    nope_dim = np.where(nope_a, lane - 16, lane - 32)
    nope_ok = nope_a | nope_b
    rope_dim = np.where(rope1, lane, lane - 64 + 16)
    rope_ok = rope1 | rope2
    return q_dim, q_ok, nope_dim, nope_ok, rope_dim, rope_ok


def _mla_layer(h, norm, w_in, q_norm, w_uq, kv_norm, w_ukv, w_o):
    s = h.shape[0]
    hh = MLA_HEADS
    dqk = MLA_NOPE + MLA_ROPE
    q_dim, q_ok, nope_dim, nope_ok, rope_dim, rope_ok = _mla_lane_maps()
    head = np.arange(hh)[:, None]
    q_cols = (head * dqk + np.where(q_ok, q_dim, 0)[None, :]).reshape(-1)
    wuq_p = jnp.where(np.tile(q_ok, hh)[None, :], w_uq[:, q_cols], 0.0).astype(BF16)
    k_cols = (head * (MLA_NOPE + MLA_V) + np.where(nope_ok, nope_dim, 0)[None, :]).reshape(-1)
    wuk_p = jnp.where(np.tile(nope_ok, hh)[None, :], w_ukv[:, k_cols], 0.0).astype(BF16)
    vx = np.arange(MLA_VX)[None, :]
    v_cols = (head * (MLA_NOPE + MLA_V) + MLA_NOPE + np.minimum(vx, MLA_V - 1)).reshape(-1)
    v_ok = np.broadcast_to(vx < MLA_V, (hh, MLA_VX)).reshape(-1)
    wuv_t = jnp.where(v_ok[:, None], jnp.transpose(w_ukv[:, v_cols]), 0.0).astype(BF16)
    vone = jnp.asarray(np.broadcast_to(vx == MLA_V, (hh, MLA_VX)).reshape(-1, 1), F32)
    n_c = MLA_Q_RANK + MLA_KV_RANK
    w_rope = jnp.where(rope_ok[None, :], w_in[:, n_c + np.where(rope_ok, rope_dim, 0)], 0.0)
    win_p = jnp.concatenate([w_in[:, :n_c], w_rope], axis=1).astype(BF16)
    half = MLA_ROPE // 2
    inv = 1.0 / (ROPE_THETA ** (jnp.arange(half, dtype=F32) / half))
    ang = jnp.arange(s, dtype=jnp.int32).astype(F32)[:, None] * inv[None, :]
    cos, sin = jnp.cos(ang), jnp.sin(ang)
    ones = jnp.ones((s, 48), F32)
    zeros = jnp.zeros((s, 48), F32)
    ct = jnp.concatenate([cos, ones, cos, ones], axis=1)
    sn = jnp.concatenate([-sin, zeros, sin, zeros], axis=1)
    q, k, v_t = mla_pre(h, norm, win_p, q_norm, kv_norm, wuq_p, wuk_p, wuv_t, vone, ct, sn, tm=256)
    o = mla_attention(q, k, v_t, tq=512)
    return matmul_residual(o, w_o.astype(BF16), h, tm=512, name="mla_out")


def _ret_tables(reverse):
    c = RET_CHUNK
    lg = jnp.log1p(-jnp.exp2(-5.0 - jnp.arange(RET_HEADS, dtype=F32)))
    if reverse:
        lg = lg[::-1]
    idx = np.arange(c)
    diff = idx[:, None] - idx[None, :]
    if reverse:
        keep, expo = diff < 0, -diff
        q_pow, k_pow = c - idx, idx
    else:
        keep, expo = diff >= 0, diff
        q_pow, k_pow = idx + 1, c - 1 - idx
    dmask = jnp.where(keep, jnp.exp(jnp.where(keep, expo, 0).astype(F32)[None] * lg[:, None, None]), 0.0)
    q_dec = jnp.exp(jnp.asarray(q_pow, F32)[None, :] * lg[:, None])
    k_dec = jnp.exp(jnp.asarray(k_pow, F32)[None, :] * lg[:, None])
    g_chunk = jnp.exp(c * lg)
    bc = lambda t: jnp.broadcast_to(t[:, :, None], (RET_HEADS, c, RET_DK))
    return [dmask, bc(q_dec), bc(k_dec), jnp.broadcast_to(g_chunk[:, None, None], (RET_HEADS, 1, RET_DV))]


def _ret_layer(h, norm, w_in, gn_w, w_o):
    s = h.shape[0]
    proj = norm_matmul(h, norm, w_in.astype(BF16), F32, tm=512, tn=1024, name="ret_in")
    half = RET_DK // 2
    inv = 1.0 / (ROPE_THETA ** (jnp.arange(half, dtype=F32) / half))
    ang = jnp.arange(s, dtype=jnp.int32).astype(F32)[:, None] * inv[None, :]
    cos, sin = jnp.cos(ang), jnp.sin(ang)
    ct = jnp.concatenate([cos, cos], axis=1)
    sn = jnp.concatenate([-sin, sin], axis=1)
    o_f, o_b = ret_scan(proj, ct, sn, _ret_tables(False), _ret_tables(True))
    return ret_post(o_f, o_b, proj, gn_w, w_o.astype(BF16), h, tm=256)


def kernel(x, na_norm, na_w_qkv, na_rpb, na_w_o, gdn_norm, gdn_w_in, gdn_conv, gdn_a_log_f, gdn_a_log_b, gdn_dt_bias_f, gdn_dt_bias_b, gdn_o_norm, gdn_w_o, mla_norm, mla_w_in, mla_q_norm, mla_w_uq, mla_kv_norm, mla_w_ukv, mla_w_o, ret_norm, ret_w_in, ret_gn, ret_w_o, mlp_norm, mlp_w1, mlp_w2, final_norm):
    b, s, d = x.shape
    depth = mlp_norm.shape[0]
    outs = []
    for bi in range(b):
        h = x[bi].astype(F32)
        for i in range(depth):
            m, j = i % 4, i // 4
            if m == 0:
                h = _na_layer(h, na_norm[j], na_w_qkv[j], na_rpb[j], na_w_o[j])
            elif m == 1:
                h = _gdn_layer(h, gdn_norm[j], gdn_w_in[j], gdn_conv[j], gdn_a_log_f[j], gdn_a_log_b[j],
                               gdn_dt_bias_f[j], gdn_dt_bias_b[j], gdn_o_norm[j], gdn_w_o[j])
            elif m == 2:
                h = _mla_layer(h, mla_norm[j], mla_w_in[j], mla_q_norm[j], mla_w_uq[j], mla_kv_norm[j],
                               mla_w_ukv[j], mla_w_o[j])
            else:
                h = _ret_layer(h, ret_norm[j], ret_w_in[j], ret_gn[j], ret_w_o[j])
            h = mlp_block(h, mlp_norm[i], mlp_w1[i].astype(BF16), mlp_w2[i].astype(BF16), final_norm,
                          final=(i == depth - 1), tm=512, tf=1024, name=f"mlp_{i}")
        outs.append(h)
    return jnp.stack(outs).astype(x.dtype)
```

```python
import functools
import math

import numpy as np
import jax
import jax.numpy as jnp
from jax import lax
from jax.experimental import pallas as pl
from jax.experimental.pallas import tpu as pltpu

F32 = jnp.float32
BF16 = jnp.bfloat16
EPS = 1e-6
ROPE_THETA = 10000.0
GRID_W = 64
NEG_BIG = -1e30
HI = lax.Precision.HIGHEST

NA_HEADS, NA_DH, NA_WIN_ROWS, NA_WIN_COLS, NA_Q_ROWS = 16, 64, 8, 16, 2
NA_KBLK = 5
GDN_HEADS, GDN_DK, GDN_CONV, GDN_CHUNK = 8, 128, 5, 64
GDN_GATE_LANES = 32
MLA_HEADS, MLA_Q_RANK, MLA_KV_RANK, MLA_NOPE, MLA_ROPE, MLA_V = 16, 768, 256, 64, 32, 64
MLA_VX = 80
RET_HEADS, RET_DK, RET_DV, RET_CHUNK = 8, 128, 256, 128

VMEM_LIMIT = 52 * 1024 * 1024


def _cparams(*sem):
    return pltpu.CompilerParams(dimension_semantics=sem, vmem_limit_bytes=VMEM_LIMIT)


def _rms(x, g):
    return x * lax.rsqrt(jnp.mean(x * x, axis=-1, keepdims=True) + EPS) * g


def _sigmoid(x):
    return 1.0 / (1.0 + jnp.exp(-x))


def _dot(a, b, **kw):
    return jnp.dot(a, b, preferred_element_type=F32, **kw)


def _dot_nt(a, b, **kw):
    return lax.dot_general(a, b, (((1,), (1,)), ((), ())), preferred_element_type=F32, **kw)


def _dot_tn(a, b, **kw):
    return lax.dot_general(a, b, (((0,), (0,)), ((), ())), preferred_element_type=F32, **kw)


def _norm_matmul_body(x_ref, g_ref, w_ref, o_ref, xn_ref):
    @pl.when(pl.program_id(1) == 0)
    def _():
        xn_ref[...] = _rms(x_ref[...], g_ref[...]).astype(BF16)

    o_ref[...] = _dot(xn_ref[...], w_ref[...]).astype(o_ref.dtype)


def norm_matmul(x, g, w, out_dtype, tm, tn, name):
    s, d = x.shape
    n = w.shape[1]
    return pl.pallas_call(
        _norm_matmul_body,
        grid=(s // tm, n // tn),
        in_specs=[pl.BlockSpec((tm, d), lambda i, j: (i, 0)),
                  pl.BlockSpec((1, d), lambda i, j: (0, 0)),
                  pl.BlockSpec((d, tn), lambda i, j: (0, j))],
        out_specs=pl.BlockSpec((tm, tn), lambda i, j: (i, j)),
        out_shape=jax.ShapeDtypeStruct((s, n), out_dtype),
        scratch_shapes=[pltpu.VMEM((tm, d), BF16)],
        compiler_params=_cparams("arbitrary", "arbitrary"),
        name=name,
    )(x, g.reshape(1, d), w)


def _matmul_res_body(a_ref, w_ref, r_ref, o_ref):
    o_ref[...] = r_ref[...] + _dot(a_ref[...], w_ref[...])


def matmul_residual(a, w, res, tm, name):
    s, k = a.shape
    n = w.shape[1]
    return pl.pallas_call(
        _matmul_res_body,
        grid=(s // tm,),
        in_specs=[pl.BlockSpec((tm, k), lambda i: (i, 0)),
                  pl.BlockSpec((k, n), lambda i: (0, 0)),
                  pl.BlockSpec((tm, n), lambda i: (i, 0))],
        out_specs=pl.BlockSpec((tm, n), lambda i: (i, 0)),
        out_shape=jax.ShapeDtypeStruct((s, n), F32),
        compiler_params=_cparams("arbitrary"),
        name=name,
    )(a, w, res)


def _mlp_body(x_ref, g_ref, w1_ref, w2_ref, fg_ref, o_ref, xn_ref, acc_ref, *, nk, final):
    k = pl.program_id(1)

    @pl.when(k == 0)
    def _():
        xn_ref[...] = _rms(x_ref[...], g_ref[...]).astype(BF16)
        acc_ref[...] = jnp.zeros_like(acc_ref)

    a = jnp.maximum(_dot(xn_ref[...], w1_ref[...]), 0.0)
    acc_ref[...] += _dot((a * a).astype(BF16), w2_ref[...])

    @pl.when(k == nk - 1)
    def _():
        y = x_ref[...] + acc_ref[...]
        if final:
            y = _rms(y, fg_ref[...])
        o_ref[...] = y


def mlp_block(x, g, w1, w2, final_g, final, tm, tf, name):
    s, d = x.shape
    f = w1.shape[1]
    nk = f // tf
    return pl.pallas_call(
        functools.partial(_mlp_body, nk=nk, final=final),
        grid=(s // tm, nk),
        in_specs=[pl.BlockSpec((tm, d), lambda i, k: (i, 0)),
                  pl.BlockSpec((1, d), lambda i, k: (0, 0)),
                  pl.BlockSpec((d, tf), lambda i, k: (0, k)),
                  pl.BlockSpec((tf, d), lambda i, k: (k, 0)),
                  pl.BlockSpec((1, d), lambda i, k: (0, 0))],
        out_specs=pl.BlockSpec((tm, d), lambda i, k: (i, 0)),
        out_shape=jax.ShapeDtypeStruct((s, d), F32),
        scratch_shapes=[pltpu.VMEM((tm, d), BF16), pltpu.VMEM((tm, d), F32)],
        compiler_params=_cparams("arbitrary", "arbitrary"),
        name=name,
    )(x, g.reshape(1, d), w1, w2, final_g.reshape(1, d))


def _na_bias_table(rpb, rows):
    n_qblk = rows // NA_Q_ROWS
    reps = {4: 0, 3: 1, 2: 2, 1: n_qblk - 2, 0: n_qblk - 1}
    w, wc, wr = GRID_W, NA_WIN_COLS, NA_WIN_ROWS
    pcol = jnp.pad(rpb.astype(F32), ((0, 0), (0, 0), (w - wc, w - wc)), mode="edge")
    e = jnp.stack([pcol[:, :, w - 1 - qc:2 * w - 1 - qc] for qc in range(w)], axis=2)
    qc = np.arange(w)
    qc0 = np.clip(qc - wc // 2, 0, w - wc)
    col_ok = (qc[None, :] >= qc0[:, None]) & (qc[None, :] < qc0[:, None] + wc)
    tabs, oks = [], []
    for t in range(5):
        i = reps[t]
        base = int(np.clip(i - 2, 0, n_qblk - NA_KBLK))
        per_qr, ok_qr = [], []
        for qr in range(NA_Q_ROWS):
            qrow = NA_Q_ROWS * i + qr
            qr0 = int(np.clip(qrow - wr // 2, 0, rows - wr))
            krows = NA_Q_ROWS * base + np.arange(NA_KBLK * NA_Q_ROWS)
            drow = np.clip(krows - qrow + wr - 1, 0, 2 * wr - 2)
            row_ok = (krows >= qr0) & (krows < qr0 + wr)
            per_qr.append(jnp.concatenate([e[:, int(d)] for d in drow], axis=-1))
            ok_qr.append(np.concatenate([col_ok & bool(r) for r in row_ok], axis=-1))
        tabs.append(jnp.concatenate(per_qr, axis=1))
        oks.append(np.concatenate(ok_qr, axis=0))
    tab = jnp.where(np.stack(oks)[:, None], jnp.stack(tabs), NEG_BIG)
    return jnp.swapaxes(tab, 2, 3)


def _na_body(q_ref, k0, k1, k2, k3, k4, v0, v1, v2, v3, v4, b_ref, o_ref):
    krefs = (k0, k1, k2, k3, k4)
    vrefs = (v0, v1, v2, v3, v4)
    nkb = NA_Q_ROWS * GRID_W
    lane = lax.broadcasted_iota(jnp.int32, (NA_KBLK * nkb, NA_DH), 1)
    ones_col = jnp.where(lane == 0, 1.0, 0.0).astype(BF16)
    for h in range(NA_HEADS):
        sl = slice(h * NA_DH, (h + 1) * NA_DH)
        k_all = jnp.concatenate([kr[:, sl] for kr in krefs], axis=0)
        v_all = jnp.concatenate([vr[:, sl] for vr in vrefs], axis=0)
        s_t = _dot_nt(k_all, q_ref[:, sl]) + b_ref[0, h]
        m = jnp.max(s_t, axis=0, keepdims=True)
        p_t = jnp.exp(s_t - m).astype(BF16)
        o = _dot_tn(p_t, jnp.concatenate([v_all, ones_col], axis=1))
        o_ref[:, sl] = (o[:, :NA_DH] / o[:, NA_DH:NA_DH + 1]).astype(o_ref.dtype)


def na_attention(qkv, bias_tab):
    s = qkv.shape[0]
    d = NA_HEADS * NA_DH
    tq = NA_Q_ROWS * GRID_W
    n_qblk = s // tq
    top = n_qblk - NA_KBLK

    def base(i):
        return jnp.clip(i - 2, 0, top)

    kv_specs = [pl.BlockSpec((tq, d), functools.partial(lambda i, dd, c: (base(i) + dd, c), dd=dd, c=c))
                for c in (1, 2) for dd in range(NA_KBLK)]
    return pl.pallas_call(
        _na_body,
        grid=(n_qblk,),
        in_specs=[pl.BlockSpec((tq, d), lambda i: (i, 0))] + kv_specs
                 + [pl.BlockSpec((1, NA_HEADS, NA_KBLK * tq, tq), lambda i: (base(i) - i + 4, 0, 0, 0))],
        out_specs=pl.BlockSpec((tq, d), lambda i: (i, 0)),
        out_shape=jax.ShapeDtypeStruct((s, d), BF16),
        compiler_params=_cparams("arbitrary"),
        name="na_attention",
    )(qkv, *([qkv] * (2 * NA_KBLK)), bias_tab)


def _gdn_conv_body(x_ref, xp_ref, xn_ref, w_ref, o_ref, *, tm, nt):
    i = pl.program_id(0)
    j = pl.program_id(1)
    prev = jnp.where(i > 0, xp_ref[...], 0.0)
    nxt = jnp.where(i < nt - 1, xn_ref[...], 0.0)
    xe = jnp.concatenate([prev, x_ref[...], nxt], axis=0)
    w = w_ref[...]
    half = GDN_CONV // 2
    acc = xe[8 - half:8 - half + tm] * w[0:1]
    for t in range(1, GDN_CONV):
        acc = acc + xe[8 - half + t:8 - half + t + tm] * w[t:t + 1]
    y = acc * _sigmoid(acc)
    for h in range(GDN_HEADS):
        sl = slice(h * GDN_DK, (h + 1) * GDN_DK)
        yh = y[:, sl]
        nrm = lax.rsqrt(jnp.sum(yh * yh, axis=1, keepdims=True) + EPS)
        fac = jnp.where(j == 0, nrm * GDN_DK ** -0.5, jnp.where(j == 1, nrm, 1.0))
        o_ref[:, sl] = (yh * fac).astype(o_ref.dtype)


def gdn_conv(proj, conv_w_t, tm):
    s = proj.shape[0]
    d = GDN_HEADS * GDN_DK
    nt = s // tm
    r8 = tm // 8
    return pl.pallas_call(
        functools.partial(_gdn_conv_body, tm=tm, nt=nt),
        grid=(nt, 3),
        in_specs=[pl.BlockSpec((tm, d), lambda i, j: (i, j)),
                  pl.BlockSpec((8, d), lambda i, j: (jnp.maximum(i * r8 - 1, 0), j)),
                  pl.BlockSpec((8, d), lambda i, j: (jnp.minimum((i + 1) * r8, nt * r8 - 1), j)),
                  pl.BlockSpec((GDN_CONV, d), lambda i, j: (0, j))],
        out_specs=pl.BlockSpec((tm, d), lambda i, j: (i, j)),
        out_shape=jax.ShapeDtypeStruct((s, 3 * d), BF16),
        compiler_params=_cparams("arbitrary", "arbitrary"),
        name="gdn_conv",
    )(proj, proj, proj, conv_w_t)


def _gdn_gates_body(gp_ref, alog_ref, dtb_ref, col_ref, row_ref, *, tm):
    c = GDN_CHUNK
    gp = gp_ref[...]
    lane = lax.broadcasted_iota(jnp.int32, (tm, 128), 1)
    is_f = (lane >= 16) & (lane < 24)
    is_b = (lane >= 24) & (lane < 32)
    beta = _sigmoid(gp)
    z = gp + dtb_ref[...]
    softplus = jnp.maximum(z, 0.0) + jnp.log1p(jnp.exp(-jnp.abs(z)))
    g = jnp.where(is_f | is_b, -jnp.exp(alog_ref[...]) * softplus, 0.0)
    r = lax.broadcasted_iota(jnp.int32, (c, c), 0)
    cc = lax.broadcasted_iota(jnp.int32, (c, c), 1)
    tri_lo = (r >= cc).astype(F32)
    tri_up = (r <= cc).astype(F32)
    lane_c = lax.broadcasted_iota(jnp.int32, (c, 128), 1)
    is_f_c = (lane_c >= 16) & (lane_c < 24)
    is_g_c = (lane_c >= 16) & (lane_c < 32)
    for ch in range(tm // c):
        gc = g[ch * c:(ch + 1) * c]
        cum = jnp.where(is_f_c, _dot(tri_lo, gc, precision=HI), _dot(tri_up, gc, precision=HI))
        colc = jnp.where(is_g_c, cum, beta[ch * c:(ch + 1) * c])
        col_ref[ch * c:(ch + 1) * c, :] = colc
        row_ref[ch] = colc.T[:GDN_GATE_LANES, :]


def gdn_gates(gp, alog_vec, dtb_vec, tm):
    s = gp.shape[0]
    c = GDN_CHUNK
    return pl.pallas_call(
        functools.partial(_gdn_gates_body, tm=tm),
        grid=(s // tm,),
        in_specs=[pl.BlockSpec((tm, 128), lambda i: (i, 0)),
                  pl.BlockSpec((1, 128), lambda i: (0, 0)),
                  pl.BlockSpec((1, 128), lambda i: (0, 0))],
        out_specs=[pl.BlockSpec((tm, 128), lambda i: (i, 0)),
                   pl.BlockSpec((tm // c, GDN_GATE_LANES, c), lambda i: (i, 0, 0))],
        out_shape=[jax.ShapeDtypeStruct((s, 128), F32), jax.ShapeDtypeStruct((s // c, GDN_GATE_LANES, c), F32)],
        compiler_params=_cparams("arbitrary"),
        name="gdn_gates",
    )(gp, alog_vec, dtb_vec)


def _bmm(a, b):
    return jnp.einsum("nik,nkj->nij", a, b, preferred_element_type=F32)


def _bmm_nt(a, b):
    return jnp.einsum("nik,njk->nij", a, b, preferred_element_type=F32)


def _unit_tri_inverse(lmat, r, cc):
    eye = (r == cc).astype(F32)
    diag_blk = (r // 16) == (cc // 16)
    ld = jnp.where(diag_blk, lmat, 0.0)
    lo = lmat - ld
    x = eye - ld
    p = _bmm(ld, ld)
    x = x + _bmm(x, p)
    p = _bmm(p, p)
    x = x + _bmm(x, p)
    p = _bmm(p, p)
    dinv = x + _bmm(x, p)
    n1 = _bmm(dinv, lo)
    n2 = _bmm(n1, n1)
    y = eye - n1 + n2 - _bmm(n1, n2)
    return _bmm(y, dinv)


def _gdn_local_body(qkv_ref, col_ref, row_ref, *out_refs, tb):
    c, dk, hh = GDN_CHUNK, GDN_DK, GDN_HEADS
    nc = tb // c
    r = lax.broadcasted_iota(jnp.int32, (nc, c, c), 1)
    cc = lax.broadcasted_iota(jnp.int32, (nc, c, c), 2)
    col3 = col_ref[...].reshape(nc, c, 128)
    for d in range(2):
        reverse = d == 1
        u_ref, w_ref, qd_ref, kd_ref, a_ref, gl_ref = out_refs[6 * d:6 * d + 6]
        incl = (r <= cc) if reverse else (r >= cc)
        strict = (r < cc) if reverse else (r > cc)
        for h in range(hh):
            sl = slice(h * dk, (h + 1) * dk)
            b_lane = h + (8 if reverse else 0)
            g_lane = h + (24 if reverse else 16)
            qc = qkv_ref[:, sl].astype(F32).reshape(nc, c, dk)
            kc = qkv_ref[:, hh * dk + h * dk:hh * dk + (h + 1) * dk].astype(F32).reshape(nc, c, dk)
            vc = qkv_ref[:, 2 * hh * dk + h * dk:2 * hh * dk + (h + 1) * dk].astype(F32).reshape(nc, c, dk)
            beta = col3[:, :, b_lane:b_lane + 1]
            gcol = col3[:, :, g_lane:g_lane + 1]
            grow = row_ref[:, g_lane:g_lane + 1, :]
            g_last = gcol[:, 0:1, :] if reverse else gcol[:, c - 1:c, :]
            decay = jnp.where(incl, jnp.exp(jnp.where(incl, gcol - grow, 0.0)), 0.0)
            eg = jnp.exp(gcol)
            kb = kc * beta
            kc16 = kc.astype(BF16)
            lmat = jnp.where(strict, _bmm_nt(kb.astype(BF16), kc16) * decay, 0.0)
            tinv = _unit_tri_inverse(lmat, r, cc)
            sol = _bmm(tinv, jnp.concatenate([vc * beta, kb * eg], axis=2))
            a_intra = jnp.where(incl, _bmm_nt(qc.astype(BF16), kc16) * decay, 0.0)
            u_ref[:, sl] = sol[:, :, :dk].reshape(tb, dk)
            w_ref[:, sl] = sol[:, :, dk:].reshape(tb, dk).astype(BF16)
            qd_ref[:, sl] = (qc * eg).reshape(tb, dk).astype(BF16)
            kd_ref[:, sl] = (kc * jnp.exp(g_last - gcol)).reshape(tb, dk).astype(BF16)
            a_ref[:, h * c:(h + 1) * c] = a_intra.reshape(tb, c).astype(BF16)
            gl_ref[:, h:h + 1, :] = jnp.broadcast_to(jnp.exp(g_last), (nc, 1, dk))


def gdn_local(qkv, gcol, grow, tb):
    s = qkv.shape[0]
    c, d = GDN_CHUNK, GDN_HEADS * GDN_DK
    nc = tb // c
    row = lambda w: pl.BlockSpec((tb, w), lambda i: (i, 0))
    per_dir_specs = [row(d), row(d), row(d), row(d), row(GDN_HEADS * c),
                     pl.BlockSpec((nc, GDN_HEADS, GDN_DK), lambda i: (i, 0, 0))]
    per_dir_shapes = [jax.ShapeDtypeStruct((s, d), F32)] + [jax.ShapeDtypeStruct((s, d), BF16)] * 3 + [
        jax.ShapeDtypeStruct((s, GDN_HEADS * c), BF16), jax.ShapeDtypeStruct((s // c, GDN_HEADS, GDN_DK), F32)]
    return pl.pallas_call(
        functools.partial(_gdn_local_body, tb=tb),
        grid=(s // tb,),
        in_specs=[row(3 * d), row(128), pl.BlockSpec((nc, GDN_GATE_LANES, c), lambda i: (i, 0, 0))],
        out_specs=per_dir_specs * 2,
        out_shape=per_dir_shapes * 2,
        compiler_params=_cparams("arbitrary"),
        name="gdn_local",
    )(qkv, gcol, grow)


def _gdn_scan_body(*refs, tb):
    c, dk, hh = GDN_CHUNK, GDN_DK, GDN_HEADS
    nc = tb // c
    ins, (of_ref, ob_ref, st_ref) = refs[:12], refs[12:]

    @pl.when(pl.program_id(0) == 0)
    def _():
        st_ref[...] = jnp.zeros_like(st_ref)

    for step in range(nc):
        for d in range(2):
            u_ref, w_ref, qd_ref, kd_ref, a_ref, gl_ref = ins[6 * d:6 * d + 6]
            o_ref = (of_ref, ob_ref)[d]
            ch = nc - 1 - step if d == 1 else step
            rows = slice(ch * c, (ch + 1) * c)
            for h in range(hh):
                sl = slice(h * dk, (h + 1) * dk)
                state = st_ref[d, h]
                st16 = state.astype(BF16)
                v_new = u_ref[rows, sl] - _dot(w_ref[rows, sl], st16)
                vn16 = v_new.astype(BF16)
                o_ref[rows, sl] = _dot(qd_ref[rows, sl], st16) + _dot(a_ref[rows, h * c:(h + 1) * c], vn16)
                st_ref[d, h] = state * gl_ref[ch, h:h + 1, :] + _dot_tn(kd_ref[rows, sl], vn16)


def gdn_scan(local_out, tb):
    s, d = local_out[0].shape
    c = GDN_CHUNK
    nb = s // tb
    nc = tb // c

    def dir_specs(blk):
        row = lambda w: pl.BlockSpec((tb, w), lambda t: (blk(t), 0))
        return [row(d), row(d), row(d), row(d), row(GDN_HEADS * c),
                pl.BlockSpec((nc, GDN_HEADS, GDN_DK), lambda t: (blk(t), 0, 0))]

    fwd = lambda t: t
    bwd = lambda t: nb - 1 - t
    return pl.pallas_call(
        functools.partial(_gdn_scan_body, tb=tb),
        grid=(nb,),
        in_specs=dir_specs(fwd) + dir_specs(bwd),
        out_specs=[pl.BlockSpec((tb, d), lambda t: (t, 0)), pl.BlockSpec((tb, d), lambda t: (nb - 1 - t, 0))],
        out_shape=[jax.ShapeDtypeStruct((s, d), F32)] * 2,
        scratch_shapes=[pltpu.VMEM((2, GDN_HEADS, GDN_DK, GDN_DK), F32)],
        compiler_params=_cparams("arbitrary"),
        name="gdn_scan",
    )(*local_out)


def _gdn_post_body(of_ref, ob_ref, z_ref, on_ref, w_ref, r_ref, o_ref):
    on = on_ref[...]
    parts = []
    for h in range(GDN_HEADS):
        sl = slice(h * GDN_DK, (h + 1) * GDN_DK)
        o = of_ref[:, sl] + ob_ref[:, sl]
        z = z_ref[:, sl]
        parts.append((_rms(o, on) * (z * _sigmoid(z))).astype(BF16))
    y = jnp.concatenate(parts, axis=1)
    o_ref[...] = r_ref[...] + _dot(y, w_ref[...])


def gdn_post(o_f, o_b, proj, o_norm, w_o, res, tm):
    s, d = o_f.shape
    return pl.pallas_call(
        _gdn_post_body,
        grid=(s // tm,),
        in_specs=[pl.BlockSpec((tm, d), lambda i: (i, 0)),
                  pl.BlockSpec((tm, d), lambda i: (i, 0)),
                  pl.BlockSpec((tm, d), lambda i: (i, 3)),
                  pl.BlockSpec((1, GDN_DK), lambda i: (0, 0)),
                  pl.BlockSpec((d, d), lambda i: (0, 0)),
                  pl.BlockSpec((tm, d), lambda i: (i, 0))],
        out_specs=pl.BlockSpec((tm, d), lambda i: (i, 0)),
        out_shape=jax.ShapeDtypeStruct((s, d), F32),
        compiler_params=_cparams("arbitrary"),
        name="gdn_post",
    )(o_f, o_b, proj, o_norm.reshape(1, GDN_DK), w_o, res)


def _mla_pre_body(x_ref, g_ref, win_ref, qn_ref, kvn_ref, wuq_ref, wuk_ref, wuv_ref, vone_ref, ct_ref, sn_ref,
                  ctt_ref, snt_ref, q_ref, k_ref, v_ref):
    xn = _rms(x_ref[...], g_ref[...]).astype(BF16)
    proj = _dot(xn, win_ref[...])
    cq = _rms(proj[:, :MLA_Q_RANK], qn_ref[...]).astype(BF16)
    ckv = _rms(proj[:, MLA_Q_RANK:MLA_Q_RANK + MLA_KV_RANK], kvn_ref[...]).astype(BF16)
    pr = proj[:, MLA_Q_RANK + MLA_KV_RANK:]
    ct, sn = ct_ref[...], sn_ref[...]
    scale = (MLA_NOPE + MLA_ROPE) ** -0.5 * math.log2(math.e)
    ctt, snt = ctt_ref[...], snt_ref[...]
    q_t = _dot_nt(wuq_ref[...], cq)
    for h in range(MLA_HEADS):
        qh = q_t[h * 128:(h + 1) * 128, :]
        partner = jnp.concatenate([qh[64:, :], qh[:64, :]], axis=0)
        q_ref[h * 128:(h + 1) * 128, :] = ((qh * ctt + partner * snt) * scale).astype(BF16)
    kr = pr * ct + pltpu.roll(pr, 64, 1) * sn
    k = _dot(ckv, wuk_ref[...])
    for h in range(MLA_HEADS):
        sl = slice(h * 128, (h + 1) * 128)
        k_ref[:, sl] = (k[:, sl] + kr).astype(BF16)
    v_t = (_dot_nt(wuv_ref[...], ckv) + vone_ref[...]).astype(BF16)
    v_ref[...] = v_t.reshape(v_ref.shape)


def mla_pre(x, g, win_p, q_norm, kv_norm, wuq_t, wuk_p, wuv_t, vone, ct, sn, tm):
    s, d = x.shape
    hp = MLA_HEADS * 128
    npair = MLA_HEADS // 2
    full = lambda a: pl.BlockSpec(a.shape, lambda i: (0,) * a.ndim)
    g2, qn2, kvn2 = g.reshape(1, d), q_norm.reshape(1, -1), kv_norm.reshape(1, -1)
    ct_t, sn_t = jnp.transpose(ct), jnp.transpose(sn)
    return pl.pallas_call(
        _mla_pre_body,
        grid=(s // tm,),
        in_specs=[pl.BlockSpec((tm, d), lambda i: (i, 0)), full(g2), full(win_p), full(qn2), full(kvn2),
                  full(wuq_t), full(wuk_p), full(wuv_t), full(vone),
                  pl.BlockSpec((tm, 128), lambda i: (i, 0)), pl.BlockSpec((tm, 128), lambda i: (i, 0)),
                  pl.BlockSpec((128, tm), lambda i: (0, i)), pl.BlockSpec((128, tm), lambda i: (0, i))],
        out_specs=[pl.BlockSpec((hp, tm), lambda i: (0, i)), pl.BlockSpec((tm, hp), lambda i: (i, 0)),
                   pl.BlockSpec((npair, 1, 2 * MLA_VX, tm), lambda i: (0, i, 0, 0))],
        out_shape=[jax.ShapeDtypeStruct((hp, s), BF16), jax.ShapeDtypeStruct((s, hp), BF16),
                   jax.ShapeDtypeStruct((npair, s // tm, 2 * MLA_VX, tm), BF16)],
        compiler_params=_cparams("arbitrary"),
        name="mla_pre",
    )(x, g2, win_p, qn2, kvn2, wuq_t, wuk_p, wuv_t, vone, ct, sn, ct_t, sn_t)


def _mla_attn_body(qt_ref, k_ref, vt_ref, o_ref, acc_ref, s_ref, *, tq, tk, nk):
    acc_ref[...] = jnp.zeros_like(acc_ref)

    def scores(j, slot):
        off = pl.multiple_of(j * tk, tk)
        for hh in range(2):
            sl = slice(hh * 128, (hh + 1) * 128)
            s_ref[slot, hh] = _dot(k_ref[pl.ds(off, tk), sl], qt_ref[sl, :])

    def consume(j, slot, carry):
        new = []
        for hh in range(2):
            m_old = carry[hh]
            rows = slice(hh * MLA_VX, (hh + 1) * MLA_VX)
            s = s_ref[slot, hh]
            m_new = jnp.maximum(m_old, jnp.max(s, axis=0, keepdims=True))
            p = jnp.exp2(s - m_new).astype(BF16)
            acc_ref[rows, :] = acc_ref[rows, :] * jnp.exp2(m_old - m_new) + _dot(vt_ref[0, j, rows, :], p)
            new.append(m_new)
        return tuple(new)

    scores(0, 0)

    def body(jj, carry):
        j = 2 * jj
        scores(j + 1, 1)
        carry = consume(j, 0, carry)
        scores(jnp.minimum(j + 2, nk - 1), 0)
        return consume(j + 1, 1, carry)

    lax.fori_loop(0, nk // 2, body, (jnp.full((1, tq), -jnp.inf, F32),) * 2)
    o_t = jnp.concatenate([acc_ref[hh * MLA_VX:hh * MLA_VX + MLA_V, :]
                           / acc_ref[hh * MLA_VX + MLA_V:hh * MLA_VX + MLA_V + 1, :] for hh in range(2)], axis=0)
    o_ref[...] = o_t.T.astype(o_ref.dtype)


def mla_attention(q_t, k, v_t, tq):
    s = k.shape[0]
    npair, nk, _, tk = v_t.shape
    return pl.pallas_call(
        functools.partial(_mla_attn_body, tq=tq, tk=tk, nk=nk),
        grid=(npair, s // tq),
        in_specs=[pl.BlockSpec((256, tq), lambda p, i: (p, i)),
                  pl.BlockSpec((s, 256), lambda p, i: (0, p)),
                  pl.BlockSpec((1, nk, 2 * MLA_VX, tk), lambda p, i: (p, 0, 0, 0))],
        out_specs=pl.BlockSpec((tq, 128), lambda p, i: (i, p)),
        out_shape=jax.ShapeDtypeStruct((s, MLA_HEADS * MLA_V), BF16),
        scratch_shapes=[pltpu.VMEM((2 * MLA_VX, tq), F32), pltpu.VMEM((2, 2, tk, tq), F32)],
        compiler_params=_cparams("arbitrary", "arbitrary"),
        name="mla_attention",
    )(q_t, k, v_t)


def _ret_direction(q_ref, k_ref, v_ref, ct_ref, sn_ref, dm_ref, qd_ref, kd_ref, gc_ref, o_ref, st_ref):
    ct, sn = ct_ref[...], sn_ref[...]
    for h in range(RET_HEADS):
        sk = slice(h * RET_DK, (h + 1) * RET_DK)
        sv = slice(h * RET_DV, (h + 1) * RET_DV)
        qh = q_ref[:, sk]
        kh = k_ref[:, sk]
        qr = qh * ct + pltpu.roll(qh, 64, 1) * sn
        kr = (kh * ct + pltpu.roll(kh, 64, 1) * sn) * RET_DK ** -0.5
        v16 = v_ref[:, sv].astype(BF16)
        state = st_ref[h]
        inner = _dot_nt(qr.astype(BF16), kr.astype(BF16)) * dm_ref[h]
        o_ref[:, sv] = (_dot(inner.astype(BF16), v16)
                        + _dot((qr * qd_ref[h]).astype(BF16), state.astype(BF16)))
        st_ref[h] = state * gc_ref[h] + _dot_tn((kr * kd_ref[h]).astype(BF16), v16)


def _ret_scan_body(qf, kf, vf, ctf, snf, qb, kb, vb, ctb, snb,
                   dmf, qdf, kdf, gcf, dmb, qdb, kdb, gcb, of_ref, ob_ref, sf_ref, sb_ref):
    @pl.when(pl.program_id(0) == 0)
    def _():
        sf_ref[...] = jnp.zeros_like(sf_ref)
        sb_ref[...] = jnp.zeros_like(sb_ref)

    _ret_direction(qf, kf, vf, ctf, snf, dmf, qdf, kdf, gcf, of_ref, sf_ref)
    _ret_direction(qb, kb, vb, ctb, snb, dmb, qdb, kdb, gcb, ob_ref, sb_ref)


def ret_scan(proj, ct, sn, tabs_f, tabs_b):
    s = proj.shape[0]
    c = RET_CHUNK
    nb = s // c
    dq = RET_HEADS * RET_DK
    dv = RET_HEADS * RET_DV

    def dir_specs(blk):
        return [pl.BlockSpec((c, dq), lambda t: (blk(t), 0)),
                pl.BlockSpec((c, dq), lambda t: (blk(t), 1)),
                pl.BlockSpec((c, dv), lambda t: (blk(t), 1)),
                pl.BlockSpec((c, 128), lambda t: (blk(t), 0)),
                pl.BlockSpec((c, 128), lambda t: (blk(t), 0))]

    full = lambda a: pl.BlockSpec(a.shape, lambda t: (0,) * a.ndim)
    fwd = lambda t: t
    bwd = lambda t: nb - 1 - t
    return pl.pallas_call(
        _ret_scan_body,
        grid=(nb,),
        in_specs=dir_specs(fwd) + dir_specs(bwd) + [full(a) for a in tabs_f] + [full(a) for a in tabs_b],
        out_specs=[pl.BlockSpec((c, dv), lambda t: (t, 0)), pl.BlockSpec((c, dv), lambda t: (nb - 1 - t, 0))],
        out_shape=[jax.ShapeDtypeStruct((s, dv), F32)] * 2,
        scratch_shapes=[pltpu.VMEM((RET_HEADS, RET_DK, RET_DV), F32)] * 2,
        compiler_params=_cparams("arbitrary"),
        name="ret_scan",
    )(proj, proj, proj, ct, sn, proj, proj, proj, ct, sn, *tabs_f, *tabs_b)


def _ret_post_body(of_ref, ob_ref, gate_ref, gn_ref, w_ref, r_ref, o_ref):
    parts = []
    for h in range(RET_HEADS):
        sv = slice(h * RET_DV, (h + 1) * RET_DV)
        o = of_ref[:, sv] + ob_ref[:, sv]
        mu = jnp.mean(o, axis=1, keepdims=True)
        oc = o - mu
        var = jnp.mean(oc * oc, axis=1, keepdims=True)
        gate = gate_ref[:, sv]
        parts.append((gate * _sigmoid(gate) * (oc * lax.rsqrt(var + EPS) * gn_ref[:, sv])).astype(BF16))
    y = jnp.concatenate(parts, axis=1)
    o_ref[...] = r_ref[...] + _dot(y, w_ref[...])


def ret_post(o_f, o_b, proj, gn_w, w_o, res, tm):
    s, dv = o_f.shape
    d = res.shape[1]
    return pl.pallas_call(
        _ret_post_body,
        grid=(s // tm,),
        in_specs=[pl.BlockSpec((tm, dv), lambda i: (i, 0)),
                  pl.BlockSpec((tm, dv), lambda i: (i, 0)),
                  pl.BlockSpec((tm, dv), lambda i: (i, 2)),
                  pl.BlockSpec((1, dv), lambda i: (0, 0)),
                  pl.BlockSpec((dv, d), lambda i: (0, 0)),
                  pl.BlockSpec((tm, d), lambda i: (i, 0))],
        out_specs=pl.BlockSpec((tm, d), lambda i: (i, 0)),
        out_shape=jax.ShapeDtypeStruct((s, d), F32),
        compiler_params=_cparams("arbitrary"),
        name="ret_post",
    )(o_f, o_b, proj, gn_w.reshape(1, dv), w_o, res)


def _na_layer(h, norm, w_qkv, rpb, w_o):
    s = h.shape[0]
    d = NA_HEADS * NA_DH
    col_scale = jnp.where(jnp.arange(3 * d) < d, NA_DH ** -0.5, 1.0).astype(F32)
    w16 = (w_qkv * col_scale[None, :]).astype(BF16)
    qkv = norm_matmul(h, norm, w16, BF16, tm=512, tn=3 * d, name="na_qkv")
    o = na_attention(qkv, _na_bias_table(rpb, s // GRID_W))
    return matmul_residual(o, w_o.astype(BF16), h, tm=512, name="na_out")


def _gdn_layer(h, norm, w_in, conv_w, a_log_f, a_log_b, dt_bias_f, dt_bias_b, o_norm, w_o):
    d = GDN_HEADS * GDN_DK
    n_main = 4 * d
    proj = norm_matmul(h, norm, w_in[:, :n_main].astype(BF16), F32, tm=512, tn=n_main, name="gdn_in")
    w_gate = jnp.pad(w_in[:, n_main:], ((0, 0), (0, 128 - 4 * GDN_HEADS))).astype(BF16)
    gp = norm_matmul(h, norm, w_gate, F32, tm=512, tn=128, name="gdn_in_gates")
    z8 = jnp.zeros((2 * GDN_HEADS,), F32)
    pad = jnp.zeros((128 - 4 * GDN_HEADS,), F32)
    alog_vec = jnp.concatenate([z8, a_log_f.astype(F32), a_log_b.astype(F32), pad]).reshape(1, 128)
    dtb_vec = jnp.concatenate([z8, dt_bias_f.astype(F32), dt_bias_b.astype(F32), pad]).reshape(1, 128)
    gcol, grow = gdn_gates(gp, alog_vec, dtb_vec, tm=512)
    qkv = gdn_conv(proj, jnp.transpose(conv_w).astype(F32), tm=256)
    o_f, o_b = gdn_scan(gdn_local(qkv, gcol, grow, tb=512), tb=256)
    return gdn_post(o_f, o_b, proj, o_norm, w_o.astype(BF16), h, tm=512)


def _mla_lane_maps():
    lane = np.arange(128)
    rope1 = lane < 16
    rope2 = (lane >= 64) & (lane < 80)
    nope_a = (lane >= 16) & (lane < 64)
    nope_b = (lane >= 80) & (lane < 96)
    q_dim = np.where(rope1, MLA_NOPE + lane, np.where(nope_a, lane - 16, np.where(rope2, lane + 16, lane - 32)))
    q_ok = lane < 96
    nope_dim = np.where(nope_a, lane - 16, lane - 32)
    nope_ok = nope_a | nope_b
    rope_dim = np.where(rope1, lane, lane - 64 + 16)
    rope_ok = rope1 | rope2
    return q_dim, q_ok, nope_dim, nope_ok, rope_dim, rope_ok


def _mla_layer(h, norm, w_in, q_norm, w_uq, kv_norm, w_ukv, w_o):
    s = h.shape[0]
    hh = MLA_HEADS
    dqk = MLA_NOPE + MLA_ROPE
    q_dim, q_ok, nope_dim, nope_ok, rope_dim, rope_ok = _mla_lane_maps()
    head = np.arange(hh)[:, None]
    q_cols = (head * dqk + np.where(q_ok, q_dim, 0)[None, :]).reshape(-1)
    wuq_t = jnp.transpose(jnp.where(np.tile(q_ok, hh)[None, :], w_uq[:, q_cols], 0.0)).astype(BF16)
    k_cols = (head * (MLA_NOPE + MLA_V) + np.where(nope_ok, nope_dim, 0)[None, :]).reshape(-1)
    wuk_p = jnp.where(np.tile(nope_ok, hh)[None, :], w_ukv[:, k_cols], 0.0).astype(BF16)
    vx = np.arange(MLA_VX)[None, :]
    v_cols = (head * (MLA_NOPE + MLA_V) + MLA_NOPE + np.minimum(vx, MLA_V - 1)).reshape(-1)
    v_ok = np.broadcast_to(vx < MLA_V, (hh, MLA_VX)).reshape(-1)
    wuv_t = jnp.where(v_ok[:, None], jnp.transpose(w_ukv[:, v_cols]), 0.0).astype(BF16)
    vone = jnp.asarray(np.broadcast_to(vx == MLA_V, (hh, MLA_VX)).reshape(-1, 1), F32)
    n_c = MLA_Q_RANK + MLA_KV_RANK
    w_rope = jnp.where(rope_ok[None, :], w_in[:, n_c + np.where(rope_ok, rope_dim, 0)], 0.0)
    win_p = jnp.concatenate([w_in[:, :n_c], w_rope], axis=1).astype(BF16)
    half = MLA_ROPE // 2
    inv = 1.0 / (ROPE_THETA ** (jnp.arange(half, dtype=F32) / half))
    ang = jnp.arange(s, dtype=jnp.int32).astype(F32)[:, None] * inv[None, :]
    cos, sin = jnp.cos(ang), jnp.sin(ang)
    ones = jnp.ones((s, 48), F32)
    zeros = jnp.zeros((s, 48), F32)
    ct = jnp.concatenate([cos, ones, cos, ones], axis=1)
    sn = jnp.concatenate([-sin, zeros, sin, zeros], axis=1)
    q_t, k, v_t = mla_pre(h, norm, win_p, q_norm, kv_norm, wuq_t, wuk_p, wuv_t, vone, ct, sn, tm=512)
    o = mla_attention(q_t, k, v_t, tq=min(2048, s))
    return matmul_residual(o, w_o.astype(BF16), h, tm=512, name="mla_out")


def _ret_tables(reverse):
    c = RET_CHUNK
    lg = jnp.log1p(-jnp.exp2(-5.0 - jnp.arange(RET_HEADS, dtype=F32)))
    if reverse:
        lg = lg[::-1]
    idx = np.arange(c)
    diff = idx[:, None] - idx[None, :]
    if reverse:
        keep, expo = diff < 0, -diff
        q_pow, k_pow = c - idx, idx
    else:
        keep, expo = diff >= 0, diff
        q_pow, k_pow = idx + 1, c - 1 - idx
    dmask = jnp.where(keep, jnp.exp(jnp.where(keep, expo, 0).astype(F32)[None] * lg[:, None, None]), 0.0)
    q_dec = jnp.exp(jnp.asarray(q_pow, F32)[None, :] * lg[:, None])
    k_dec = jnp.exp(jnp.asarray(k_pow, F32)[None, :] * lg[:, None])
    g_chunk = jnp.exp(c * lg)
    bc = lambda t: jnp.broadcast_to(t[:, :, None], (RET_HEADS, c, RET_DK))
    return [dmask, bc(q_dec), bc(k_dec), jnp.broadcast_to(g_chunk[:, None, None], (RET_HEADS, 1, RET_DV))]


def _ret_layer(h, norm, w_in, gn_w, w_o):
    s = h.shape[0]
    proj = norm_matmul(h, norm, w_in.astype(BF16), F32, tm=256, tn=w_in.shape[1], name="ret_in")
    half = RET_DK // 2
    inv = 1.0 / (ROPE_THETA ** (jnp.arange(half, dtype=F32) / half))
    ang = jnp.arange(s, dtype=jnp.int32).astype(F32)[:, None] * inv[None, :]
    cos, sin = jnp.cos(ang), jnp.sin(ang)
    ct = jnp.concatenate([cos, cos], axis=1)
    sn = jnp.concatenate([-sin, sin], axis=1)
    o_f, o_b = ret_scan(proj, ct, sn, _ret_tables(False), _ret_tables(True))
    return ret_post(o_f, o_b, proj, gn_w, w_o.astype(BF16), h, tm=256)


def kernel(x, na_norm, na_w_qkv, na_rpb, na_w_o, gdn_norm, gdn_w_in, gdn_conv, gdn_a_log_f, gdn_a_log_b, gdn_dt_bias_f, gdn_dt_bias_b, gdn_o_norm, gdn_w_o, mla_norm, mla_w_in, mla_q_norm, mla_w_uq, mla_kv_norm, mla_w_ukv, mla_w_o, ret_norm, ret_w_in, ret_gn, ret_w_o, mlp_norm, mlp_w1, mlp_w2, final_norm):
    b, s, d = x.shape
    depth = mlp_norm.shape[0]
    outs = []
    for bi in range(b):
        h = x[bi].astype(F32)
        for i in range(depth):
            m, j = i % 4, i // 4
            if m == 0:
                h = _na_layer(h, na_norm[j], na_w_qkv[j], na_rpb[j], na_w_o[j])
            elif m == 1:
                h = _gdn_layer(h, gdn_norm[j], gdn_w_in[j], gdn_conv[j], gdn_a_log_f[j], gdn_a_log_b[j],
                               gdn_dt_bias_f[j], gdn_dt_bias_b[j], gdn_o_norm[j], gdn_w_o[j])
            elif m == 2:
                h = _mla_layer(h, mla_norm[j], mla_w_in[j], mla_q_norm[j], mla_w_uq[j], mla_kv_norm[j],
                               mla_w_ukv[j], mla_w_o[j])
            else:
                h = _ret_layer(h, ret_norm[j], ret_w_in[j], ret_gn[j], ret_w_o[j])
            h = mlp_block(h, mlp_norm[i], mlp_w1[i].astype(BF16), mlp_w2[i].astype(BF16), final_norm,
                          final=(i == depth - 1), tm=512, tf=mlp_w1.shape[2], name=f"mlp_{i}")
        outs.append(h)
    return jnp.stack(outs).astype(x.dtype)
```

```python
import functools
import math

import numpy as np
import jax
import jax.numpy as jnp
from jax import lax
from jax.experimental import pallas as pl
from jax.experimental.pallas import tpu as pltpu

F32 = jnp.float32
BF16 = jnp.bfloat16
EPS = 1e-6
ROPE_THETA = 10000.0
GRID_W = 64
NEG_BIG = -1e30
HI = lax.Precision.HIGHEST

NA_HEADS, NA_DH, NA_WIN_ROWS, NA_WIN_COLS, NA_Q_ROWS = 16, 64, 8, 16, 2
NA_KBLK = 5
GDN_HEADS, GDN_DK, GDN_CONV, GDN_CHUNK = 8, 128, 5, 64
GDN_GATE_LANES = 32
MLA_HEADS, MLA_Q_RANK, MLA_KV_RANK, MLA_NOPE, MLA_ROPE, MLA_V = 16, 768, 256, 64, 32, 64
MLA_VX = 80
RET_HEADS, RET_DK, RET_DV, RET_CHUNK = 8, 128, 256, 128

VMEM_LIMIT = 52 * 1024 * 1024


def _cparams(*sem):
    return pltpu.CompilerParams(dimension_semantics=sem, vmem_limit_bytes=VMEM_LIMIT)


def _rms(x, g):
    return x * lax.rsqrt(jnp.mean(x * x, axis=-1, keepdims=True) + EPS) * g


def _sigmoid(x):
    return 1.0 / (1.0 + jnp.exp(-x))


def _dot(a, b, **kw):
    return jnp.dot(a, b, preferred_element_type=F32, **kw)


def _dot_nt(a, b, **kw):
    return lax.dot_general(a, b, (((1,), (1,)), ((), ())), preferred_element_type=F32, **kw)


def _dot_tn(a, b, **kw):
    return lax.dot_general(a, b, (((0,), (0,)), ((), ())), preferred_element_type=F32, **kw)


def _norm_matmul_body(x_ref, g_ref, w_ref, o_ref, xn_ref):
    @pl.when(pl.program_id(1) == 0)
    def _():
        xn_ref[...] = _rms(x_ref[...], g_ref[...]).astype(BF16)

    o_ref[...] = _dot(xn_ref[...], w_ref[...]).astype(o_ref.dtype)


def norm_matmul(x, g, w, out_dtype, tm, tn, name):
    s, d = x.shape
    n = w.shape[1]
    return pl.pallas_call(
        _norm_matmul_body,
        grid=(s // tm, n // tn),
        in_specs=[pl.BlockSpec((tm, d), lambda i, j: (i, 0)),
                  pl.BlockSpec((1, d), lambda i, j: (0, 0)),
                  pl.BlockSpec((d, tn), lambda i, j: (0, j))],
        out_specs=pl.BlockSpec((tm, tn), lambda i, j: (i, j)),
        out_shape=jax.ShapeDtypeStruct((s, n), out_dtype),
        scratch_shapes=[pltpu.VMEM((tm, d), BF16)],
        compiler_params=_cparams("arbitrary", "arbitrary"),
        name=name,
    )(x, g.reshape(1, d), w)


def _matmul_res_body(a_ref, w_ref, r_ref, o_ref):
    o_ref[...] = r_ref[...] + _dot(a_ref[...], w_ref[...])


def matmul_residual(a, w, res, tm, name):
    s, k = a.shape
    n = w.shape[1]
    return pl.pallas_call(
        _matmul_res_body,
        grid=(s // tm,),
        in_specs=[pl.BlockSpec((tm, k), lambda i: (i, 0)),
                  pl.BlockSpec((k, n), lambda i: (0, 0)),
                  pl.BlockSpec((tm, n), lambda i: (i, 0))],
        out_specs=pl.BlockSpec((tm, n), lambda i: (i, 0)),
        out_shape=jax.ShapeDtypeStruct((s, n), F32),
        compiler_params=_cparams("arbitrary"),
        name=name,
    )(a, w, res)


def _mlp_body(x_ref, g_ref, w1_ref, w2_ref, fg_ref, o_ref, xn_ref, acc_ref, *, nk, final):
    k = pl.program_id(1)

    @pl.when(k == 0)
    def _():
        xn_ref[...] = _rms(x_ref[...], g_ref[...]).astype(BF16)
        acc_ref[...] = jnp.zeros_like(acc_ref)

    a = jnp.maximum(_dot(xn_ref[...], w1_ref[...]), 0.0)
    acc_ref[...] += _dot((a * a).astype(BF16), w2_ref[...])

    @pl.when(k == nk - 1)
    def _():
        y = x_ref[...] + acc_ref[...]
        if final:
            y = _rms(y, fg_ref[...])
        o_ref[...] = y


def mlp_block(x, g, w1, w2, final_g, final, tm, tf, name):
    s, d = x.shape
    f = w1.shape[1]
    nk = f // tf
    return pl.pallas_call(
        functools.partial(_mlp_body, nk=nk, final=final),
        grid=(s // tm, nk),
        in_specs=[pl.BlockSpec((tm, d), lambda i, k: (i, 0)),
                  pl.BlockSpec((1, d), lambda i, k: (0, 0)),
                  pl.BlockSpec((d, tf), lambda i, k: (0, k)),
                  pl.BlockSpec((tf, d), lambda i, k: (k, 0)),
                  pl.BlockSpec((1, d), lambda i, k: (0, 0))],
        out_specs=pl.BlockSpec((tm, d), lambda i, k: (i, 0)),
        out_shape=jax.ShapeDtypeStruct((s, d), F32),
        scratch_shapes=[pltpu.VMEM((tm, d), BF16), pltpu.VMEM((tm, d), F32)],
        compiler_params=_cparams("arbitrary", "arbitrary"),
        name=name,
    )(x, g.reshape(1, d), w1, w2, final_g.reshape(1, d))


def _na_bias_table(rpb, rows):
    n_qblk = rows // NA_Q_ROWS
    reps = {4: 0, 3: 1, 2: 2, 1: n_qblk - 2, 0: n_qblk - 1}
    w, wc, wr = GRID_W, NA_WIN_COLS, NA_WIN_ROWS
    pcol = jnp.pad(rpb.astype(F32), ((0, 0), (0, 0), (w - wc, w - wc)), mode="edge")
    e = jnp.stack([pcol[:, :, w - 1 - qc:2 * w - 1 - qc] for qc in range(w)], axis=2)
    qc = np.arange(w)
    qc0 = np.clip(qc - wc // 2, 0, w - wc)
    col_ok = (qc[None, :] >= qc0[:, None]) & (qc[None, :] < qc0[:, None] + wc)
    tabs, oks = [], []
    for t in range(5):
        i = reps[t]
        base = int(np.clip(i - 2, 0, n_qblk - NA_KBLK))
        per_qr, ok_qr = [], []
        for qr in range(NA_Q_ROWS):
            qrow = NA_Q_ROWS * i + qr
            qr0 = int(np.clip(qrow - wr // 2, 0, rows - wr))
            krows = NA_Q_ROWS * base + np.arange(NA_KBLK * NA_Q_ROWS)
            drow = np.clip(krows - qrow + wr - 1, 0, 2 * wr - 2)
            row_ok = (krows >= qr0) & (krows < qr0 + wr)
            per_qr.append(jnp.concatenate([e[:, int(d)] for d in drow], axis=-1))
            ok_qr.append(np.concatenate([col_ok & bool(r) for r in row_ok], axis=-1))
        tabs.append(jnp.concatenate(per_qr, axis=1))
        oks.append(np.concatenate(ok_qr, axis=0))
    tab = jnp.where(np.stack(oks)[:, None], jnp.stack(tabs), NEG_BIG)
    return jnp.swapaxes(tab, 2, 3)


def _na_body(q_ref, k0, k1, k2, k3, k4, v0, v1, v2, v3, v4, b_ref, o_ref):
    krefs = (k0, k1, k2, k3, k4)
    vrefs = (v0, v1, v2, v3, v4)
    nkb = NA_Q_ROWS * GRID_W
    lane = lax.broadcasted_iota(jnp.int32, (NA_KBLK * nkb, NA_DH), 1)
    ones_col = jnp.where(lane == 0, 1.0, 0.0).astype(BF16)
    for h in range(NA_HEADS):
        sl = slice(h * NA_DH, (h + 1) * NA_DH)
        k_all = jnp.concatenate([kr[:, sl] for kr in krefs], axis=0)
        v_all = jnp.concatenate([vr[:, sl] for vr in vrefs], axis=0)
        s_t = _dot_nt(k_all, q_ref[:, sl]) + b_ref[0, h]
        m = jnp.max(s_t, axis=0, keepdims=True)
        p_t = jnp.exp(s_t - m).astype(BF16)
        o = _dot_tn(p_t, jnp.concatenate([v_all, ones_col], axis=1))
        o_ref[:, sl] = (o[:, :NA_DH] / o[:, NA_DH:NA_DH + 1]).astype(o_ref.dtype)


def na_attention(qkv, bias_tab):
    s = qkv.shape[0]
    d = NA_HEADS * NA_DH
    tq = NA_Q_ROWS * GRID_W
    n_qblk = s // tq
    top = n_qblk - NA_KBLK

    def base(i):
        return jnp.clip(i - 2, 0, top)

    kv_specs = [pl.BlockSpec((tq, d), functools.partial(lambda i, dd, c: (base(i) + dd, c), dd=dd, c=c))
                for c in (1, 2) for dd in range(NA_KBLK)]
    return pl.pallas_call(
        _na_body,
        grid=(n_qblk,),
        in_specs=[pl.BlockSpec((tq, d), lambda i: (i, 0))] + kv_specs
                 + [pl.BlockSpec((1, NA_HEADS, NA_KBLK * tq, tq), lambda i: (base(i) - i + 4, 0, 0, 0))],
        out_specs=pl.BlockSpec((tq, d), lambda i: (i, 0)),
        out_shape=jax.ShapeDtypeStruct((s, d), BF16),
        compiler_params=_cparams("arbitrary"),
        name="na_attention",
    )(qkv, *([qkv] * (2 * NA_KBLK)), bias_tab)


def _gdn_conv_body(x_ref, xp_ref, xn_ref, w_ref, o_ref, *, tm, nt):
    i = pl.program_id(0)
    j = pl.program_id(1)
    prev = jnp.where(i > 0, xp_ref[...], 0.0)
    nxt = jnp.where(i < nt - 1, xn_ref[...], 0.0)
    xe = jnp.concatenate([prev, x_ref[...], nxt], axis=0)
    w = w_ref[...]
    half = GDN_CONV // 2
    acc = xe[8 - half:8 - half + tm] * w[0:1]
    for t in range(1, GDN_CONV):
        acc = acc + xe[8 - half + t:8 - half + t + tm] * w[t:t + 1]
    y = acc * _sigmoid(acc)
    for h in range(GDN_HEADS):
        sl = slice(h * GDN_DK, (h + 1) * GDN_DK)
        yh = y[:, sl]
        nrm = lax.rsqrt(jnp.sum(yh * yh, axis=1, keepdims=True) + EPS)
        fac = jnp.where(j == 0, nrm * GDN_DK ** -0.5, jnp.where(j == 1, nrm, 1.0))
        o_ref[:, sl] = (yh * fac).astype(o_ref.dtype)


def gdn_conv(proj, conv_w_t, tm):
    s = proj.shape[0]
    d = GDN_HEADS * GDN_DK
    nt = s // tm
    r8 = tm // 8
    return pl.pallas_call(
        functools.partial(_gdn_conv_body, tm=tm, nt=nt),
        grid=(nt, 3),
        in_specs=[pl.BlockSpec((tm, d), lambda i, j: (i, j)),
                  pl.BlockSpec((8, d), lambda i, j: (jnp.maximum(i * r8 - 1, 0), j)),
                  pl.BlockSpec((8, d), lambda i, j: (jnp.minimum((i + 1) * r8, nt * r8 - 1), j)),
                  pl.BlockSpec((GDN_CONV, d), lambda i, j: (0, j))],
        out_specs=pl.BlockSpec((tm, d), lambda i, j: (i, j)),
        out_shape=jax.ShapeDtypeStruct((s, 3 * d), BF16),
        compiler_params=_cparams("arbitrary", "arbitrary"),
        name="gdn_conv",
    )(proj, proj, proj, conv_w_t)


def _gdn_gates_body(gp_ref, alog_ref, dtb_ref, col_ref, row_ref, *, tm):
    c = GDN_CHUNK
    gp = gp_ref[...]
    lane = lax.broadcasted_iota(jnp.int32, (tm, 128), 1)
    is_f = (lane >= 16) & (lane < 24)
    is_b = (lane >= 24) & (lane < 32)
    beta = _sigmoid(gp)
    z = gp + dtb_ref[...]
    softplus = jnp.maximum(z, 0.0) + jnp.log1p(jnp.exp(-jnp.abs(z)))
    g = jnp.where(is_f | is_b, -jnp.exp(alog_ref[...]) * softplus, 0.0)
    r = lax.broadcasted_iota(jnp.int32, (c, c), 0)
    cc = lax.broadcasted_iota(jnp.int32, (c, c), 1)
    tri_lo = (r >= cc).astype(F32)
    tri_up = (r <= cc).astype(F32)
    lane_c = lax.broadcasted_iota(jnp.int32, (c, 128), 1)
    is_f_c = (lane_c >= 16) & (lane_c < 24)
    is_g_c = (lane_c >= 16) & (lane_c < 32)
    for ch in range(tm // c):
        gc = g[ch * c:(ch + 1) * c]
        cum = jnp.where(is_f_c, _dot(tri_lo, gc, precision=HI), _dot(tri_up, gc, precision=HI))
        colc = jnp.where(is_g_c, cum, beta[ch * c:(ch + 1) * c])
        col_ref[ch * c:(ch + 1) * c, :] = colc
        row_ref[ch] = colc.T[:GDN_GATE_LANES, :]


def gdn_gates(gp, alog_vec, dtb_vec, tm):
    s = gp.shape[0]
    c = GDN_CHUNK
    return pl.pallas_call(
        functools.partial(_gdn_gates_body, tm=tm),
        grid=(s // tm,),
        in_specs=[pl.BlockSpec((tm, 128), lambda i: (i, 0)),
                  pl.BlockSpec((1, 128), lambda i: (0, 0)),
                  pl.BlockSpec((1, 128), lambda i: (0, 0))],
        out_specs=[pl.BlockSpec((tm, 128), lambda i: (i, 0)),
                   pl.BlockSpec((tm // c, GDN_GATE_LANES, c), lambda i: (i, 0, 0))],
        out_shape=[jax.ShapeDtypeStruct((s, 128), F32), jax.ShapeDtypeStruct((s // c, GDN_GATE_LANES, c), F32)],
        compiler_params=_cparams("arbitrary"),
        name="gdn_gates",
    )(gp, alog_vec, dtb_vec)


def _bmm(a, b):
    return jnp.einsum("nik,nkj->nij", a, b, preferred_element_type=F32)


def _bmm_nt(a, b):
    return jnp.einsum("nik,njk->nij", a, b, preferred_element_type=F32)


def _unit_tri_inverse(lmat, r, cc):
    eye = (r == cc).astype(F32)
    diag_blk = (r // 16) == (cc // 16)
    ld = jnp.where(diag_blk, lmat, 0.0)
    lo = lmat - ld
    x = eye - ld
    p = _bmm(ld, ld)
    x = x + _bmm(x, p)
    p = _bmm(p, p)
    x = x + _bmm(x, p)
    p = _bmm(p, p)
    dinv = x + _bmm(x, p)
    n1 = _bmm(dinv, lo)
    n2 = _bmm(n1, n1)
    y = eye - n1 + n2 - _bmm(n1, n2)
    return _bmm(y, dinv)


def _gdn_local_body(qkv_ref, col_ref, row_ref, *out_refs, tb):
    c, dk, hh = GDN_CHUNK, GDN_DK, GDN_HEADS
    nc = tb // c
    nb = hh * nc
    r = lax.broadcasted_iota(jnp.int32, (nb, c, c), 1)
    cc = lax.broadcasted_iota(jnp.int32, (nb, c, c), 2)
    col3 = col_ref[...].reshape(nc, c, 128)

    def heads(base):
        return jnp.concatenate([qkv_ref[:, base + h * dk:base + (h + 1) * dk].astype(F32).reshape(nc, c, dk)
                                for h in range(hh)], axis=0)

    qc, kc, vc = heads(0), heads(hh * dk), heads(2 * hh * dk)
    kc16 = kc.astype(BF16)
    qk = _bmm_nt(qc.astype(BF16), kc16)
    for d in range(2):
        reverse = d == 1
        u_ref, wq_ref, ak_ref, gl_ref = out_refs[4 * d:4 * d + 4]
        incl = (r <= cc) if reverse else (r >= cc)
        strict = (r < cc) if reverse else (r > cc)
        b0 = 8 if reverse else 0
        g0 = 24 if reverse else 16
        beta = jnp.concatenate([col3[:, :, b0 + h:b0 + h + 1] for h in range(hh)], axis=0)
        gcol = jnp.concatenate([col3[:, :, g0 + h:g0 + h + 1] for h in range(hh)], axis=0)
        grow = jnp.concatenate([row_ref[:, g0 + h:g0 + h + 1, :] for h in range(hh)], axis=0)
        g_last = gcol[:, 0:1, :] if reverse else gcol[:, c - 1:c, :]
        decay = jnp.where(incl, jnp.exp(jnp.where(incl, gcol - grow, 0.0)), 0.0)
        eg = jnp.exp(gcol)
        kb = kc * beta
        lmat = jnp.where(strict, _bmm_nt(kb.astype(BF16), kc16) * decay, 0.0)
        tinv = _unit_tri_inverse(lmat, r, cc)
        sol = _bmm(tinv, jnp.concatenate([vc * beta, kb * eg], axis=2))
        a_intra = jnp.where(incl, qk * decay, 0.0).astype(BF16)
        w16 = sol[:, :, dk:].astype(BF16)
        qd16 = (qc * eg).astype(BF16)
        kdt16 = jnp.swapaxes(kc * jnp.exp(g_last - gcol), 1, 2).astype(BF16)
        gl = jnp.broadcast_to(jnp.exp(g_last), (nb, 1, dk))
        for h in range(hh):
            sl = slice(h * dk, (h + 1) * dk)
            hb = slice(h * nc, (h + 1) * nc)
            u_ref[:, sl] = sol[hb, :, :dk].reshape(tb, dk)
            wq_ref[:, 0:c, sl] = w16[hb]
            wq_ref[:, c:2 * c, sl] = qd16[hb]
            ak_ref[:, 0:c, h * c:(h + 1) * c] = a_intra[hb]
            ak_ref[:, c:c + dk, h * c:(h + 1) * c] = kdt16[hb]
            gl_ref[:, h:h + 1, :] = gl[hb]


def gdn_local(qkv, gcol, grow, tb):
    s = qkv.shape[0]
    c, d = GDN_CHUNK, GDN_HEADS * GDN_DK
    nc = tb // c
    row = lambda w: pl.BlockSpec((tb, w), lambda i: (i, 0))
    per_dir_specs = [row(d),
                     pl.BlockSpec((nc, 2 * c, d), lambda i: (i, 0, 0)),
                     pl.BlockSpec((nc, c + GDN_DK, GDN_HEADS * c), lambda i: (i, 0, 0)),
                     pl.BlockSpec((nc, GDN_HEADS, GDN_DK), lambda i: (i, 0, 0))]
    per_dir_shapes = [jax.ShapeDtypeStruct((s, d), F32),
                      jax.ShapeDtypeStruct((s // c, 2 * c, d), BF16),
                      jax.ShapeDtypeStruct((s // c, c + GDN_DK, GDN_HEADS * c), BF16),
                      jax.ShapeDtypeStruct((s // c, GDN_HEADS, GDN_DK), F32)]
    return pl.pallas_call(
        functools.partial(_gdn_local_body, tb=tb),
        grid=(s // tb,),
        in_specs=[row(3 * d), row(128), pl.BlockSpec((nc, GDN_GATE_LANES, c), lambda i: (i, 0, 0))],
        out_specs=per_dir_specs * 2,
        out_shape=per_dir_shapes * 2,
        compiler_params=_cparams("arbitrary"),
        name="gdn_local",
    )(qkv, gcol, grow)


def _gdn_scan_body(*refs, tb):
    c, dk, hh = GDN_CHUNK, GDN_DK, GDN_HEADS
    nc = tb // c
    ins, (of_ref, ob_ref, st_ref) = refs[:8], refs[8:]

    @pl.when(pl.program_id(0) == 0)
    def _():
        st_ref[...] = jnp.zeros_like(st_ref)

    for step in range(nc):
        for d in range(2):
            u_ref, wq_ref, ak_ref, gl_ref = ins[4 * d:4 * d + 4]
            o_ref = (of_ref, ob_ref)[d]
            ch = nc - 1 - step if d == 1 else step
            rows = slice(ch * c, (ch + 1) * c)
            for h in range(hh):
                sl = slice(h * dk, (h + 1) * dk)
                state = st_ref[d, h]
                ws_qs = _dot(wq_ref[ch, :, sl], state.astype(BF16))
                v_new = u_ref[rows, sl] - ws_qs[:c]
                av_kv = _dot(ak_ref[ch, :, h * c:(h + 1) * c], v_new.astype(BF16))
                o_ref[rows, sl] = ws_qs[c:] + av_kv[:c]
                st_ref[d, h] = state * gl_ref[ch, h:h + 1, :] + av_kv[c:]


def gdn_scan(local_out, tb):
    s, d = local_out[0].shape
    c = GDN_CHUNK
    nb = s // tb
    nc = tb // c

    def dir_specs(blk):
        return [pl.BlockSpec((tb, d), lambda t: (blk(t), 0)),
                pl.BlockSpec((nc, 2 * c, d), lambda t: (blk(t), 0, 0)),
                pl.BlockSpec((nc, c + GDN_DK, GDN_HEADS * c), lambda t: (blk(t), 0, 0)),
                pl.BlockSpec((nc, GDN_HEADS, GDN_DK), lambda t: (blk(t), 0, 0))]

    fwd = lambda t: t
    bwd = lambda t: nb - 1 - t
    return pl.pallas_call(
        functools.partial(_gdn_scan_body, tb=tb),
        grid=(nb,),
        in_specs=dir_specs(fwd) + dir_specs(bwd),
        out_specs=[pl.BlockSpec((tb, d), lambda t: (t, 0)), pl.BlockSpec((tb, d), lambda t: (nb - 1 - t, 0))],
        out_shape=[jax.ShapeDtypeStruct((s, d), F32)] * 2,
        scratch_shapes=[pltpu.VMEM((2, GDN_HEADS, GDN_DK, GDN_DK), F32)],
        compiler_params=_cparams("arbitrary"),
        name="gdn_scan",
    )(*local_out)


def _gdn_post_body(of_ref, ob_ref, z_ref, on_ref, w_ref, r_ref, o_ref):
    on = on_ref[...]
    parts = []
    for h in range(GDN_HEADS):
        sl = slice(h * GDN_DK, (h + 1) * GDN_DK)
        o = of_ref[:, sl] + ob_ref[:, sl]
        z = z_ref[:, sl]
        parts.append((_rms(o, on) * (z * _sigmoid(z))).astype(BF16))
    y = jnp.concatenate(parts, axis=1)
    o_ref[...] = r_ref[...] + _dot(y, w_ref[...])


def gdn_post(o_f, o_b, proj, o_norm, w_o, res, tm):
    s, d = o_f.shape
    return pl.pallas_call(
        _gdn_post_body,
        grid=(s // tm,),
        in_specs=[pl.BlockSpec((tm, d), lambda i: (i, 0)),
                  pl.BlockSpec((tm, d), lambda i: (i, 0)),
                  pl.BlockSpec((tm, d), lambda i: (i, 3)),
                  pl.BlockSpec((1, GDN_DK), lambda i: (0, 0)),
                  pl.BlockSpec((d, d), lambda i: (0, 0)),
                  pl.BlockSpec((tm, d), lambda i: (i, 0))],
        out_specs=pl.BlockSpec((tm, d), lambda i: (i, 0)),
        out_shape=jax.ShapeDtypeStruct((s, d), F32),
        compiler_params=_cparams("arbitrary"),
        name="gdn_post",
    )(o_f, o_b, proj, o_norm.reshape(1, GDN_DK), w_o, res)


def _mla_pre_body(x_ref, g_ref, win_ref, qn_ref, kvn_ref, wuq_ref, wuk_ref, wuv_ref, vone_ref, ct_ref, sn_ref,
                  ctt_ref, snt_ref, q_ref, k_ref, v_ref):
    xn = _rms(x_ref[...], g_ref[...]).astype(BF16)
    proj = _dot(xn, win_ref[...])
    cq = _rms(proj[:, :MLA_Q_RANK], qn_ref[...]).astype(BF16)
    ckv = _rms(proj[:, MLA_Q_RANK:MLA_Q_RANK + MLA_KV_RANK], kvn_ref[...]).astype(BF16)
    pr = proj[:, MLA_Q_RANK + MLA_KV_RANK:]
    ct, sn = ct_ref[...], sn_ref[...]
    scale = (MLA_NOPE + MLA_ROPE) ** -0.5 * math.log2(math.e)
    ctt, snt = ctt_ref[...], snt_ref[...]
    q_t = _dot_nt(wuq_ref[...], cq)
    for h in range(MLA_HEADS):
        qh = q_t[h * 128:(h + 1) * 128, :]
        partner = jnp.concatenate([qh[64:, :], qh[:64, :]], axis=0)
        q_ref[h * 128:(h + 1) * 128, :] = ((qh * ctt + partner * snt) * scale).astype(BF16)
    kr = pr * ct + pltpu.roll(pr, 64, 1) * sn
    k = _dot(ckv, wuk_ref[...])
    for h in range(MLA_HEADS):
        sl = slice(h * 128, (h + 1) * 128)
        k_ref[:, sl] = (k[:, sl] + kr).astype(BF16)
    v_t = (_dot_nt(wuv_ref[...], ckv) + vone_ref[...]).astype(BF16)
    v_ref[...] = v_t.reshape(v_ref.shape)


def mla_pre(x, g, win_p, q_norm, kv_norm, wuq_t, wuk_p, wuv_t, vone, ct, sn, tm):
    s, d = x.shape
    hp = MLA_HEADS * 128
    npair = MLA_HEADS // 2
    full = lambda a: pl.BlockSpec(a.shape, lambda i: (0,) * a.ndim)
    g2, qn2, kvn2 = g.reshape(1, d), q_norm.reshape(1, -1), kv_norm.reshape(1, -1)
    ct_t, sn_t = jnp.transpose(ct), jnp.transpose(sn)
    return pl.pallas_call(
        _mla_pre_body,
        grid=(s // tm,),
        in_specs=[pl.BlockSpec((tm, d), lambda i: (i, 0)), full(g2), full(win_p), full(qn2), full(kvn2),
                  full(wuq_t), full(wuk_p), full(wuv_t), full(vone),
                  pl.BlockSpec((tm, 128), lambda i: (i, 0)), pl.BlockSpec((tm, 128), lambda i: (i, 0)),
                  pl.BlockSpec((128, tm), lambda i: (0, i)), pl.BlockSpec((128, tm), lambda i: (0, i))],
        out_specs=[pl.BlockSpec((hp, tm), lambda i: (0, i)), pl.BlockSpec((tm, hp), lambda i: (i, 0)),
                   pl.BlockSpec((npair, 1, 2 * MLA_VX, tm), lambda i: (0, i, 0, 0))],
        out_shape=[jax.ShapeDtypeStruct((hp, s), BF16), jax.ShapeDtypeStruct((s, hp), BF16),
                   jax.ShapeDtypeStruct((npair, s // tm, 2 * MLA_VX, tm), BF16)],
        compiler_params=_cparams("arbitrary"),
        name="mla_pre",
    )(x, g2, win_p, qn2, kvn2, wuq_t, wuk_p, wuv_t, vone, ct, sn, ct_t, sn_t)


def _mla_attn_body(qt_ref, k_ref, vt_ref, o_ref, acc_ref, s_ref, *, tq, tk, nk):
    acc_ref[...] = jnp.zeros_like(acc_ref)

    def scores(j, slot):
        off = pl.multiple_of(j * tk, tk)
        for hh in range(2):
            sl = slice(hh * 128, (hh + 1) * 128)
            s_ref[slot, hh] = _dot(k_ref[pl.ds(off, tk), sl], qt_ref[sl, :])

    def consume(j, slot, carry):
        new = []
        for hh in range(2):
            m_old = carry[hh]
            rows = slice(hh * MLA_VX, (hh + 1) * MLA_VX)
            s = s_ref[slot, hh]
            m_new = jnp.maximum(m_old, jnp.max(s, axis=0, keepdims=True))
            p = jnp.exp2(s - m_new).astype(BF16)
            acc_ref[rows, :] = acc_ref[rows, :] * jnp.exp2(m_old - m_new) + _dot(vt_ref[0, j, rows, :], p)
            new.append(m_new)
        return tuple(new)

    scores(0, 0)

    def body(jj, carry):
        j = 2 * jj
        scores(j + 1, 1)
        carry = consume(j, 0, carry)
        scores(jnp.minimum(j + 2, nk - 1), 0)
        return consume(j + 1, 1, carry)

    lax.fori_loop(0, nk // 2, body, (jnp.full((1, tq), -jnp.inf, F32),) * 2)
    o_t = jnp.concatenate([acc_ref[hh * MLA_VX:hh * MLA_VX + MLA_V, :]
                           / acc_ref[hh * MLA_VX + MLA_V:hh * MLA_VX + MLA_V + 1, :] for hh in range(2)], axis=0)
    o_ref[...] = o_t.T.astype(o_ref.dtype)


def mla_attention(q_t, k, v_t, tq):
    s = k.shape[0]
    npair, nk, _, tk = v_t.shape
    return pl.pallas_call(
        functools.partial(_mla_attn_body, tq=tq, tk=tk, nk=nk),
        grid=(npair, s // tq),
        in_specs=[pl.BlockSpec((256, tq), lambda p, i: (p, i)),
                  pl.BlockSpec((s, 256), lambda p, i: (0, p)),
                  pl.BlockSpec((1, nk, 2 * MLA_VX, tk), lambda p, i: (p, 0, 0, 0))],
        out_specs=pl.BlockSpec((tq, 128), lambda p, i: (i, p)),
        out_shape=jax.ShapeDtypeStruct((s, MLA_HEADS * MLA_V), BF16),
        scratch_shapes=[pltpu.VMEM((2 * MLA_VX, tq), F32), pltpu.VMEM((2, 2, tk, tq), F32)],
        compiler_params=_cparams("arbitrary", "arbitrary"),
        name="mla_attention",
    )(q_t, k, v_t)


def _ret_direction(q_ref, k_ref, v_ref, ct_ref, sn_ref, dm_ref, qd_ref, kd_ref, gc_ref, o_ref, st_ref):
    ct, sn = ct_ref[...], sn_ref[...]
    for h in range(RET_HEADS):
        sk = slice(h * RET_DK, (h + 1) * RET_DK)
        sv = slice(h * RET_DV, (h + 1) * RET_DV)
        qh = q_ref[:, sk]
        kh = k_ref[:, sk]
        qr = qh * ct + pltpu.roll(qh, 64, 1) * sn
        kr = (kh * ct + pltpu.roll(kh, 64, 1) * sn) * RET_DK ** -0.5
        v16 = v_ref[:, sv].astype(BF16)
        state = st_ref[h]
        inner = _dot_nt(qr.astype(BF16), kr.astype(BF16)) * dm_ref[h]
        o_ref[:, sv] = (_dot(inner.astype(BF16), v16)
                        + _dot((qr * qd_ref[h]).astype(BF16), state.astype(BF16)))
        st_ref[h] = state * gc_ref[h] + _dot_tn((kr * kd_ref[h]).astype(BF16), v16)


def _ret_scan_body(qf, kf, vf, ctf, snf, qb, kb, vb, ctb, snb,
                   dmf, qdf, kdf, gcf, dmb, qdb, kdb, gcb, of_ref, ob_ref, sf_ref, sb_ref):
    @pl.when(pl.program_id(0) == 0)
    def _():
        sf_ref[...] = jnp.zeros_like(sf_ref)
        sb_ref[...] = jnp.zeros_like(sb_ref)

    _ret_direction(qf, kf, vf, ctf, snf, dmf, qdf, kdf, gcf, of_ref, sf_ref)
    _ret_direction(qb, kb, vb, ctb, snb, dmb, qdb, kdb, gcb, ob_ref, sb_ref)


def ret_scan(proj, ct, sn, tabs_f, tabs_b):
    s = proj.shape[0]
    c = RET_CHUNK
    nb = s // c
    dq = RET_HEADS * RET_DK
    dv = RET_HEADS * RET_DV

    def dir_specs(blk):
        return [pl.BlockSpec((c, dq), lambda t: (blk(t), 0)),
                pl.BlockSpec((c, dq), lambda t: (blk(t), 1)),
                pl.BlockSpec((c, dv), lambda t: (blk(t), 1)),
                pl.BlockSpec((c, 128), lambda t: (blk(t), 0)),
                pl.BlockSpec((c, 128), lambda t: (blk(t), 0))]

    full = lambda a: pl.BlockSpec(a.shape, lambda t: (0,) * a.ndim)
    fwd = lambda t: t
    bwd = lambda t: nb - 1 - t
    return pl.pallas_call(
        _ret_scan_body,
        grid=(nb,),
        in_specs=dir_specs(fwd) + dir_specs(bwd) + [full(a) for a in tabs_f] + [full(a) for a in tabs_b],
        out_specs=[pl.BlockSpec((c, dv), lambda t: (t, 0)), pl.BlockSpec((c, dv), lambda t: (nb - 1 - t, 0))],
        out_shape=[jax.ShapeDtypeStruct((s, dv), F32)] * 2,
        scratch_shapes=[pltpu.VMEM((RET_HEADS, RET_DK, RET_DV), F32)] * 2,
        compiler_params=_cparams("arbitrary"),
        name="ret_scan",
    )(proj, proj, proj, ct, sn, proj, proj, proj, ct, sn, *tabs_f, *tabs_b)


def _ret_post_body(of_ref, ob_ref, gate_ref, gn_ref, w_ref, r_ref, o_ref):
    parts = []
    for h in range(RET_HEADS):
        sv = slice(h * RET_DV, (h + 1) * RET_DV)
        o = of_ref[:, sv] + ob_ref[:, sv]
        mu = jnp.mean(o, axis=1, keepdims=True)
        oc = o - mu
        var = jnp.mean(oc * oc, axis=1, keepdims=True)
        gate = gate_ref[:, sv]
        parts.append((gate * _sigmoid(gate) * (oc * lax.rsqrt(var + EPS) * gn_ref[:, sv])).astype(BF16))
    y = jnp.concatenate(parts, axis=1)
    o_ref[...] = r_ref[...] + _dot(y, w_ref[...])


def ret_post(o_f, o_b, proj, gn_w, w_o, res, tm):
    s, dv = o_f.shape
    d = res.shape[1]
    return pl.pallas_call(
        _ret_post_body,
        grid=(s // tm,),
        in_specs=[pl.BlockSpec((tm, dv), lambda i: (i, 0)),
                  pl.BlockSpec((tm, dv), lambda i: (i, 0)),
                  pl.BlockSpec((tm, dv), lambda i: (i, 2)),
                  pl.BlockSpec((1, dv), lambda i: (0, 0)),
                  pl.BlockSpec((dv, d), lambda i: (0, 0)),
                  pl.BlockSpec((tm, d), lambda i: (i, 0))],
        out_specs=pl.BlockSpec((tm, d), lambda i: (i, 0)),
        out_shape=jax.ShapeDtypeStruct((s, d), F32),
        compiler_params=_cparams("arbitrary"),
        name="ret_post",
    )(o_f, o_b, proj, gn_w.reshape(1, dv), w_o, res)


def _na_layer(h, norm, w_qkv, rpb, w_o):
    s = h.shape[0]
    d = NA_HEADS * NA_DH
    col_scale = jnp.where(jnp.arange(3 * d) < d, NA_DH ** -0.5, 1.0).astype(F32)
    w16 = (w_qkv * col_scale[None, :]).astype(BF16)
    qkv = norm_matmul(h, norm, w16, BF16, tm=512, tn=3 * d, name="na_qkv")
    o = na_attention(qkv, _na_bias_table(rpb, s // GRID_W))
    return matmul_residual(o, w_o.astype(BF16), h, tm=512, name="na_out")


def _gdn_layer(h, norm, w_in, conv_w, a_log_f, a_log_b, dt_bias_f, dt_bias_b, o_norm, w_o):
    d = GDN_HEADS * GDN_DK
    n_main = 4 * d
    proj = norm_matmul(h, norm, w_in[:, :n_main].astype(BF16), F32, tm=512, tn=n_main, name="gdn_in")
    w_gate = jnp.pad(w_in[:, n_main:], ((0, 0), (0, 128 - 4 * GDN_HEADS))).astype(BF16)
    gp = norm_matmul(h, norm, w_gate, F32, tm=512, tn=128, name="gdn_in_gates")
    z8 = jnp.zeros((2 * GDN_HEADS,), F32)
    pad = jnp.zeros((128 - 4 * GDN_HEADS,), F32)
    alog_vec = jnp.concatenate([z8, a_log_f.astype(F32), a_log_b.astype(F32), pad]).reshape(1, 128)
    dtb_vec = jnp.concatenate([z8, dt_bias_f.astype(F32), dt_bias_b.astype(F32), pad]).reshape(1, 128)
    gcol, grow = gdn_gates(gp, alog_vec, dtb_vec, tm=512)
    qkv = gdn_conv(proj, jnp.transpose(conv_w).astype(F32), tm=256)
    o_f, o_b = gdn_scan(gdn_local(qkv, gcol, grow, tb=256), tb=256)
    return gdn_post(o_f, o_b, proj, o_norm, w_o.astype(BF16), h, tm=512)


def _mla_lane_maps():
    lane = np.arange(128)
    rope1 = lane < 16
    rope2 = (lane >= 64) & (lane < 80)
    nope_a = (lane >= 16) & (lane < 64)
    nope_b = (lane >= 80) & (lane < 96)
    q_dim = np.where(rope1, MLA_NOPE + lane, np.where(nope_a, lane - 16, np.where(rope2, lane + 16, lane - 32)))
    q_ok = lane < 96
    nope_dim = np.where(nope_a, lane - 16, lane - 32)
    nope_ok = nope_a | nope_b
    rope_dim = np.where(rope1, lane, lane - 64 + 16)
    rope_ok = rope1 | rope2
    return q_dim, q_ok, nope_dim, nope_ok, rope_dim, rope_ok


def _mla_layer(h, norm, w_in, q_norm, w_uq, kv_norm, w_ukv, w_o):
    s = h.shape[0]
    hh = MLA_HEADS
    dqk = MLA_NOPE + MLA_ROPE
    q_dim, q_ok, nope_dim, nope_ok, rope_dim, rope_ok = _mla_lane_maps()
    head = np.arange(hh)[:, None]
    q_cols = (head * dqk + np.where(q_ok, q_dim, 0)[None, :]).reshape(-1)
    wuq_t = jnp.transpose(jnp.where(np.tile(q_ok, hh)[None, :], w_uq[:, q_cols], 0.0)).astype(BF16)
    k_cols = (head * (MLA_NOPE + MLA_V) + np.where(nope_ok, nope_dim, 0)[None, :]).reshape(-1)
    wuk_p = jnp.where(np.tile(nope_ok, hh)[None, :], w_ukv[:, k_cols], 0.0).astype(BF16)
    vx = np.arange(MLA_VX)[None, :]
    v_cols = (head * (MLA_NOPE + MLA_V) + MLA_NOPE + np.minimum(vx, MLA_V - 1)).reshape(-1)
    v_ok = np.broadcast_to(vx < MLA_V, (hh, MLA_VX)).reshape(-1)
    wuv_t = jnp.where(v_ok[:, None], jnp.transpose(w_ukv[:, v_cols]), 0.0).astype(BF16)
    vone = jnp.asarray(np.broadcast_to(vx == MLA_V, (hh, MLA_VX)).reshape(-1, 1), F32)
    n_c = MLA_Q_RANK + MLA_KV_RANK
    w_rope = jnp.where(rope_ok[None, :], w_in[:, n_c + np.where(rope_ok, rope_dim, 0)], 0.0)
    win_p = jnp.concatenate([w_in[:, :n_c], w_rope], axis=1).astype(BF16)
    half = MLA_ROPE // 2
    inv = 1.0 / (ROPE_THETA ** (jnp.arange(half, dtype=F32) / half))
    ang = jnp.arange(s, dtype=jnp.int32).astype(F32)[:, None] * inv[None, :]
    cos, sin = jnp.cos(ang), jnp.sin(ang)
    ones = jnp.ones((s, 48), F32)
    zeros = jnp.zeros((s, 48), F32)
    ct = jnp.concatenate([cos, ones, cos, ones], axis=1)
    sn = jnp.concatenate([-sin, zeros, sin, zeros], axis=1)
    q_t, k, v_t = mla_pre(h, norm, win_p, q_norm, kv_norm, wuq_t, wuk_p, wuv_t, vone, ct, sn, tm=512)
    o = mla_attention(q_t, k, v_t, tq=min(2048, s))
    return matmul_residual(o, w_o.astype(BF16), h, tm=512, name="mla_out")


def _ret_tables(reverse):
    c = RET_CHUNK
    lg = jnp.log1p(-jnp.exp2(-5.0 - jnp.arange(RET_HEADS, dtype=F32)))
    if reverse:
        lg = lg[::-1]
    idx = np.arange(c)
    diff = idx[:, None] - idx[None, :]
    if reverse:
        keep, expo = diff < 0, -diff
        q_pow, k_pow = c - idx, idx
    else:
        keep, expo = diff >= 0, diff
        q_pow, k_pow = idx + 1, c - 1 - idx
    dmask = jnp.where(keep, jnp.exp(jnp.where(keep, expo, 0).astype(F32)[None] * lg[:, None, None]), 0.0)
    q_dec = jnp.exp(jnp.asarray(q_pow, F32)[None, :] * lg[:, None])
    k_dec = jnp.exp(jnp.asarray(k_pow, F32)[None, :] * lg[:, None])
    g_chunk = jnp.exp(c * lg)
    bc = lambda t: jnp.broadcast_to(t[:, :, None], (RET_HEADS, c, RET_DK))
    return [dmask, bc(q_dec), bc(k_dec), jnp.broadcast_to(g_chunk[:, None, None], (RET_HEADS, 1, RET_DV))]


def _ret_layer(h, norm, w_in, gn_w, w_o):
    s = h.shape[0]
    proj = norm_matmul(h, norm, w_in.astype(BF16), F32, tm=256, tn=w_in.shape[1], name="ret_in")
    half = RET_DK // 2
    inv = 1.0 / (ROPE_THETA ** (jnp.arange(half, dtype=F32) / half))
    ang = jnp.arange(s, dtype=jnp.int32).astype(F32)[:, None] * inv[None, :]
    cos, sin = jnp.cos(ang), jnp.sin(ang)
    ct = jnp.concatenate([cos, cos], axis=1)
    sn = jnp.concatenate([-sin, sin], axis=1)
    o_f, o_b = ret_scan(proj, ct, sn, _ret_tables(False), _ret_tables(True))
    return ret_post(o_f, o_b, proj, gn_w, w_o.astype(BF16), h, tm=256)


def kernel(x, na_norm, na_w_qkv, na_rpb, na_w_o, gdn_norm, gdn_w_in, gdn_conv, gdn_a_log_f, gdn_a_log_b, gdn_dt_bias_f, gdn_dt_bias_b, gdn_o_norm, gdn_w_o, mla_norm, mla_w_in, mla_q_norm, mla_w_uq, mla_kv_norm, mla_w_ukv, mla_w_o, ret_norm, ret_w_in, ret_gn, ret_w_o, mlp_norm, mlp_w1, mlp_w2, final_norm):
    b, s, d = x.shape
    depth = mlp_norm.shape[0]
    outs = []
    for bi in range(b):
        h = x[bi].astype(F32)
        for i in range(depth):
            m, j = i % 4, i // 4
            if m == 0:
                h = _na_layer(h, na_norm[j], na_w_qkv[j], na_rpb[j], na_w_o[j])
            elif m == 1:
                h = _gdn_layer(h, gdn_norm[j], gdn_w_in[j], gdn_conv[j], gdn_a_log_f[j], gdn_a_log_b[j],
                               gdn_dt_bias_f[j], gdn_dt_bias_b[j], gdn_o_norm[j], gdn_w_o[j])
            elif m == 2:
                h = _mla_layer(h, mla_norm[j], mla_w_in[j], mla_q_norm[j], mla_w_uq[j], mla_kv_norm[j],
                               mla_w_ukv[j], mla_w_o[j])
            else:
                h = _ret_layer(h, ret_norm[j], ret_w_in[j], ret_gn[j], ret_w_o[j])
            h = mlp_block(h, mlp_norm[i], mlp_w1[i].astype(BF16), mlp_w2[i].astype(BF16), final_norm,
                          final=(i == depth - 1), tm=512, tf=mlp_w1.shape[2], name=f"mlp_{i}")
        outs.append(h)
    return jnp.stack(outs).astype(x.dtype)
```

```python
import functools
import math

import numpy as np
import jax
import jax.numpy as jnp
from jax import lax
from jax.experimental import pallas as pl
from jax.experimental.pallas import tpu as pltpu

F32 = jnp.float32
BF16 = jnp.bfloat16
EPS = 1e-6
ROPE_THETA = 10000.0
GRID_W = 64
NEG_BIG = -1e30
HI = lax.Precision.HIGHEST

NA_HEADS, NA_DH, NA_WIN_ROWS, NA_WIN_COLS, NA_Q_ROWS = 16, 64, 8, 16, 2
NA_KBLK = 5
GDN_HEADS, GDN_DK, GDN_CONV, GDN_CHUNK = 8, 128, 5, 64
GDN_GATE_LANES = 32
MLA_HEADS, MLA_Q_RANK, MLA_KV_RANK, MLA_NOPE, MLA_ROPE, MLA_V = 16, 768, 256, 64, 32, 64
MLA_VX = 80
RET_HEADS, RET_DK, RET_DV, RET_CHUNK = 8, 128, 256, 128

VMEM_LIMIT = 52 * 1024 * 1024


def _cparams(*sem):
    return pltpu.CompilerParams(dimension_semantics=sem, vmem_limit_bytes=VMEM_LIMIT)


def _rms(x, g):
    return x * lax.rsqrt(jnp.mean(x * x, axis=-1, keepdims=True) + EPS) * g


def _sigmoid(x):
    return 1.0 / (1.0 + jnp.exp(-x))


def _dot(a, b, **kw):
    return jnp.dot(a, b, preferred_element_type=F32, **kw)


def _dot_nt(a, b, **kw):
    return lax.dot_general(a, b, (((1,), (1,)), ((), ())), preferred_element_type=F32, **kw)


def _dot_tn(a, b, **kw):
    return lax.dot_general(a, b, (((0,), (0,)), ((), ())), preferred_element_type=F32, **kw)


def _norm_matmul_body(x_ref, g_ref, w_ref, o_ref, xn_ref):
    @pl.when(pl.program_id(1) == 0)
    def _():
        xn_ref[...] = _rms(x_ref[...], g_ref[...]).astype(BF16)

    o_ref[...] = _dot(xn_ref[...], w_ref[...]).astype(o_ref.dtype)


def norm_matmul(x, g, w, out_dtype, tm, tn, name):
    s, d = x.shape
    n = w.shape[1]
    return pl.pallas_call(
        _norm_matmul_body,
        grid=(s // tm, n // tn),
        in_specs=[pl.BlockSpec((tm, d), lambda i, j: (i, 0)),
                  pl.BlockSpec((1, d), lambda i, j: (0, 0)),
                  pl.BlockSpec((d, tn), lambda i, j: (0, j))],
        out_specs=pl.BlockSpec((tm, tn), lambda i, j: (i, j)),
        out_shape=jax.ShapeDtypeStruct((s, n), out_dtype),
        scratch_shapes=[pltpu.VMEM((tm, d), BF16)],
        compiler_params=_cparams("arbitrary", "arbitrary"),
        name=name,
    )(x, g.reshape(1, d), w)


def _matmul_res_body(a_ref, w_ref, r_ref, o_ref):
    o_ref[...] = r_ref[...] + _dot(a_ref[...], w_ref[...])


def matmul_residual(a, w, res, tm, name):
    s, k = a.shape
    n = w.shape[1]
    return pl.pallas_call(
        _matmul_res_body,
        grid=(s // tm,),
        in_specs=[pl.BlockSpec((tm, k), lambda i: (i, 0)),
                  pl.BlockSpec((k, n), lambda i: (0, 0)),
                  pl.BlockSpec((tm, n), lambda i: (i, 0))],
        out_specs=pl.BlockSpec((tm, n), lambda i: (i, 0)),
        out_shape=jax.ShapeDtypeStruct((s, n), F32),
        compiler_params=_cparams("arbitrary"),
        name=name,
    )(a, w, res)


def _mlp_body(x_ref, g_ref, w1_ref, w2_ref, fg_ref, o_ref, xn_ref, acc_ref, *, nk, final):
    k = pl.program_id(1)

    @pl.when(k == 0)
    def _():
        xn_ref[...] = _rms(x_ref[...], g_ref[...]).astype(BF16)
        acc_ref[...] = jnp.zeros_like(acc_ref)

    a = jnp.maximum(_dot(xn_ref[...], w1_ref[...]), 0.0)
    acc_ref[...] += _dot((a * a).astype(BF16), w2_ref[...])

    @pl.when(k == nk - 1)
    def _():
        y = x_ref[...] + acc_ref[...]
        if final:
            y = _rms(y, fg_ref[...])
        o_ref[...] = y


def mlp_block(x, g, w1, w2, final_g, final, tm, tf, name):
    s, d = x.shape
    f = w1.shape[1]
    nk = f // tf
    return pl.pallas_call(
        functools.partial(_mlp_body, nk=nk, final=final),
        grid=(s // tm, nk),
        in_specs=[pl.BlockSpec((tm, d), lambda i, k: (i, 0)),
                  pl.BlockSpec((1, d), lambda i, k: (0, 0)),
                  pl.BlockSpec((d, tf), lambda i, k: (0, k)),
                  pl.BlockSpec((tf, d), lambda i, k: (k, 0)),
                  pl.BlockSpec((1, d), lambda i, k: (0, 0))],
        out_specs=pl.BlockSpec((tm, d), lambda i, k: (i, 0)),
        out_shape=jax.ShapeDtypeStruct((s, d), F32),
        scratch_shapes=[pltpu.VMEM((tm, d), BF16), pltpu.VMEM((tm, d), F32)],
        compiler_params=_cparams("arbitrary", "arbitrary"),
        name=name,
    )(x, g.reshape(1, d), w1, w2, final_g.reshape(1, d))


def _na_bias_table(rpb, rows):
    n_qblk = rows // NA_Q_ROWS
    reps = {4: 0, 3: 1, 2: 2, 1: n_qblk - 2, 0: n_qblk - 1}
    w, wc, wr = GRID_W, NA_WIN_COLS, NA_WIN_ROWS
    pcol = jnp.pad(rpb.astype(F32), ((0, 0), (0, 0), (w - wc, w - wc)), mode="edge")
    e = jnp.stack([pcol[:, :, w - 1 - qc:2 * w - 1 - qc] for qc in range(w)], axis=2)
    qc = np.arange(w)
    qc0 = np.clip(qc - wc // 2, 0, w - wc)
    col_ok = (qc[None, :] >= qc0[:, None]) & (qc[None, :] < qc0[:, None] + wc)
    tabs, oks = [], []
    for t in range(5):
        i = reps[t]
        base = int(np.clip(i - 2, 0, n_qblk - NA_KBLK))
        per_qr, ok_qr = [], []
        for qr in range(NA_Q_ROWS):
            qrow = NA_Q_ROWS * i + qr
            qr0 = int(np.clip(qrow - wr // 2, 0, rows - wr))
            krows = NA_Q_ROWS * base + np.arange(NA_KBLK * NA_Q_ROWS)
            drow = np.clip(krows - qrow + wr - 1, 0, 2 * wr - 2)
            row_ok = (krows >= qr0) & (krows < qr0 + wr)
            per_qr.append(jnp.concatenate([e[:, int(d)] for d in drow], axis=-1))
            ok_qr.append(np.concatenate([col_ok & bool(r) for r in row_ok], axis=-1))
        tabs.append(jnp.concatenate(per_qr, axis=1))
        oks.append(np.concatenate(ok_qr, axis=0))
    tab = jnp.where(np.stack(oks)[:, None], jnp.stack(tabs), NEG_BIG)
    return jnp.swapaxes(tab, 2, 3)


def _na_body(q_ref, k0, k1, k2, k3, k4, v0, v1, v2, v3, v4, b_ref, o_ref):
    krefs = (k0, k1, k2, k3, k4)
    vrefs = (v0, v1, v2, v3, v4)
    nkb = NA_Q_ROWS * GRID_W
    lane_k = lax.broadcasted_iota(jnp.int32, (NA_KBLK * nkb, 128), 1)
    ones_blk = jnp.where(lane_k == 0, 1.0, 0.0).astype(BF16)
    first = lax.broadcasted_iota(jnp.int32, (nkb, 128), 1) < NA_DH
    for pr in range(NA_HEADS // 2):
        sl = slice(pr * 128, (pr + 1) * 128)
        k_all = jnp.concatenate([kr[:, sl] for kr in krefs], axis=0)
        v_aug = jnp.concatenate([jnp.concatenate([vr[:, sl] for vr in vrefs], axis=0), ones_blk], axis=1)
        q2 = q_ref[:, sl]
        outs = []
        for hh in range(2):
            qh = jnp.where(first if hh == 0 else jnp.logical_not(first), q2, jnp.zeros_like(q2))
            s_t = _dot_nt(k_all, qh) + b_ref[0, 2 * pr + hh]
            m = jnp.max(s_t, axis=0, keepdims=True)
            p_t = jnp.exp(s_t - m).astype(BF16)
            o = _dot_tn(p_t, v_aug)
            outs.append(o[:, :128] / o[:, 128:129])
        o_ref[:, sl] = jnp.where(first, outs[0], outs[1]).astype(o_ref.dtype)


def na_attention(qkv, bias_tab):
    s = qkv.shape[0]
    d = NA_HEADS * NA_DH
    tq = NA_Q_ROWS * GRID_W
    n_qblk = s // tq
    top = n_qblk - NA_KBLK

    def base(i):
        return jnp.clip(i - 2, 0, top)

    kv_specs = [pl.BlockSpec((tq, d), functools.partial(lambda i, dd, c: (base(i) + dd, c), dd=dd, c=c))
                for c in (1, 2) for dd in range(NA_KBLK)]
    return pl.pallas_call(
        _na_body,
        grid=(n_qblk,),
        in_specs=[pl.BlockSpec((tq, d), lambda i: (i, 0))] + kv_specs
                 + [pl.BlockSpec((1, NA_HEADS, NA_KBLK * tq, tq), lambda i: (base(i) - i + 4, 0, 0, 0))],
        out_specs=pl.BlockSpec((tq, d), lambda i: (i, 0)),
        out_shape=jax.ShapeDtypeStruct((s, d), BF16),
        compiler_params=_cparams("arbitrary"),
        name="na_attention",
    )(qkv, *([qkv] * (2 * NA_KBLK)), bias_tab)


def _gdn_conv_body(x_ref, xp_ref, xn_ref, w_ref, o_ref, xe_ref, *, tm, nt):
    i = pl.program_id(0)
    j = pl.program_id(1)
    xe_ref[0:8, :] = jnp.where(i > 0, xp_ref[...], 0.0)
    xe_ref[8:8 + tm, :] = x_ref[...]
    xe_ref[8 + tm:16 + tm, :] = jnp.where(i < nt - 1, xn_ref[...], 0.0)
    w = w_ref[...]
    half = GDN_CONV // 2
    acc = xe_ref[8 - half:8 - half + tm, :] * w[0:1]
    for t in range(1, GDN_CONV):
        acc = acc + xe_ref[8 - half + t:8 - half + t + tm, :] * w[t:t + 1]
    y = acc * _sigmoid(acc)
    for h in range(GDN_HEADS):
        sl = slice(h * GDN_DK, (h + 1) * GDN_DK)
        yh = y[:, sl]
        nrm = lax.rsqrt(jnp.sum(yh * yh, axis=1, keepdims=True) + EPS)
        fac = jnp.where(j == 0, nrm * GDN_DK ** -0.5, jnp.where(j == 1, nrm, 1.0))
        o_ref[:, sl] = (yh * fac).astype(o_ref.dtype)


def gdn_conv(proj, conv_w_t, tm):
    s = proj.shape[0]
    d = GDN_HEADS * GDN_DK
    nt = s // tm
    r8 = tm // 8
    return pl.pallas_call(
        functools.partial(_gdn_conv_body, tm=tm, nt=nt),
        grid=(nt, 3),
        in_specs=[pl.BlockSpec((tm, d), lambda i, j: (i, j)),
                  pl.BlockSpec((8, d), lambda i, j: (jnp.maximum(i * r8 - 1, 0), j)),
                  pl.BlockSpec((8, d), lambda i, j: (jnp.minimum((i + 1) * r8, nt * r8 - 1), j)),
                  pl.BlockSpec((GDN_CONV, d), lambda i, j: (0, j))],
        out_specs=pl.BlockSpec((tm, d), lambda i, j: (i, j)),
        out_shape=jax.ShapeDtypeStruct((s, 3 * d), BF16),
        scratch_shapes=[pltpu.VMEM((tm + 16, d), F32)],
        compiler_params=_cparams("arbitrary", "arbitrary"),
        name="gdn_conv",
    )(proj, proj, proj, conv_w_t)


def _gdn_gates_body(gp_ref, alog_ref, dtb_ref, col_ref, row_ref, *, tm):
    c = GDN_CHUNK
    gp = gp_ref[...]
    lane = lax.broadcasted_iota(jnp.int32, (tm, 128), 1)
    is_f = (lane >= 16) & (lane < 24)
    is_b = (lane >= 24) & (lane < 32)
    beta = _sigmoid(gp)
    z = gp + dtb_ref[...]
    softplus = jnp.maximum(z, 0.0) + jnp.log1p(jnp.exp(-jnp.abs(z)))
    g = jnp.where(is_f | is_b, -jnp.exp(alog_ref[...]) * softplus, 0.0)
    r = lax.broadcasted_iota(jnp.int32, (c, c), 0)
    cc = lax.broadcasted_iota(jnp.int32, (c, c), 1)
    tri_lo = (r >= cc).astype(F32)
    tri_up = (r <= cc).astype(F32)
    lane_c = lax.broadcasted_iota(jnp.int32, (c, 128), 1)
    is_f_c = (lane_c >= 16) & (lane_c < 24)
    is_g_c = (lane_c >= 16) & (lane_c < 32)
    for ch in range(tm // c):
        gc = g[ch * c:(ch + 1) * c]
        cum = jnp.where(is_f_c, _dot(tri_lo, gc, precision=HI), _dot(tri_up, gc, precision=HI))
        colc = jnp.where(is_g_c, cum, beta[ch * c:(ch + 1) * c])
        col_ref[ch * c:(ch + 1) * c, :] = colc
        row_ref[ch] = colc.T[:GDN_GATE_LANES, :]


def gdn_gates(gp, alog_vec, dtb_vec, tm):
    s = gp.shape[0]
    c = GDN_CHUNK
    return pl.pallas_call(
        functools.partial(_gdn_gates_body, tm=tm),
        grid=(s // tm,),
        in_specs=[pl.BlockSpec((tm, 128), lambda i: (i, 0)),
                  pl.BlockSpec((1, 128), lambda i: (0, 0)),
                  pl.BlockSpec((1, 128), lambda i: (0, 0))],
        out_specs=[pl.BlockSpec((tm, 128), lambda i: (i, 0)),
                   pl.BlockSpec((tm // c, GDN_GATE_LANES, c), lambda i: (i, 0, 0))],
        out_shape=[jax.ShapeDtypeStruct((s, 128), F32), jax.ShapeDtypeStruct((s // c, GDN_GATE_LANES, c), F32)],
        compiler_params=_cparams("arbitrary"),
        name="gdn_gates",
    )(gp, alog_vec, dtb_vec)


def _bmm(a, b):
    return jnp.einsum("nik,nkj->nij", a, b, preferred_element_type=F32)


def _bmm_nt(a, b):
    return jnp.einsum("nik,njk->nij", a, b, preferred_element_type=F32)


def _unit_tri_inverse(lmat, r, cc):
    eye = (r == cc).astype(F32)
    diag_blk = (r // 16) == (cc // 16)
    ld = jnp.where(diag_blk, lmat, 0.0)
    lo = lmat - ld
    x = eye - ld
    p = _bmm(ld, ld)
    x = x + _bmm(x, p)
    p = _bmm(p, p)
    x = x + _bmm(x, p)
    p = _bmm(p, p)
    dinv = x + _bmm(x, p)
    n1 = _bmm(dinv, lo)
    n2 = _bmm(n1, n1)
    y = eye - n1 + n2 - _bmm(n1, n2)
    return _bmm(y, dinv)


def _gdn_local_body(qkv_ref, col_ref, row_ref, *out_refs, tb):
    c, dk, hh = GDN_CHUNK, GDN_DK, GDN_HEADS
    nc = tb // c
    nb = hh * nc
    r = lax.broadcasted_iota(jnp.int32, (nb, c, c), 1)
    cc = lax.broadcasted_iota(jnp.int32, (nb, c, c), 2)
    col3 = col_ref[...].reshape(nc, c, 128)

    def heads(base):
        return jnp.concatenate([qkv_ref[:, base + h * dk:base + (h + 1) * dk].astype(F32).reshape(nc, c, dk)
                                for h in range(hh)], axis=0)

    qc, kc, vc = heads(0), heads(hh * dk), heads(2 * hh * dk)
    kc16 = kc.astype(BF16)
    qk = _bmm_nt(qc.astype(BF16), kc16)
    for d in range(2):
        reverse = d == 1
        u_ref, wq_ref, ak_ref, gl_ref = out_refs[4 * d:4 * d + 4]
        incl = (r <= cc) if reverse else (r >= cc)
        strict = (r < cc) if reverse else (r > cc)
        b0 = 8 if reverse else 0
        g0 = 24 if reverse else 16
        beta = jnp.concatenate([col3[:, :, b0 + h:b0 + h + 1] for h in range(hh)], axis=0)
        gcol = jnp.concatenate([col3[:, :, g0 + h:g0 + h + 1] for h in range(hh)], axis=0)
        grow = jnp.concatenate([row_ref[:, g0 + h:g0 + h + 1, :] for h in range(hh)], axis=0)
        g_last = gcol[:, 0:1, :] if reverse else gcol[:, c - 1:c, :]
        decay = jnp.where(incl, jnp.exp(jnp.where(incl, gcol - grow, 0.0)), 0.0)
        eg = jnp.exp(gcol)
        kb = kc * beta
        lmat = jnp.where(strict, _bmm_nt(kb.astype(BF16), kc16) * decay, 0.0)
        tinv = _unit_tri_inverse(lmat, r, cc)
        sol = _bmm(tinv, jnp.concatenate([vc * beta, kb * eg], axis=2))
        a_intra = jnp.where(incl, qk * decay, 0.0).astype(BF16)
        w16 = sol[:, :, dk:].astype(BF16)
        qd16 = (qc * eg).astype(BF16)
        kdt16 = jnp.swapaxes(kc * jnp.exp(g_last - gcol), 1, 2).astype(BF16)
        gl = jnp.broadcast_to(jnp.exp(g_last), (nb, 1, dk))
        for h in range(hh):
            sl = slice(h * dk, (h + 1) * dk)
            hb = slice(h * nc, (h + 1) * nc)
            u_ref[:, sl] = sol[hb, :, :dk].reshape(tb, dk)
            wq_ref[:, 0:c, sl] = w16[hb]
            wq_ref[:, c:2 * c, sl] = qd16[hb]
            ak_ref[:, 0:c, h * c:(h + 1) * c] = a_intra[hb]
            ak_ref[:, c:c + dk, h * c:(h + 1) * c] = kdt16[hb]
            gl_ref[:, h:h + 1, :] = gl[hb]


def gdn_local(qkv, gcol, grow, tb):
    s = qkv.shape[0]
    c, d = GDN_CHUNK, GDN_HEADS * GDN_DK
    nc = tb // c
    row = lambda w: pl.BlockSpec((tb, w), lambda i: (i, 0))
    per_dir_specs = [row(d),
                     pl.BlockSpec((nc, 2 * c, d), lambda i: (i, 0, 0)),
                     pl.BlockSpec((nc, c + GDN_DK, GDN_HEADS * c), lambda i: (i, 0, 0)),
                     pl.BlockSpec((nc, GDN_HEADS, GDN_DK), lambda i: (i, 0, 0))]
    per_dir_shapes = [jax.ShapeDtypeStruct((s, d), F32),
                      jax.ShapeDtypeStruct((s // c, 2 * c, d), BF16),
                      jax.ShapeDtypeStruct((s // c, c + GDN_DK, GDN_HEADS * c), BF16),
                      jax.ShapeDtypeStruct((s // c, GDN_HEADS, GDN_DK), F32)]
    return pl.pallas_call(
        functools.partial(_gdn_local_body, tb=tb),
        grid=(s // tb,),
        in_specs=[row(3 * d), row(128), pl.BlockSpec((nc, GDN_GATE_LANES, c), lambda i: (i, 0, 0))],
        out_specs=per_dir_specs * 2,
        out_shape=per_dir_shapes * 2,
        compiler_params=_cparams("arbitrary"),
        name="gdn_local",
    )(qkv, gcol, grow)


def _gdn_scan_body(*refs, tb):
    c, dk, hh = GDN_CHUNK, GDN_DK, GDN_HEADS
    nc = tb // c
    ins, (of_ref, ob_ref, st_ref) = refs[:8], refs[8:]

    @pl.when(pl.program_id(0) == 0)
    def _():
        st_ref[...] = jnp.zeros_like(st_ref)

    for step in range(nc):
        for d in range(2):
            u_ref, wq_ref, ak_ref, gl_ref = ins[4 * d:4 * d + 4]
            o_ref = (of_ref, ob_ref)[d]
            ch = nc - 1 - step if d == 1 else step
            rows = slice(ch * c, (ch + 1) * c)
            for h in range(hh):
                sl = slice(h * dk, (h + 1) * dk)
                state = st_ref[d, h]
                ws_qs = _dot(wq_ref[ch, :, sl], state.astype(BF16))
                v_new = u_ref[rows, sl] - ws_qs[:c]
                av_kv = _dot(ak_ref[ch, :, h * c:(h + 1) * c], v_new.astype(BF16))
                o_ref[rows, sl] = ws_qs[c:] + av_kv[:c]
                st_ref[d, h] = state * gl_ref[ch, h:h + 1, :] + av_kv[c:]


def gdn_scan(local_out, tb):
    s, d = local_out[0].shape
    c = GDN_CHUNK
    nb = s // tb
    nc = tb // c

    def dir_specs(blk):
        return [pl.BlockSpec((tb, d), lambda t: (blk(t), 0)),
                pl.BlockSpec((nc, 2 * c, d), lambda t: (blk(t), 0, 0)),
                pl.BlockSpec((nc, c + GDN_DK, GDN_HEADS * c), lambda t: (blk(t), 0, 0)),
                pl.BlockSpec((nc, GDN_HEADS, GDN_DK), lambda t: (blk(t), 0, 0))]

    fwd = lambda t: t
    bwd = lambda t: nb - 1 - t
    return pl.pallas_call(
        functools.partial(_gdn_scan_body, tb=tb),
        grid=(nb,),
        in_specs=dir_specs(fwd) + dir_specs(bwd),
        out_specs=[pl.BlockSpec((tb, d), lambda t: (t, 0)), pl.BlockSpec((tb, d), lambda t: (nb - 1 - t, 0))],
        out_shape=[jax.ShapeDtypeStruct((s, d), F32)] * 2,
        scratch_shapes=[pltpu.VMEM((2, GDN_HEADS, GDN_DK, GDN_DK), F32)],
        compiler_params=_cparams("arbitrary"),
        name="gdn_scan",
    )(*local_out)


def _gdn_post_body(of_ref, ob_ref, z_ref, on_ref, w_ref, r_ref, o_ref):
    on = on_ref[...]
    parts = []
    for h in range(GDN_HEADS):
        sl = slice(h * GDN_DK, (h + 1) * GDN_DK)
        o = of_ref[:, sl] + ob_ref[:, sl]
        z = z_ref[:, sl]
        parts.append((_rms(o, on) * (z * _sigmoid(z))).astype(BF16))
    y = jnp.concatenate(parts, axis=1)
    o_ref[...] = r_ref[...] + _dot(y, w_ref[...])


def gdn_post(o_f, o_b, proj, o_norm, w_o, res, tm):
    s, d = o_f.shape
    return pl.pallas_call(
        _gdn_post_body,
        grid=(s // tm,),
        in_specs=[pl.BlockSpec((tm, d), lambda i: (i, 0)),
                  pl.BlockSpec((tm, d), lambda i: (i, 0)),
                  pl.BlockSpec((tm, d), lambda i: (i, 3)),
                  pl.BlockSpec((1, GDN_DK), lambda i: (0, 0)),
                  pl.BlockSpec((d, d), lambda i: (0, 0)),
                  pl.BlockSpec((tm, d), lambda i: (i, 0))],
        out_specs=pl.BlockSpec((tm, d), lambda i: (i, 0)),
        out_shape=jax.ShapeDtypeStruct((s, d), F32),
        compiler_params=_cparams("arbitrary"),
        name="gdn_post",
    )(o_f, o_b, proj, o_norm.reshape(1, GDN_DK), w_o, res)


def _mla_pre_body(x_ref, g_ref, win_ref, qn_ref, kvn_ref, wuq_ref, wuk_ref, wuv_ref, vone_ref, ct_ref, sn_ref,
                  ctt_ref, snt_ref, q_ref, k_ref, v_ref):
    xn = _rms(x_ref[...], g_ref[...]).astype(BF16)
    proj = _dot(xn, win_ref[...])
    cq = _rms(proj[:, :MLA_Q_RANK], qn_ref[...]).astype(BF16)
    ckv = _rms(proj[:, MLA_Q_RANK:MLA_Q_RANK + MLA_KV_RANK], kvn_ref[...]).astype(BF16)
    pr = proj[:, MLA_Q_RANK + MLA_KV_RANK:]
    ct, sn = ct_ref[...], sn_ref[...]
    scale = (MLA_NOPE + MLA_ROPE) ** -0.5 * math.log2(math.e)
    ctt, snt = ctt_ref[...], snt_ref[...]
    q_t = _dot_nt(wuq_ref[...], cq)
    for h in range(MLA_HEADS):
        qh = q_t[h * 128:(h + 1) * 128, :]
        partner = jnp.concatenate([qh[64:, :], qh[:64, :]], axis=0)
        q_ref[h * 128:(h + 1) * 128, :] = ((qh * ctt + partner * snt) * scale).astype(BF16)
    kr = pr * ct + pltpu.roll(pr, 64, 1) * sn
    k = _dot(ckv, wuk_ref[...])
    for h in range(MLA_HEADS):
        sl = slice(h * 128, (h + 1) * 128)
        k_ref[:, sl] = (k[:, sl] + kr).astype(BF16)
    v_t = (_dot_nt(wuv_ref[...], ckv) + vone_ref[...]).astype(BF16)
    v_ref[...] = v_t.reshape(v_ref.shape)


def mla_pre(x, g, win_p, q_norm, kv_norm, wuq_t, wuk_p, wuv_t, vone, ct, sn, tm):
    s, d = x.shape
    hp = MLA_HEADS * 128
    npair = MLA_HEADS // 2
    full = lambda a: pl.BlockSpec(a.shape, lambda i: (0,) * a.ndim)
    g2, qn2, kvn2 = g.reshape(1, d), q_norm.reshape(1, -1), kv_norm.reshape(1, -1)
    ct_t, sn_t = jnp.transpose(ct), jnp.transpose(sn)
    return pl.pallas_call(
        _mla_pre_body,
        grid=(s // tm,),
        in_specs=[pl.BlockSpec((tm, d), lambda i: (i, 0)), full(g2), full(win_p), full(qn2), full(kvn2),
                  full(wuq_t), full(wuk_p), full(wuv_t), full(vone),
                  pl.BlockSpec((tm, 128), lambda i: (i, 0)), pl.BlockSpec((tm, 128), lambda i: (i, 0)),
                  pl.BlockSpec((128, tm), lambda i: (0, i)), pl.BlockSpec((128, tm), lambda i: (0, i))],
        out_specs=[pl.BlockSpec((hp, tm), lambda i: (0, i)), pl.BlockSpec((tm, hp), lambda i: (i, 0)),
                   pl.BlockSpec((npair, 1, 2 * MLA_VX, tm), lambda i: (0, i, 0, 0))],
        out_shape=[jax.ShapeDtypeStruct((hp, s), BF16), jax.ShapeDtypeStruct((s, hp), BF16),
                   jax.ShapeDtypeStruct((npair, s // tm, 2 * MLA_VX, tm), BF16)],
        compiler_params=_cparams("arbitrary"),
        name="mla_pre",
    )(x, g2, win_p, qn2, kvn2, wuq_t, wuk_p, wuv_t, vone, ct, sn, ct_t, sn_t)


def _mla_attn_body(qt_ref, k_ref, vt_ref, o_ref, acc_ref, s_ref, *, tq, tk, nk):
    acc_ref[...] = jnp.zeros_like(acc_ref)

    def scores(j, slot):
        off = pl.multiple_of(j * tk, tk)
        for hh in range(2):
            sl = slice(hh * 128, (hh + 1) * 128)
            s_ref[slot, hh] = _dot(k_ref[pl.ds(off, tk), sl], qt_ref[sl, :])

    def consume(j, slot, carry):
        new = []
        for hh in range(2):
            m_old = carry[hh]
            rows = slice(hh * MLA_VX, (hh + 1) * MLA_VX)
            s = s_ref[slot, hh]
            m_new = jnp.maximum(m_old, jnp.max(s, axis=0, keepdims=True))
            p = jnp.exp2(s - m_new).astype(BF16)
            acc_ref[rows, :] = acc_ref[rows, :] * jnp.exp2(m_old - m_new) + _dot(vt_ref[0, j, rows, :], p)
            new.append(m_new)
        return tuple(new)

    scores(0, 0)

    def body(jj, carry):
        j = 2 * jj
        scores(j + 1, 1)
        carry = consume(j, 0, carry)
        scores(jnp.minimum(j + 2, nk - 1), 0)
        return consume(j + 1, 1, carry)

    lax.fori_loop(0, nk // 2, body, (jnp.full((1, tq), -jnp.inf, F32),) * 2)
    o_t = jnp.concatenate([acc_ref[hh * MLA_VX:hh * MLA_VX + MLA_V, :]
                           / acc_ref[hh * MLA_VX + MLA_V:hh * MLA_VX + MLA_V + 1, :] for hh in range(2)], axis=0)
    o_ref[...] = o_t.T.astype(o_ref.dtype)


def mla_attention(q_t, k, v_t, tq):
    s = k.shape[0]
    npair, nk, _, tk = v_t.shape
    return pl.pallas_call(
        functools.partial(_mla_attn_body, tq=tq, tk=tk, nk=nk),
        grid=(npair, s // tq),
        in_specs=[pl.BlockSpec((256, tq), lambda p, i: (p, i)),
                  pl.BlockSpec((s, 256), lambda p, i: (0, p)),
                  pl.BlockSpec((1, nk, 2 * MLA_VX, tk), lambda p, i: (p, 0, 0, 0))],
        out_specs=pl.BlockSpec((tq, 128), lambda p, i: (i, p)),
        out_shape=jax.ShapeDtypeStruct((s, MLA_HEADS * MLA_V), BF16),
        scratch_shapes=[pltpu.VMEM((2 * MLA_VX, tq), F32), pltpu.VMEM((2, 2, tk, tq), F32)],
        compiler_params=_cparams("arbitrary", "arbitrary"),
        name="mla_attention",
    )(q_t, k, v_t)


def _ret_direction(q_ref, k_ref, v_ref, ct_ref, sn_ref, dm_ref, qd_ref, kd_ref, gc_ref, o_ref, st_ref):
    ct, sn = ct_ref[...], sn_ref[...]
    for h in range(RET_HEADS):
        sk = slice(h * RET_DK, (h + 1) * RET_DK)
        sv = slice(h * RET_DV, (h + 1) * RET_DV)
        qh = q_ref[:, sk]
        kh = k_ref[:, sk]
        qr = qh * ct + pltpu.roll(qh, 64, 1) * sn
        kr = (kh * ct + pltpu.roll(kh, 64, 1) * sn) * RET_DK ** -0.5
        v16 = v_ref[:, sv].astype(BF16)
        state = st_ref[h]
        inner = _dot_nt(qr.astype(BF16), kr.astype(BF16)) * dm_ref[h]
        o_ref[:, sv] = (_dot(inner.astype(BF16), v16)
                        + _dot((qr * qd_ref[h]).astype(BF16), state.astype(BF16)))
        st_ref[h] = state * gc_ref[h] + _dot_tn((kr * kd_ref[h]).astype(BF16), v16)


def _ret_scan_body(qf, kf, vf, ctf, snf, qb, kb, vb, ctb, snb,
                   dmf, qdf, kdf, gcf, dmb, qdb, kdb, gcb, of_ref, ob_ref, sf_ref, sb_ref):
    @pl.when(pl.program_id(0) == 0)
    def _():
        sf_ref[...] = jnp.zeros_like(sf_ref)
        sb_ref[...] = jnp.zeros_like(sb_ref)

    _ret_direction(qf, kf, vf, ctf, snf, dmf, qdf, kdf, gcf, of_ref, sf_ref)
    _ret_direction(qb, kb, vb, ctb, snb, dmb, qdb, kdb, gcb, ob_ref, sb_ref)


def ret_scan(proj, ct, sn, tabs_f, tabs_b):
    s = proj.shape[0]
    c = RET_CHUNK
    nb = s // c
    dq = RET_HEADS * RET_DK
    dv = RET_HEADS * RET_DV

    def dir_specs(blk):
        return [pl.BlockSpec((c, dq), lambda t: (blk(t), 0)),
                pl.BlockSpec((c, dq), lambda t: (blk(t), 1)),
                pl.BlockSpec((c, dv), lambda t: (blk(t), 1)),
                pl.BlockSpec((c, 128), lambda t: (blk(t), 0)),
                pl.BlockSpec((c, 128), lambda t: (blk(t), 0))]

    full = lambda a: pl.BlockSpec(a.shape, lambda t: (0,) * a.ndim)
    fwd = lambda t: t
    bwd = lambda t: nb - 1 - t
    return pl.pallas_call(
        _ret_scan_body,
        grid=(nb,),
        in_specs=dir_specs(fwd) + dir_specs(bwd) + [full(a) for a in tabs_f] + [full(a) for a in tabs_b],
        out_specs=[pl.BlockSpec((c, dv), lambda t: (t, 0)), pl.BlockSpec((c, dv), lambda t: (nb - 1 - t, 0))],
        out_shape=[jax.ShapeDtypeStruct((s, dv), F32)] * 2,
        scratch_shapes=[pltpu.VMEM((RET_HEADS, RET_DK, RET_DV), F32)] * 2,
        compiler_params=_cparams("arbitrary"),
        name="ret_scan",
    )(proj, proj, proj, ct, sn, proj, proj, proj, ct, sn, *tabs_f, *tabs_b)


def _ret_post_body(of_ref, ob_ref, gate_ref, gn_ref, w_ref, r_ref, o_ref):
    parts = []
    for h in range(RET_HEADS):
        sv = slice(h * RET_DV, (h + 1) * RET_DV)
        o = of_ref[:, sv] + ob_ref[:, sv]
        mu = jnp.mean(o, axis=1, keepdims=True)
        oc = o - mu
        var = jnp.mean(oc * oc, axis=1, keepdims=True)
        gate = gate_ref[:, sv]
        parts.append((gate * _sigmoid(gate) * (oc * lax.rsqrt(var + EPS) * gn_ref[:, sv])).astype(BF16))
    y = jnp.concatenate(parts, axis=1)
    o_ref[...] = r_ref[...] + _dot(y, w_ref[...])


def ret_post(o_f, o_b, proj, gn_w, w_o, res, tm):
    s, dv = o_f.shape
    d = res.shape[1]
    return pl.pallas_call(
        _ret_post_body,
        grid=(s // tm,),
        in_specs=[pl.BlockSpec((tm, dv), lambda i: (i, 0)),
                  pl.BlockSpec((tm, dv), lambda i: (i, 0)),
                  pl.BlockSpec((tm, dv), lambda i: (i, 2)),
                  pl.BlockSpec((1, dv), lambda i: (0, 0)),
                  pl.BlockSpec((dv, d), lambda i: (0, 0)),
                  pl.BlockSpec((tm, d), lambda i: (i, 0))],
        out_specs=pl.BlockSpec((tm, d), lambda i: (i, 0)),
        out_shape=jax.ShapeDtypeStruct((s, d), F32),
        compiler_params=_cparams("arbitrary"),
        name="ret_post",
    )(o_f, o_b, proj, gn_w.reshape(1, dv), w_o, res)


def _na_layer(h, norm, w_qkv, rpb, w_o):
    s = h.shape[0]
    d = NA_HEADS * NA_DH
    col_scale = jnp.where(jnp.arange(3 * d) < d, NA_DH ** -0.5, 1.0).astype(F32)
    w16 = (w_qkv * col_scale[None, :]).astype(BF16)
    qkv = norm_matmul(h, norm, w16, BF16, tm=512, tn=3 * d, name="na_qkv")
    o = na_attention(qkv, _na_bias_table(rpb, s // GRID_W))
    return matmul_residual(o, w_o.astype(BF16), h, tm=512, name="na_out")


def _gdn_layer(h, norm, w_in, conv_w, a_log_f, a_log_b, dt_bias_f, dt_bias_b, o_norm, w_o):
    d = GDN_HEADS * GDN_DK
    n_main = 4 * d
    proj = norm_matmul(h, norm, w_in[:, :n_main].astype(BF16), F32, tm=512, tn=n_main, name="gdn_in")
    w_gate = jnp.pad(w_in[:, n_main:], ((0, 0), (0, 128 - 4 * GDN_HEADS))).astype(BF16)
    gp = norm_matmul(h, norm, w_gate, F32, tm=512, tn=128, name="gdn_in_gates")
    z8 = jnp.zeros((2 * GDN_HEADS,), F32)
    pad = jnp.zeros((128 - 4 * GDN_HEADS,), F32)
    alog_vec = jnp.concatenate([z8, a_log_f.astype(F32), a_log_b.astype(F32), pad]).reshape(1, 128)
    dtb_vec = jnp.concatenate([z8, dt_bias_f.astype(F32), dt_bias_b.astype(F32), pad]).reshape(1, 128)
    gcol, grow = gdn_gates(gp, alog_vec, dtb_vec, tm=512)
    qkv = gdn_conv(proj, jnp.transpose(conv_w).astype(F32), tm=256)
    o_f, o_b = gdn_scan(gdn_local(qkv, gcol, grow, tb=256), tb=256)
    return gdn_post(o_f, o_b, proj, o_norm, w_o.astype(BF16), h, tm=512)


def _mla_lane_maps():
    lane = np.arange(128)
    rope1 = lane < 16
    rope2 = (lane >= 64) & (lane < 80)
    nope_a = (lane >= 16) & (lane < 64)
    nope_b = (lane >= 80) & (lane < 96)
    q_dim = np.where(rope1, MLA_NOPE + lane, np.where(nope_a, lane - 16, np.where(rope2, lane + 16, lane - 32)))
    q_ok = lane < 96
    nope_dim = np.where(nope_a, lane - 16, lane - 32)
    nope_ok = nope_a | nope_b
    rope_dim = np.where(rope1, lane, lane - 64 + 16)
    rope_ok = rope1 | rope2
    return q_dim, q_ok, nope_dim, nope_ok, rope_dim, rope_ok


def _mla_layer(h, norm, w_in, q_norm, w_uq, kv_norm, w_ukv, w_o):
    s = h.shape[0]
    hh = MLA_HEADS
    dqk = MLA_NOPE + MLA_ROPE
    q_dim, q_ok, nope_dim, nope_ok, rope_dim, rope_ok = _mla_lane_maps()
    head = np.arange(hh)[:, None]
    q_cols = (head * dqk + np.where(q_ok, q_dim, 0)[None, :]).reshape(-1)
    wuq_t = jnp.transpose(jnp.where(np.tile(q_ok, hh)[None, :], w_uq[:, q_cols], 0.0)).astype(BF16)
    k_cols = (head * (MLA_NOPE + MLA_V) + np.where(nope_ok, nope_dim, 0)[None, :]).reshape(-1)
    wuk_p = jnp.where(np.tile(nope_ok, hh)[None, :], w_ukv[:, k_cols], 0.0).astype(BF16)
    vx = np.arange(MLA_VX)[None, :]
    v_cols = (head * (MLA_NOPE + MLA_V) + MLA_NOPE + np.minimum(vx, MLA_V - 1)).reshape(-1)
    v_ok = np.broadcast_to(vx < MLA_V, (hh, MLA_VX)).reshape(-1)
    wuv_t = jnp.where(v_ok[:, None], jnp.transpose(w_ukv[:, v_cols]), 0.0).astype(BF16)
    vone = jnp.asarray(np.broadcast_to(vx == MLA_V, (hh, MLA_VX)).reshape(-1, 1), F32)
    n_c = MLA_Q_RANK + MLA_KV_RANK
    w_rope = jnp.where(rope_ok[None, :], w_in[:, n_c + np.where(rope_ok, rope_dim, 0)], 0.0)
    win_p = jnp.concatenate([w_in[:, :n_c], w_rope], axis=1).astype(BF16)
    half = MLA_ROPE // 2
    inv = 1.0 / (ROPE_THETA ** (jnp.arange(half, dtype=F32) / half))
    ang = jnp.arange(s, dtype=jnp.int32).astype(F32)[:, None] * inv[None, :]
    cos, sin = jnp.cos(ang), jnp.sin(ang)
    ones = jnp.ones((s, 48), F32)
    zeros = jnp.zeros((s, 48), F32)
    ct = jnp.concatenate([cos, ones, cos, ones], axis=1)
    sn = jnp.concatenate([-sin, zeros, sin, zeros], axis=1)
    q_t, k, v_t = mla_pre(h, norm, win_p, q_norm, kv_norm, wuq_t, wuk_p, wuv_t, vone, ct, sn, tm=512)
    o = mla_attention(q_t, k, v_t, tq=min(2048, s))
    return matmul_residual(o, w_o.astype(BF16), h, tm=512, name="mla_out")


def _ret_tables(reverse):
    c = RET_CHUNK
    lg = jnp.log1p(-jnp.exp2(-5.0 - jnp.arange(RET_HEADS, dtype=F32)))
    if reverse:
        lg = lg[::-1]
    idx = np.arange(c)
    diff = idx[:, None] - idx[None, :]
    if reverse:
        keep, expo = diff < 0, -diff
        q_pow, k_pow = c - idx, idx
    else:
        keep, expo = diff >= 0, diff
        q_pow, k_pow = idx + 1, c - 1 - idx
    dmask = jnp.where(keep, jnp.exp(jnp.where(keep, expo, 0).astype(F32)[None] * lg[:, None, None]), 0.0)
    q_dec = jnp.exp(jnp.asarray(q_pow, F32)[None, :] * lg[:, None])
    k_dec = jnp.exp(jnp.asarray(k_pow, F32)[None, :] * lg[:, None])
    g_chunk = jnp.exp(c * lg)
    bc = lambda t: jnp.broadcast_to(t[:, :, None], (RET_HEADS, c, RET_DK))
    return [dmask, bc(q_dec), bc(k_dec), jnp.broadcast_to(g_chunk[:, None, None], (RET_HEADS, 1, RET_DV))]


def _ret_layer(h, norm, w_in, gn_w, w_o):
    s = h.shape[0]
    proj = norm_matmul(h, norm, w_in.astype(BF16), F32, tm=256, tn=w_in.shape[1], name="ret_in")
    half = RET_DK // 2
    inv = 1.0 / (ROPE_THETA ** (jnp.arange(half, dtype=F32) / half))
    ang = jnp.arange(s, dtype=jnp.int32).astype(F32)[:, None] * inv[None, :]
    cos, sin = jnp.cos(ang), jnp.sin(ang)
    ct = jnp.concatenate([cos, cos], axis=1)
    sn = jnp.concatenate([-sin, sin], axis=1)
    o_f, o_b = ret_scan(proj, ct, sn, _ret_tables(False), _ret_tables(True))
    return ret_post(o_f, o_b, proj, gn_w, w_o.astype(BF16), h, tm=256)


def kernel(x, na_norm, na_w_qkv, na_rpb, na_w_o, gdn_norm, gdn_w_in, gdn_conv, gdn_a_log_f, gdn_a_log_b, gdn_dt_bias_f, gdn_dt_bias_b, gdn_o_norm, gdn_w_o, mla_norm, mla_w_in, mla_q_norm, mla_w_uq, mla_kv_norm, mla_w_ukv, mla_w_o, ret_norm, ret_w_in, ret_gn, ret_w_o, mlp_norm, mlp_w1, mlp_w2, final_norm):
    b, s, d = x.shape
    depth = mlp_norm.shape[0]
    outs = []
    for bi in range(b):
        h = x[bi].astype(F32)
        for i in range(depth):
            m, j = i % 4, i // 4
            if m == 0:
                h = _na_layer(h, na_norm[j], na_w_qkv[j], na_rpb[j], na_w_o[j])
            elif m == 1:
                h = _gdn_layer(h, gdn_norm[j], gdn_w_in[j], gdn_conv[j], gdn_a_log_f[j], gdn_a_log_b[j],
                               gdn_dt_bias_f[j], gdn_dt_bias_b[j], gdn_o_norm[j], gdn_w_o[j])
            elif m == 2:
                h = _mla_layer(h, mla_norm[j], mla_w_in[j], mla_q_norm[j], mla_w_uq[j], mla_kv_norm[j],
                               mla_w_ukv[j], mla_w_o[j])
            else:
                h = _ret_layer(h, ret_norm[j], ret_w_in[j], ret_gn[j], ret_w_o[j])
            h = mlp_block(h, mlp_norm[i], mlp_w1[i].astype(BF16), mlp_w2[i].astype(BF16), final_norm,
                          final=(i == depth - 1), tm=512, tf=mlp_w1.shape[2], name=f"mlp_{i}")
        outs.append(h)
    return jnp.stack(outs).astype(x.dtype)
```

```python
import functools
import math

import numpy as np
import jax
import jax.numpy as jnp
from jax import lax
from jax.experimental import pallas as pl
from jax.experimental.pallas import tpu as pltpu

F32 = jnp.float32
BF16 = jnp.bfloat16
EPS = 1e-6
ROPE_THETA = 10000.0
GRID_W = 64
NEG_BIG = -1e30
HI = lax.Precision.HIGHEST

NA_HEADS, NA_DH, NA_WIN_ROWS, NA_WIN_COLS, NA_Q_ROWS = 16, 64, 8, 16, 2
NA_KBLK = 5
GDN_HEADS, GDN_DK, GDN_CONV, GDN_CHUNK = 8, 128, 5, 64
GDN_GATE_LANES = 32
MLA_HEADS, MLA_Q_RANK, MLA_KV_RANK, MLA_NOPE, MLA_ROPE, MLA_V = 16, 768, 256, 64, 32, 64
MLA_VX = 80
RET_HEADS, RET_DK, RET_DV, RET_CHUNK = 8, 128, 256, 128

VMEM_LIMIT = 52 * 1024 * 1024


def _cparams(*sem):
    return pltpu.CompilerParams(dimension_semantics=sem, vmem_limit_bytes=VMEM_LIMIT)


def _rms(x, g):
    return x * lax.rsqrt(jnp.mean(x * x, axis=-1, keepdims=True) + EPS) * g


def _sigmoid(x):
    return 1.0 / (1.0 + jnp.exp(-x))


def _dot(a, b, **kw):
    return jnp.dot(a, b, preferred_element_type=F32, **kw)


def _dot_nt(a, b, **kw):
    return lax.dot_general(a, b, (((1,), (1,)), ((), ())), preferred_element_type=F32, **kw)


def _dot_tn(a, b, **kw):
    return lax.dot_general(a, b, (((0,), (0,)), ((), ())), preferred_element_type=F32, **kw)


def _norm_matmul_body(x_ref, g_ref, w_ref, o_ref, xn_ref):
    @pl.when(pl.program_id(1) == 0)
    def _():
        xn_ref[...] = _rms(x_ref[...], g_ref[...]).astype(BF16)

    o_ref[...] = _dot(xn_ref[...], w_ref[...]).astype(o_ref.dtype)


def norm_matmul(x, g, w, out_dtype, tm, tn, name):
    s, d = x.shape
    n = w.shape[1]
    return pl.pallas_call(
        _norm_matmul_body,
        grid=(s // tm, n // tn),
        in_specs=[pl.BlockSpec((tm, d), lambda i, j: (i, 0)),
                  pl.BlockSpec((1, d), lambda i, j: (0, 0)),
                  pl.BlockSpec((d, tn), lambda i, j: (0, j))],
        out_specs=pl.BlockSpec((tm, tn), lambda i, j: (i, j)),
        out_shape=jax.ShapeDtypeStruct((s, n), out_dtype),
        scratch_shapes=[pltpu.VMEM((tm, d), BF16)],
        compiler_params=_cparams("arbitrary", "arbitrary"),
        name=name,
    )(x, g.reshape(1, d), w)


def _matmul_res_body(a_ref, w_ref, r_ref, o_ref):
    o_ref[...] = r_ref[...] + _dot(a_ref[...], w_ref[...])


def matmul_residual(a, w, res, tm, name):
    s, k = a.shape
    n = w.shape[1]
    return pl.pallas_call(
        _matmul_res_body,
        grid=(s // tm,),
        in_specs=[pl.BlockSpec((tm, k), lambda i: (i, 0)),
                  pl.BlockSpec((k, n), lambda i: (0, 0)),
                  pl.BlockSpec((tm, n), lambda i: (i, 0))],
        out_specs=pl.BlockSpec((tm, n), lambda i: (i, 0)),
        out_shape=jax.ShapeDtypeStruct((s, n), F32),
        compiler_params=_cparams("arbitrary"),
        name=name,
    )(a, w, res)


def _mlp_body(x_ref, g_ref, w1_ref, w2_ref, fg_ref, o_ref, xn_ref, acc_ref, *, nk, final):
    k = pl.program_id(1)

    @pl.when(k == 0)
    def _():
        xn_ref[...] = _rms(x_ref[...], g_ref[...]).astype(BF16)
        acc_ref[...] = jnp.zeros_like(acc_ref)

    a = jnp.maximum(_dot(xn_ref[...], w1_ref[...]), 0.0)
    acc_ref[...] += _dot((a * a).astype(BF16), w2_ref[...])

    @pl.when(k == nk - 1)
    def _():
        y = x_ref[...] + acc_ref[...]
        if final:
            y = _rms(y, fg_ref[...])
        o_ref[...] = y


def mlp_block(x, g, w1, w2, final_g, final, tm, tf, name):
    s, d = x.shape
    f = w1.shape[1]
    nk = f // tf
    return pl.pallas_call(
        functools.partial(_mlp_body, nk=nk, final=final),
        grid=(s // tm, nk),
        in_specs=[pl.BlockSpec((tm, d), lambda i, k: (i, 0)),
                  pl.BlockSpec((1, d), lambda i, k: (0, 0)),
                  pl.BlockSpec((d, tf), lambda i, k: (0, k)),
                  pl.BlockSpec((tf, d), lambda i, k: (k, 0)),
                  pl.BlockSpec((1, d), lambda i, k: (0, 0))],
        out_specs=pl.BlockSpec((tm, d), lambda i, k: (i, 0)),
        out_shape=jax.ShapeDtypeStruct((s, d), F32),
        scratch_shapes=[pltpu.VMEM((tm, d), BF16), pltpu.VMEM((tm, d), F32)],
        compiler_params=_cparams("arbitrary", "arbitrary"),
        name=name,
    )(x, g.reshape(1, d), w1, w2, final_g.reshape(1, d))


def _na_bias_table(rpb, rows):
    n_qblk = rows // NA_Q_ROWS
    reps = {4: 0, 3: 1, 2: 2, 1: n_qblk - 2, 0: n_qblk - 1}
    w, wc, wr = GRID_W, NA_WIN_COLS, NA_WIN_ROWS
    pcol = jnp.pad(rpb.astype(F32), ((0, 0), (0, 0), (w - wc, w - wc)), mode="edge")
    e = jnp.stack([pcol[:, :, w - 1 - qc:2 * w - 1 - qc] for qc in range(w)], axis=2)
    qc = np.arange(w)
    qc0 = np.clip(qc - wc // 2, 0, w - wc)
    col_ok = (qc[None, :] >= qc0[:, None]) & (qc[None, :] < qc0[:, None] + wc)
    tabs, oks = [], []
    for t in range(5):
        i = reps[t]
        base = int(np.clip(i - 2, 0, n_qblk - NA_KBLK))
        per_qr, ok_qr = [], []
        for qr in range(NA_Q_ROWS):
            qrow = NA_Q_ROWS * i + qr
            qr0 = int(np.clip(qrow - wr // 2, 0, rows - wr))
            krows = NA_Q_ROWS * base + np.arange(NA_KBLK * NA_Q_ROWS)
            drow = np.clip(krows - qrow + wr - 1, 0, 2 * wr - 2)
            row_ok = (krows >= qr0) & (krows < qr0 + wr)
            per_qr.append(jnp.concatenate([e[:, int(d)] for d in drow], axis=-1))
            ok_qr.append(np.concatenate([col_ok & bool(r) for r in row_ok], axis=-1))
        tabs.append(jnp.concatenate(per_qr, axis=1))
        oks.append(np.concatenate(ok_qr, axis=0))
    tab = jnp.where(np.stack(oks)[:, None], jnp.stack(tabs), NEG_BIG)
    return jnp.swapaxes(tab, 2, 3)


def _na_body(q_ref, k0, k1, k2, k3, k4, v0, v1, v2, v3, v4, b_ref, o_ref):
    krefs = (k0, k1, k2, k3, k4)
    vrefs = (v0, v1, v2, v3, v4)
    nkb = NA_Q_ROWS * GRID_W
    lane_k = lax.broadcasted_iota(jnp.int32, (NA_KBLK * nkb, 128), 1)
    ones_blk = jnp.where(lane_k == 0, 1.0, 0.0).astype(BF16)
    first = lax.broadcasted_iota(jnp.int32, (nkb, 128), 1) < NA_DH
    for pr in range(NA_HEADS // 2):
        sl = slice(pr * 128, (pr + 1) * 128)
        k_all = jnp.concatenate([kr[:, sl] for kr in krefs], axis=0)
        v_aug = jnp.concatenate([jnp.concatenate([vr[:, sl] for vr in vrefs], axis=0), ones_blk], axis=1)
        q2 = q_ref[:, sl]
        outs = []
        for hh in range(2):
            qh = jnp.where(first if hh == 0 else jnp.logical_not(first), q2, jnp.zeros_like(q2))
            s_t = _dot_nt(k_all, qh) + b_ref[0, 2 * pr + hh]
            m = jnp.max(s_t, axis=0, keepdims=True)
            p_t = jnp.exp(s_t - m).astype(BF16)
            o = _dot_tn(p_t, v_aug)
            outs.append(o[:, :128] / o[:, 128:129])
        o_ref[:, sl] = jnp.where(first, outs[0], outs[1]).astype(o_ref.dtype)


def na_attention(qkv, bias_tab):
    s = qkv.shape[0]
    d = NA_HEADS * NA_DH
    tq = NA_Q_ROWS * GRID_W
    n_qblk = s // tq
    top = n_qblk - NA_KBLK

    def base(i):
        return jnp.clip(i - 2, 0, top)

    kv_specs = [pl.BlockSpec((tq, d), functools.partial(lambda i, dd, c: (base(i) + dd, c), dd=dd, c=c))
                for c in (1, 2) for dd in range(NA_KBLK)]
    return pl.pallas_call(
        _na_body,
        grid=(n_qblk,),
        in_specs=[pl.BlockSpec((tq, d), lambda i: (i, 0))] + kv_specs
                 + [pl.BlockSpec((1, NA_HEADS, NA_KBLK * tq, tq), lambda i: (base(i) - i + 4, 0, 0, 0))],
        out_specs=pl.BlockSpec((tq, d), lambda i: (i, 0)),
        out_shape=jax.ShapeDtypeStruct((s, d), BF16),
        compiler_params=_cparams("arbitrary"),
        name="na_attention",
    )(qkv, *([qkv] * (2 * NA_KBLK)), bias_tab)


def _gdn_conv_body(x_ref, xp_ref, xn_ref, w_ref, o_ref, xe_ref, *, tm, nt):
    i = pl.program_id(0)
    j = pl.program_id(1)
    xe_ref[0:8, :] = jnp.where(i > 0, xp_ref[...], 0.0)
    xe_ref[8:8 + tm, :] = x_ref[...]
    xe_ref[8 + tm:16 + tm, :] = jnp.where(i < nt - 1, xn_ref[...], 0.0)
    w = w_ref[...]
    half = GDN_CONV // 2
    acc = xe_ref[8 - half:8 - half + tm, :] * w[0:1]
    for t in range(1, GDN_CONV):
        acc = acc + xe_ref[8 - half + t:8 - half + t + tm, :] * w[t:t + 1]
    y = acc * _sigmoid(acc)
    for h in range(GDN_HEADS):
        sl = slice(h * GDN_DK, (h + 1) * GDN_DK)
        yh = y[:, sl]
        nrm = lax.rsqrt(jnp.sum(yh * yh, axis=1, keepdims=True) + EPS)
        fac = jnp.where(j == 0, nrm * GDN_DK ** -0.5, jnp.where(j == 1, nrm, 1.0))
        o_ref[:, sl] = (yh * fac).astype(o_ref.dtype)


def gdn_conv(proj, conv_w_t, tm):
    s = proj.shape[0]
    d = GDN_HEADS * GDN_DK
    nt = s // tm
    r8 = tm // 8
    return pl.pallas_call(
        functools.partial(_gdn_conv_body, tm=tm, nt=nt),
        grid=(nt, 3),
        in_specs=[pl.BlockSpec((tm, d), lambda i, j: (i, j)),
                  pl.BlockSpec((8, d), lambda i, j: (jnp.maximum(i * r8 - 1, 0), j)),
                  pl.BlockSpec((8, d), lambda i, j: (jnp.minimum((i + 1) * r8, nt * r8 - 1), j)),
                  pl.BlockSpec((GDN_CONV, d), lambda i, j: (0, j))],
        out_specs=pl.BlockSpec((tm, d), lambda i, j: (i, j)),
        out_shape=jax.ShapeDtypeStruct((s, 3 * d), BF16),
        scratch_shapes=[pltpu.VMEM((tm + 16, d), F32)],
        compiler_params=_cparams("arbitrary", "arbitrary"),
        name="gdn_conv",
    )(proj, proj, proj, conv_w_t)


def _gdn_gates_body(gp_ref, alog_ref, dtb_ref, col_ref, row_ref, *, tm):
    c = GDN_CHUNK
    gp = gp_ref[...]
    lane = lax.broadcasted_iota(jnp.int32, (tm, 128), 1)
    is_f = (lane >= 16) & (lane < 24)
    is_b = (lane >= 24) & (lane < 32)
    beta = _sigmoid(gp)
    z = gp + dtb_ref[...]
    softplus = jnp.maximum(z, 0.0) + jnp.log1p(jnp.exp(-jnp.abs(z)))
    g = jnp.where(is_f | is_b, -jnp.exp(alog_ref[...]) * softplus, 0.0)
    r = lax.broadcasted_iota(jnp.int32, (c, c), 0)
    cc = lax.broadcasted_iota(jnp.int32, (c, c), 1)
    tri_lo = (r >= cc).astype(F32)
    tri_up = (r <= cc).astype(F32)
    lane_c = lax.broadcasted_iota(jnp.int32, (c, 128), 1)
    is_f_c = (lane_c >= 16) & (lane_c < 24)
    is_g_c = (lane_c >= 16) & (lane_c < 32)
    for ch in range(tm // c):
        gc = g[ch * c:(ch + 1) * c]
        cum = jnp.where(is_f_c, _dot(tri_lo, gc, precision=HI), _dot(tri_up, gc, precision=HI))
        colc = jnp.where(is_g_c, cum, beta[ch * c:(ch + 1) * c])
        col_ref[ch * c:(ch + 1) * c, :] = colc
        row_ref[ch] = colc.T[:GDN_GATE_LANES, :]


def gdn_gates(gp, gate_blk, alog_vec, dtb_vec, tm):
    s = gp.shape[0]
    c = GDN_CHUNK
    return pl.pallas_call(
        functools.partial(_gdn_gates_body, tm=tm),
        grid=(s // tm,),
        in_specs=[pl.BlockSpec((tm, 128), lambda i: (i, gate_blk)),
                  pl.BlockSpec((1, 128), lambda i: (0, 0)),
                  pl.BlockSpec((1, 128), lambda i: (0, 0))],
        out_specs=[pl.BlockSpec((tm, 128), lambda i: (i, 0)),
                   pl.BlockSpec((tm // c, GDN_GATE_LANES, c), lambda i: (i, 0, 0))],
        out_shape=[jax.ShapeDtypeStruct((s, 128), F32), jax.ShapeDtypeStruct((s // c, GDN_GATE_LANES, c), F32)],
        compiler_params=_cparams("arbitrary"),
        name="gdn_gates",
    )(gp, alog_vec, dtb_vec)


def _bmm(a, b):
    return jnp.einsum("nik,nkj->nij", a, b, preferred_element_type=F32)


def _bmm_nt(a, b):
    return jnp.einsum("nik,njk->nij", a, b, preferred_element_type=F32)


def _unit_tri_inverse(lmat, r, cc):
    eye = (r == cc).astype(F32)
    diag_blk = (r // 16) == (cc // 16)
    ld = jnp.where(diag_blk, lmat, 0.0)
    lo = lmat - ld
    x = eye - ld
    p = _bmm(ld, ld)
    x = x + _bmm(x, p)
    p = _bmm(p, p)
    x = x + _bmm(x, p)
    p = _bmm(p, p)
    dinv = x + _bmm(x, p)
    n1 = _bmm(dinv, lo)
    n2 = _bmm(n1, n1)
    y = eye - n1 + n2 - _bmm(n1, n2)
    return _bmm(y, dinv)


def _gdn_local_body(qkv_ref, col_ref, row_ref, *out_refs, tb):
    c, dk, hh = GDN_CHUNK, GDN_DK, GDN_HEADS
    nc = tb // c
    nb = hh * nc
    r = lax.broadcasted_iota(jnp.int32, (nb, c, c), 1)
    cc = lax.broadcasted_iota(jnp.int32, (nb, c, c), 2)
    col3 = col_ref[...].reshape(nc, c, 128)

    def heads(base):
        return jnp.concatenate([qkv_ref[:, base + h * dk:base + (h + 1) * dk].astype(F32).reshape(nc, c, dk)
                                for h in range(hh)], axis=0)

    qc, kc, vc = heads(0), heads(hh * dk), heads(2 * hh * dk)
    kc16 = kc.astype(BF16)
    qk = _bmm_nt(qc.astype(BF16), kc16)
    for d in range(2):
        reverse = d == 1
        u_ref, wq_ref, ak_ref, gl_ref = out_refs[4 * d:4 * d + 4]
        incl = (r <= cc) if reverse else (r >= cc)
        strict = (r < cc) if reverse else (r > cc)
        b0 = 8 if reverse else 0
        g0 = 24 if reverse else 16
        beta = jnp.concatenate([col3[:, :, b0 + h:b0 + h + 1] for h in range(hh)], axis=0)
        gcol = jnp.concatenate([col3[:, :, g0 + h:g0 + h + 1] for h in range(hh)], axis=0)
        grow = jnp.concatenate([row_ref[:, g0 + h:g0 + h + 1, :] for h in range(hh)], axis=0)
        g_last = gcol[:, 0:1, :] if reverse else gcol[:, c - 1:c, :]
        decay = jnp.where(incl, jnp.exp(jnp.where(incl, gcol - grow, 0.0)), 0.0)
        eg = jnp.exp(gcol)
        kb = kc * beta
        lmat = jnp.where(strict, _bmm_nt(kb.astype(BF16), kc16) * decay, 0.0)
        tinv = _unit_tri_inverse(lmat, r, cc)
        sol = _bmm(tinv, jnp.concatenate([vc * beta, kb * eg], axis=2))
        a_intra = jnp.where(incl, qk * decay, 0.0).astype(BF16)
        w16 = sol[:, :, dk:].astype(BF16)
        qd16 = (qc * eg).astype(BF16)
        kdt16 = jnp.swapaxes(kc * jnp.exp(g_last - gcol), 1, 2).astype(BF16)
        gl = jnp.broadcast_to(jnp.exp(g_last), (nb, 1, dk))
        for h in range(hh):
            sl = slice(h * dk, (h + 1) * dk)
            hb = slice(h * nc, (h + 1) * nc)
            u_ref[:, sl] = sol[hb, :, :dk].reshape(tb, dk)
            wq_ref[:, 0:c, sl] = w16[hb]
            wq_ref[:, c:2 * c, sl] = qd16[hb]
            ak_ref[:, 0:c, h * c:(h + 1) * c] = a_intra[hb]
            ak_ref[:, c:c + dk, h * c:(h + 1) * c] = kdt16[hb]
            gl_ref[:, h:h + 1, :] = gl[hb]


def gdn_local(qkv, gcol, grow, tb):
    s = qkv.shape[0]
    c, d = GDN_CHUNK, GDN_HEADS * GDN_DK
    nc = tb // c
    row = lambda w: pl.BlockSpec((tb, w), lambda i: (i, 0))
    per_dir_specs = [row(d),
                     pl.BlockSpec((nc, 2 * c, d), lambda i: (i, 0, 0)),
                     pl.BlockSpec((nc, c + GDN_DK, GDN_HEADS * c), lambda i: (i, 0, 0)),
                     pl.BlockSpec((nc, GDN_HEADS, GDN_DK), lambda i: (i, 0, 0))]
    per_dir_shapes = [jax.ShapeDtypeStruct((s, d), F32),
                      jax.ShapeDtypeStruct((s // c, 2 * c, d), BF16),
                      jax.ShapeDtypeStruct((s // c, c + GDN_DK, GDN_HEADS * c), BF16),
                      jax.ShapeDtypeStruct((s // c, GDN_HEADS, GDN_DK), F32)]
    return pl.pallas_call(
        functools.partial(_gdn_local_body, tb=tb),
        grid=(s // tb,),
        in_specs=[row(3 * d), row(128), pl.BlockSpec((nc, GDN_GATE_LANES, c), lambda i: (i, 0, 0))],
        out_specs=per_dir_specs * 2,
        out_shape=per_dir_shapes * 2,
        compiler_params=_cparams("arbitrary"),
        name="gdn_local",
    )(qkv, gcol, grow)


def _gdn_scan_body(*refs, tb):
    c, dk, hh = GDN_CHUNK, GDN_DK, GDN_HEADS
    nc = tb // c
    ins, (of_ref, ob_ref, st_ref) = refs[:8], refs[8:]

    @pl.when(pl.program_id(0) == 0)
    def _():
        st_ref[...] = jnp.zeros_like(st_ref)

    for step in range(nc):
        for d in range(2):
            u_ref, wq_ref, ak_ref, gl_ref = ins[4 * d:4 * d + 4]
            o_ref = (of_ref, ob_ref)[d]
            ch = nc - 1 - step if d == 1 else step
            rows = slice(ch * c, (ch + 1) * c)
            for h in range(hh):
                sl = slice(h * dk, (h + 1) * dk)
                state = st_ref[d, h]
                ws_qs = _dot(wq_ref[ch, :, sl], state.astype(BF16))
                v_new = u_ref[rows, sl] - ws_qs[:c]
                av_kv = _dot(ak_ref[ch, :, h * c:(h + 1) * c], v_new.astype(BF16))
                o_ref[rows, sl] = ws_qs[c:] + av_kv[:c]
                st_ref[d, h] = state * gl_ref[ch, h:h + 1, :] + av_kv[c:]


def gdn_scan(local_out, tb):
    s, d = local_out[0].shape
    c = GDN_CHUNK
    nb = s // tb
    nc = tb // c

    def dir_specs(blk):
        return [pl.BlockSpec((tb, d), lambda t: (blk(t), 0)),
                pl.BlockSpec((nc, 2 * c, d), lambda t: (blk(t), 0, 0)),
                pl.BlockSpec((nc, c + GDN_DK, GDN_HEADS * c), lambda t: (blk(t), 0, 0)),
                pl.BlockSpec((nc, GDN_HEADS, GDN_DK), lambda t: (blk(t), 0, 0))]

    fwd = lambda t: t
    bwd = lambda t: nb - 1 - t
    return pl.pallas_call(
        functools.partial(_gdn_scan_body, tb=tb),
        grid=(nb,),
        in_specs=dir_specs(fwd) + dir_specs(bwd),
        out_specs=[pl.BlockSpec((tb, d), lambda t: (t, 0)), pl.BlockSpec((tb, d), lambda t: (nb - 1 - t, 0))],
        out_shape=[jax.ShapeDtypeStruct((s, d), F32)] * 2,
        scratch_shapes=[pltpu.VMEM((2, GDN_HEADS, GDN_DK, GDN_DK), F32)],
        compiler_params=_cparams("arbitrary"),
        name="gdn_scan",
    )(*local_out)


def _gdn_post_body(of_ref, ob_ref, z_ref, on_ref, w_ref, r_ref, o_ref):
    on = on_ref[...]
    parts = []
    for h in range(GDN_HEADS):
        sl = slice(h * GDN_DK, (h + 1) * GDN_DK)
        o = of_ref[:, sl] + ob_ref[:, sl]
        z = z_ref[:, sl]
        parts.append((_rms(o, on) * (z * _sigmoid(z))).astype(BF16))
    y = jnp.concatenate(parts, axis=1)
    o_ref[...] = r_ref[...] + _dot(y, w_ref[...])


def gdn_post(o_f, o_b, proj, o_norm, w_o, res, tm):
    s, d = o_f.shape
    return pl.pallas_call(
        _gdn_post_body,
        grid=(s // tm,),
        in_specs=[pl.BlockSpec((tm, d), lambda i: (i, 0)),
                  pl.BlockSpec((tm, d), lambda i: (i, 0)),
                  pl.BlockSpec((tm, d), lambda i: (i, 3)),
                  pl.BlockSpec((1, GDN_DK), lambda i: (0, 0)),
                  pl.BlockSpec((d, d), lambda i: (0, 0)),
                  pl.BlockSpec((tm, d), lambda i: (i, 0))],
        out_specs=pl.BlockSpec((tm, d), lambda i: (i, 0)),
        out_shape=jax.ShapeDtypeStruct((s, d), F32),
        compiler_params=_cparams("arbitrary"),
        name="gdn_post",
    )(o_f, o_b, proj, o_norm.reshape(1, GDN_DK), w_o, res)


def _mla_pre_body(x_ref, g_ref, win_ref, qn_ref, kvn_ref, wuq_ref, wuk_ref, wuv_ref, vone_ref, ct_ref, sn_ref,
                  ctt_ref, snt_ref, q_ref, k_ref, v_ref):
    xn = _rms(x_ref[...], g_ref[...]).astype(BF16)
    proj = _dot(xn, win_ref[...])
    cq = _rms(proj[:, :MLA_Q_RANK], qn_ref[...]).astype(BF16)
    ckv = _rms(proj[:, MLA_Q_RANK:MLA_Q_RANK + MLA_KV_RANK], kvn_ref[...]).astype(BF16)
    pr = proj[:, MLA_Q_RANK + MLA_KV_RANK:]
    ct, sn = ct_ref[...], sn_ref[...]
    scale = (MLA_NOPE + MLA_ROPE) ** -0.5 * math.log2(math.e)
    ctt, snt = ctt_ref[...], snt_ref[...]
    q_t = _dot_nt(wuq_ref[...], cq)
    for h in range(MLA_HEADS):
        qh = q_t[h * 128:(h + 1) * 128, :]
        partner = jnp.concatenate([qh[64:, :], qh[:64, :]], axis=0)
        q_ref[h * 128:(h + 1) * 128, :] = ((qh * ctt + partner * snt) * scale).astype(BF16)
    kr = pr * ct + pltpu.roll(pr, 64, 1) * sn
    k = _dot(ckv, wuk_ref[...])
    for h in range(MLA_HEADS):
        sl = slice(h * 128, (h + 1) * 128)
        k_ref[:, sl] = (k[:, sl] + kr).astype(BF16)
    v_t = (_dot_nt(wuv_ref[...], ckv) + vone_ref[...]).astype(BF16)
    v_ref[...] = v_t.reshape(v_ref.shape)


def mla_pre(x, g, win_p, q_norm, kv_norm, wuq_t, wuk_p, wuv_t, vone, ct, sn, tm):
    s, d = x.shape
    hp = MLA_HEADS * 128
    npair = MLA_HEADS // 2
    full = lambda a: pl.BlockSpec(a.shape, lambda i: (0,) * a.ndim)
    g2, qn2, kvn2 = g.reshape(1, d), q_norm.reshape(1, -1), kv_norm.reshape(1, -1)
    ct_t, sn_t = jnp.transpose(ct), jnp.transpose(sn)
    return pl.pallas_call(
        _mla_pre_body,
        grid=(s // tm,),
        in_specs=[pl.BlockSpec((tm, d), lambda i: (i, 0)), full(g2), full(win_p), full(qn2), full(kvn2),
                  full(wuq_t), full(wuk_p), full(wuv_t), full(vone),
                  pl.BlockSpec((tm, 128), lambda i: (i, 0)), pl.BlockSpec((tm, 128), lambda i: (i, 0)),
                  pl.BlockSpec((128, tm), lambda i: (0, i)), pl.BlockSpec((128, tm), lambda i: (0, i))],
        out_specs=[pl.BlockSpec((hp, tm), lambda i: (0, i)), pl.BlockSpec((tm, hp), lambda i: (i, 0)),
                   pl.BlockSpec((npair, 1, 2 * MLA_VX, tm), lambda i: (0, i, 0, 0))],
        out_shape=[jax.ShapeDtypeStruct((hp, s), BF16), jax.ShapeDtypeStruct((s, hp), BF16),
                   jax.ShapeDtypeStruct((npair, s // tm, 2 * MLA_VX, tm), BF16)],
        compiler_params=_cparams("arbitrary"),
        name="mla_pre",
    )(x, g2, win_p, qn2, kvn2, wuq_t, wuk_p, wuv_t, vone, ct, sn, ct_t, sn_t)


def _mla_attn_body(qt_ref, k_ref, vt_ref, o_ref, acc_ref, s_ref, *, tq, tk, nk):
    acc_ref[...] = jnp.zeros_like(acc_ref)

    def scores(j, slot):
        off = pl.multiple_of(j * tk, tk)
        for hh in range(2):
            sl = slice(hh * 128, (hh + 1) * 128)
            s_ref[slot, hh] = _dot(k_ref[pl.ds(off, tk), sl], qt_ref[sl, :])

    def consume(j, slot, carry):
        new = []
        for hh in range(2):
            m_old = carry[hh]
            rows = slice(hh * MLA_VX, (hh + 1) * MLA_VX)
            s = s_ref[slot, hh]
            m_new = jnp.maximum(m_old, jnp.max(s, axis=0, keepdims=True))
            p = jnp.exp2(s - m_new).astype(BF16)
            acc_ref[rows, :] = acc_ref[rows, :] * jnp.exp2(m_old - m_new) + _dot(vt_ref[0, j, rows, :], p)
            new.append(m_new)
        return tuple(new)

    scores(0, 0)

    def body(jj, carry):
        j = 2 * jj
        scores(j + 1, 1)
        carry = consume(j, 0, carry)
        scores(jnp.minimum(j + 2, nk - 1), 0)
        return consume(j + 1, 1, carry)

    lax.fori_loop(0, nk // 2, body, (jnp.full((1, tq), -jnp.inf, F32),) * 2)
    o_t = jnp.concatenate([acc_ref[hh * MLA_VX:hh * MLA_VX + MLA_V, :]
                           / acc_ref[hh * MLA_VX + MLA_V:hh * MLA_VX + MLA_V + 1, :] for hh in range(2)], axis=0)
    o_ref[...] = o_t.T.astype(o_ref.dtype)


def mla_attention(q_t, k, v_t, tq):
    s = k.shape[0]
    npair, nk, _, tk = v_t.shape
    return pl.pallas_call(
        functools.partial(_mla_attn_body, tq=tq, tk=tk, nk=nk),
        grid=(npair, s // tq),
        in_specs=[pl.BlockSpec((256, tq), lambda p, i: (p, i)),
                  pl.BlockSpec((s, 256), lambda p, i: (0, p)),
                  pl.BlockSpec((1, nk, 2 * MLA_VX, tk), lambda p, i: (p, 0, 0, 0))],
        out_specs=pl.BlockSpec((tq, 128), lambda p, i: (i, p)),
        out_shape=jax.ShapeDtypeStruct((s, MLA_HEADS * MLA_V), BF16),
        scratch_shapes=[pltpu.VMEM((2 * MLA_VX, tq), F32), pltpu.VMEM((2, 2, tk, tq), F32)],
        compiler_params=_cparams("arbitrary", "arbitrary"),
        name="mla_attention",
    )(q_t, k, v_t)


def _ret_direction(q_ref, k_ref, v_ref, ct_ref, sn_ref, dm_ref, qd_ref, kd_ref, gc_ref, o_ref, st_ref):
    ct, sn = ct_ref[...], sn_ref[...]
    for h in range(RET_HEADS):
        sk = slice(h * RET_DK, (h + 1) * RET_DK)
        sv = slice(h * RET_DV, (h + 1) * RET_DV)
        qh = q_ref[:, sk]
        kh = k_ref[:, sk]
        qr = qh * ct + pltpu.roll(qh, 64, 1) * sn
        kr = (kh * ct + pltpu.roll(kh, 64, 1) * sn) * RET_DK ** -0.5
        v16 = v_ref[:, sv].astype(BF16)
        state = st_ref[h]
        inner = _dot_nt(qr.astype(BF16), kr.astype(BF16)) * dm_ref[h]
        o_ref[:, sv] = (_dot(inner.astype(BF16), v16)
                        + _dot((qr * qd_ref[h]).astype(BF16), state.astype(BF16)))
        st_ref[h] = state * gc_ref[h] + _dot_tn((kr * kd_ref[h]).astype(BF16), v16)


def _ret_scan_body(qf, kf, vf, ctf, snf, qb, kb, vb, ctb, snb,
                   dmf, qdf, kdf, gcf, dmb, qdb, kdb, gcb, of_ref, ob_ref, sf_ref, sb_ref):
    @pl.when(pl.program_id(0) == 0)
    def _():
        sf_ref[...] = jnp.zeros_like(sf_ref)
        sb_ref[...] = jnp.zeros_like(sb_ref)

    _ret_direction(qf, kf, vf, ctf, snf, dmf, qdf, kdf, gcf, of_ref, sf_ref)
    _ret_direction(qb, kb, vb, ctb, snb, dmb, qdb, kdb, gcb, ob_ref, sb_ref)


def ret_scan(proj, ct, sn, tabs_f, tabs_b):
    s = proj.shape[0]
    c = RET_CHUNK
    nb = s // c
    dq = RET_HEADS * RET_DK
    dv = RET_HEADS * RET_DV

    def dir_specs(blk):
        return [pl.BlockSpec((c, dq), lambda t: (blk(t), 0)),
                pl.BlockSpec((c, dq), lambda t: (blk(t), 1)),
                pl.BlockSpec((c, dv), lambda t: (blk(t), 1)),
                pl.BlockSpec((c, 128), lambda t: (blk(t), 0)),
                pl.BlockSpec((c, 128), lambda t: (blk(t), 0))]

    full = lambda a: pl.BlockSpec(a.shape, lambda t: (0,) * a.ndim)
    fwd = lambda t: t
    bwd = lambda t: nb - 1 - t
    return pl.pallas_call(
        _ret_scan_body,
        grid=(nb,),
        in_specs=dir_specs(fwd) + dir_specs(bwd) + [full(a) for a in tabs_f] + [full(a) for a in tabs_b],
        out_specs=[pl.BlockSpec((c, dv), lambda t: (t, 0)), pl.BlockSpec((c, dv), lambda t: (nb - 1 - t, 0))],
        out_shape=[jax.ShapeDtypeStruct((s, dv), F32)] * 2,
        scratch_shapes=[pltpu.VMEM((RET_HEADS, RET_DK, RET_DV), F32)] * 2,
        compiler_params=_cparams("arbitrary"),
        name="ret_scan",
    )(proj, proj, proj, ct, sn, proj, proj, proj, ct, sn, *tabs_f, *tabs_b)


def _ret_post_body(of_ref, ob_ref, gate_ref, gn_ref, w_ref, r_ref, o_ref):
    parts = []
    for h in range(RET_HEADS):
        sv = slice(h * RET_DV, (h + 1) * RET_DV)
        o = of_ref[:, sv] + ob_ref[:, sv]
        mu = jnp.mean(o, axis=1, keepdims=True)
        oc = o - mu
        var = jnp.mean(oc * oc, axis=1, keepdims=True)
        gate = gate_ref[:, sv]
        parts.append((gate * _sigmoid(gate) * (oc * lax.rsqrt(var + EPS) * gn_ref[:, sv])).astype(BF16))
    y = jnp.concatenate(parts, axis=1)
    o_ref[...] = r_ref[...] + _dot(y, w_ref[...])


def ret_post(o_f, o_b, proj, gn_w, w_o, res, tm):
    s, dv = o_f.shape
    d = res.shape[1]
    return pl.pallas_call(
        _ret_post_body,
        grid=(s // tm,),
        in_specs=[pl.BlockSpec((tm, dv), lambda i: (i, 0)),
                  pl.BlockSpec((tm, dv), lambda i: (i, 0)),
                  pl.BlockSpec((tm, dv), lambda i: (i, 2)),
                  pl.BlockSpec((1, dv), lambda i: (0, 0)),
                  pl.BlockSpec((dv, d), lambda i: (0, 0)),
                  pl.BlockSpec((tm, d), lambda i: (i, 0))],
        out_specs=pl.BlockSpec((tm, d), lambda i: (i, 0)),
        out_shape=jax.ShapeDtypeStruct((s, d), F32),
        compiler_params=_cparams("arbitrary"),
        name="ret_post",
    )(o_f, o_b, proj, gn_w.reshape(1, dv), w_o, res)


def _na_layer(h, norm, w_qkv, rpb, w_o):
    s = h.shape[0]
    d = NA_HEADS * NA_DH
    col_scale = jnp.where(jnp.arange(3 * d) < d, NA_DH ** -0.5, 1.0).astype(F32)
    w16 = (w_qkv * col_scale[None, :]).astype(BF16)
    qkv = norm_matmul(h, norm, w16, BF16, tm=512, tn=3 * d, name="na_qkv")
    o = na_attention(qkv, _na_bias_table(rpb, s // GRID_W))
    return matmul_residual(o, w_o.astype(BF16), h, tm=512, name="na_out")


def _gdn_layer(h, norm, w_in, conv_w, a_log_f, a_log_b, dt_bias_f, dt_bias_b, o_norm, w_o):
    d = GDN_HEADS * GDN_DK
    n_main = 4 * d
    w_all = jnp.pad(w_in, ((0, 0), (0, 128 - 4 * GDN_HEADS))).astype(BF16)
    proj = norm_matmul(h, norm, w_all, F32, tm=512, tn=n_main + 128, name="gdn_in")
    z8 = jnp.zeros((2 * GDN_HEADS,), F32)
    pad = jnp.zeros((128 - 4 * GDN_HEADS,), F32)
    alog_vec = jnp.concatenate([z8, a_log_f.astype(F32), a_log_b.astype(F32), pad]).reshape(1, 128)
    dtb_vec = jnp.concatenate([z8, dt_bias_f.astype(F32), dt_bias_b.astype(F32), pad]).reshape(1, 128)
    gcol, grow = gdn_gates(proj, n_main // 128, alog_vec, dtb_vec, tm=512)
    qkv = gdn_conv(proj, jnp.transpose(conv_w).astype(F32), tm=256)
    o_f, o_b = gdn_scan(gdn_local(qkv, gcol, grow, tb=256), tb=256)
    return gdn_post(o_f, o_b, proj, o_norm, w_o.astype(BF16), h, tm=512)


def _mla_lane_maps():
    lane = np.arange(128)
    rope1 = lane < 16
    rope2 = (lane >= 64) & (lane < 80)
    nope_a = (lane >= 16) & (lane < 64)
    nope_b = (lane >= 80) & (lane < 96)
    q_dim = np.where(rope1, MLA_NOPE + lane, np.where(nope_a, lane - 16, np.where(rope2, lane + 16, lane - 32)))
    q_ok = lane < 96
    nope_dim = np.where(nope_a, lane - 16, lane - 32)
    nope_ok = nope_a | nope_b
    rope_dim = np.where(rope1, lane, lane - 64 + 16)
    rope_ok = rope1 | rope2
    return q_dim, q_ok, nope_dim, nope_ok, rope_dim, rope_ok


def _mla_layer(h, norm, w_in, q_norm, w_uq, kv_norm, w_ukv, w_o):
    s = h.shape[0]
    hh = MLA_HEADS
    dqk = MLA_NOPE + MLA_ROPE
    q_dim, q_ok, nope_dim, nope_ok, rope_dim, rope_ok = _mla_lane_maps()
    head = np.arange(hh)[:, None]
    q_cols = (head * dqk + np.where(q_ok, q_dim, 0)[None, :]).reshape(-1)
    wuq_t = jnp.transpose(jnp.where(np.tile(q_ok, hh)[None, :], w_uq[:, q_cols], 0.0)).astype(BF16)
    k_cols = (head * (MLA_NOPE + MLA_V) + np.where(nope_ok, nope_dim, 0)[None, :]).reshape(-1)
    wuk_p = jnp.where(np.tile(nope_ok, hh)[None, :], w_ukv[:, k_cols], 0.0).astype(BF16)
    vx = np.arange(MLA_VX)[None, :]
    v_cols = (head * (MLA_NOPE + MLA_V) + MLA_NOPE + np.minimum(vx, MLA_V - 1)).reshape(-1)
    v_ok = np.broadcast_to(vx < MLA_V, (hh, MLA_VX)).reshape(-1)
    wuv_t = jnp.where(v_ok[:, None], jnp.transpose(w_ukv[:, v_cols]), 0.0).astype(BF16)
    vone = jnp.asarray(np.broadcast_to(vx == MLA_V, (hh, MLA_VX)).reshape(-1, 1), F32)
    n_c = MLA_Q_RANK + MLA_KV_RANK
    w_rope = jnp.where(rope_ok[None, :], w_in[:, n_c + np.where(rope_ok, rope_dim, 0)], 0.0)
    win_p = jnp.concatenate([w_in[:, :n_c], w_rope], axis=1).astype(BF16)
    half = MLA_ROPE // 2
    inv = 1.0 / (ROPE_THETA ** (jnp.arange(half, dtype=F32) / half))
    ang = jnp.arange(s, dtype=jnp.int32).astype(F32)[:, None] * inv[None, :]
    cos, sin = jnp.cos(ang), jnp.sin(ang)
    ones = jnp.ones((s, 48), F32)
    zeros = jnp.zeros((s, 48), F32)
    ct = jnp.concatenate([cos, ones, cos, ones], axis=1)
    sn = jnp.concatenate([-sin, zeros, sin, zeros], axis=1)
    q_t, k, v_t = mla_pre(h, norm, win_p, q_norm, kv_norm, wuq_t, wuk_p, wuv_t, vone, ct, sn, tm=512)
    o = mla_attention(q_t, k, v_t, tq=min(2048, s))
    return matmul_residual(o, w_o.astype(BF16), h, tm=512, name="mla_out")


def _ret_tables(reverse):
    c = RET_CHUNK
    lg = jnp.log1p(-jnp.exp2(-5.0 - jnp.arange(RET_HEADS, dtype=F32)))
    if reverse:
        lg = lg[::-1]
    idx = np.arange(c)
    diff = idx[:, None] - idx[None, :]
    if reverse:
        keep, expo = diff < 0, -diff
        q_pow, k_pow = c - idx, idx
    else:
        keep, expo = diff >= 0, diff
        q_pow, k_pow = idx + 1, c - 1 - idx
    dmask = jnp.where(keep, jnp.exp(jnp.where(keep, expo, 0).astype(F32)[None] * lg[:, None, None]), 0.0)
    q_dec = jnp.exp(jnp.asarray(q_pow, F32)[None, :] * lg[:, None])
    k_dec = jnp.exp(jnp.asarray(k_pow, F32)[None, :] * lg[:, None])
    g_chunk = jnp.exp(c * lg)
    bc = lambda t: jnp.broadcast_to(t[:, :, None], (RET_HEADS, c, RET_DK))
    return [dmask, bc(q_dec), bc(k_dec), jnp.broadcast_to(g_chunk[:, None, None], (RET_HEADS, 1, RET_DV))]


def _ret_layer(h, norm, w_in, gn_w, w_o):
    s = h.shape[0]
    proj = norm_matmul(h, norm, w_in.astype(BF16), F32, tm=256, tn=w_in.shape[1], name="ret_in")
    half = RET_DK // 2
    inv = 1.0 / (ROPE_THETA ** (jnp.arange(half, dtype=F32) / half))
    ang = jnp.arange(s, dtype=jnp.int32).astype(F32)[:, None] * inv[None, :]
    cos, sin = jnp.cos(ang), jnp.sin(ang)
    ct = jnp.concatenate([cos, cos], axis=1)
    sn = jnp.concatenate([-sin, sin], axis=1)
    o_f, o_b = ret_scan(proj, ct, sn, _ret_tables(False), _ret_tables(True))
    return ret_post(o_f, o_b, proj, gn_w, w_o.astype(BF16), h, tm=256)


def kernel(x, na_norm, na_w_qkv, na_rpb, na_w_o, gdn_norm, gdn_w_in, gdn_conv, gdn_a_log_f, gdn_a_log_b, gdn_dt_bias_f, gdn_dt_bias_b, gdn_o_norm, gdn_w_o, mla_norm, mla_w_in, mla_q_norm, mla_w_uq, mla_kv_norm, mla_w_ukv, mla_w_o, ret_norm, ret_w_in, ret_gn, ret_w_o, mlp_norm, mlp_w1, mlp_w2, final_norm):
    b, s, d = x.shape
    depth = mlp_norm.shape[0]
    outs = []
    for bi in range(b):
        h = x[bi].astype(F32)
        for i in range(depth):
            m, j = i % 4, i // 4
            if m == 0:
                h = _na_layer(h, na_norm[j], na_w_qkv[j], na_rpb[j], na_w_o[j])
            elif m == 1:
                h = _gdn_layer(h, gdn_norm[j], gdn_w_in[j], gdn_conv[j], gdn_a_log_f[j], gdn_a_log_b[j],
                               gdn_dt_bias_f[j], gdn_dt_bias_b[j], gdn_o_norm[j], gdn_w_o[j])
            elif m == 2:
                h = _mla_layer(h, mla_norm[j], mla_w_in[j], mla_q_norm[j], mla_w_uq[j], mla_kv_norm[j],
                               mla_w_ukv[j], mla_w_o[j])
            else:
                h = _ret_layer(h, ret_norm[j], ret_w_in[j], ret_gn[j], ret_w_o[j])
            h = mlp_block(h, mlp_norm[i], mlp_w1[i].astype(BF16), mlp_w2[i].astype(BF16), final_norm,
                          final=(i == depth - 1), tm=512, tf=mlp_w1.shape[2], name=f"mlp_{i}")
        outs.append(h)
    return jnp.stack(outs).astype(x.dtype)
```

```python
import functools
import math

import numpy as np
import jax
import jax.numpy as jnp
from jax import lax
from jax.experimental import pallas as pl
from jax.experimental.pallas import tpu as pltpu

F32 = jnp.float32
BF16 = jnp.bfloat16
EPS = 1e-6
ROPE_THETA = 10000.0
GRID_W = 64
NEG_BIG = -1e30
HI = lax.Precision.HIGHEST

NA_HEADS, NA_DH, NA_WIN_ROWS, NA_WIN_COLS, NA_Q_ROWS = 16, 64, 8, 16, 2
NA_KBLK = 5
GDN_HEADS, GDN_DK, GDN_CONV, GDN_CHUNK = 8, 128, 5, 64
GDN_GATE_LANES = 32
MLA_HEADS, MLA_Q_RANK, MLA_KV_RANK, MLA_NOPE, MLA_ROPE, MLA_V = 16, 768, 256, 64, 32, 64
MLA_VX = 80
RET_HEADS, RET_DK, RET_DV, RET_CHUNK = 8, 128, 256, 128

VMEM_LIMIT = 52 * 1024 * 1024

TM_DENSE = 512
TM_WIDE = 256
TM_OUT = 1024
TB_GDN = 256
TQ_MLA, TK_MLA = 2048, 512


def _cparams(*sem):
    return pltpu.CompilerParams(dimension_semantics=sem, vmem_limit_bytes=VMEM_LIMIT)


def _rms(x, g):
    return x * lax.rsqrt(jnp.mean(x * x, axis=-1, keepdims=True) + EPS) * g


def _sigmoid(x):
    return 1.0 / (1.0 + jnp.exp(-x))


def _dot(a, b, **kw):
    return jnp.dot(a, b, preferred_element_type=F32, **kw)


def _dot_nt(a, b, **kw):
    return lax.dot_general(a, b, (((1,), (1,)), ((), ())), preferred_element_type=F32, **kw)


def _dot_tn(a, b, **kw):
    return lax.dot_general(a, b, (((0,), (0,)), ((), ())), preferred_element_type=F32, **kw)


def _norm_matmul_body(x_ref, g_ref, w_ref, o_ref, xn_ref):
    @pl.when(pl.program_id(1) == 0)
    def _():
        xn_ref[...] = _rms(x_ref[...], g_ref[...]).astype(BF16)

    o_ref[...] = _dot(xn_ref[...], w_ref[...]).astype(o_ref.dtype)


def norm_matmul(x, g, w, out_dtype, tm, tn, name):
    s, d = x.shape
    n = w.shape[1]
    return pl.pallas_call(
        _norm_matmul_body,
        grid=(s // tm, n // tn),
        in_specs=[pl.BlockSpec((tm, d), lambda i, j: (i, 0)),
                  pl.BlockSpec((1, d), lambda i, j: (0, 0)),
                  pl.BlockSpec((d, tn), lambda i, j: (0, j))],
        out_specs=pl.BlockSpec((tm, tn), lambda i, j: (i, j)),
        out_shape=jax.ShapeDtypeStruct((s, n), out_dtype),
        scratch_shapes=[pltpu.VMEM((tm, d), BF16)],
        compiler_params=_cparams("arbitrary", "arbitrary"),
        name=name,
    )(x, g.reshape(1, d), w)


def _matmul_res_body(a_ref, w_ref, r_ref, o_ref):
    o_ref[...] = r_ref[...] + _dot(a_ref[...], w_ref[...])


def matmul_residual(a, w, res, tm, name):
    s, k = a.shape
    n = w.shape[1]
    return pl.pallas_call(
        _matmul_res_body,
        grid=(s // tm,),
        in_specs=[pl.BlockSpec((tm, k), lambda i: (i, 0)),
                  pl.BlockSpec((k, n), lambda i: (0, 0)),
                  pl.BlockSpec((tm, n), lambda i: (i, 0))],
        out_specs=pl.BlockSpec((tm, n), lambda i: (i, 0)),
        out_shape=jax.ShapeDtypeStruct((s, n), F32),
        compiler_params=_cparams("arbitrary"),
        name=name,
    )(a, w, res)


def _mlp_body(x_ref, g_ref, w1_ref, w2_ref, fg_ref, o_ref, xn_ref, acc_ref, *, nk, final):
    k = pl.program_id(1)

    @pl.when(k == 0)
    def _():
        xn_ref[...] = _rms(x_ref[...], g_ref[...]).astype(BF16)
        acc_ref[...] = jnp.zeros_like(acc_ref)

    a = jnp.maximum(_dot(xn_ref[...], w1_ref[...]), 0.0)
    acc_ref[...] += _dot((a * a).astype(BF16), w2_ref[...])

    @pl.when(k == nk - 1)
    def _():
        y = x_ref[...] + acc_ref[...]
        if final:
            y = _rms(y, fg_ref[...])
        o_ref[...] = y


def mlp_block(x, g, w1, w2, final_g, final, tm, tf, name):
    s, d = x.shape
    f = w1.shape[1]
    nk = f // tf
    return pl.pallas_call(
        functools.partial(_mlp_body, nk=nk, final=final),
        grid=(s // tm, nk),
        in_specs=[pl.BlockSpec((tm, d), lambda i, k: (i, 0)),
                  pl.BlockSpec((1, d), lambda i, k: (0, 0)),
                  pl.BlockSpec((d, tf), lambda i, k: (0, k)),
                  pl.BlockSpec((tf, d), lambda i, k: (k, 0)),
                  pl.BlockSpec((1, d), lambda i, k: (0, 0))],
        out_specs=pl.BlockSpec((tm, d), lambda i, k: (i, 0)),
        out_shape=jax.ShapeDtypeStruct((s, d), F32),
        scratch_shapes=[pltpu.VMEM((tm, d), BF16), pltpu.VMEM((tm, d), F32)],
        compiler_params=_cparams("arbitrary", "arbitrary"),
        name=name,
    )(x, g.reshape(1, d), w1, w2, final_g.reshape(1, d))


def _na_bias_table(rpb, rows):
    n_qblk = rows // NA_Q_ROWS
    reps = {4: 0, 3: 1, 2: 2, 1: n_qblk - 2, 0: n_qblk - 1}
    w, wc, wr = GRID_W, NA_WIN_COLS, NA_WIN_ROWS
    pcol = jnp.pad(rpb.astype(F32), ((0, 0), (0, 0), (w - wc, w - wc)), mode="edge")
    e = jnp.stack([pcol[:, :, w - 1 - qc:2 * w - 1 - qc] for qc in range(w)], axis=2)
    qc = np.arange(w)
    qc0 = np.clip(qc - wc // 2, 0, w - wc)
    col_ok = (qc[None, :] >= qc0[:, None]) & (qc[None, :] < qc0[:, None] + wc)
    tabs, oks = [], []
    for t in range(5):
        i = reps[t]
        base = int(np.clip(i - 2, 0, n_qblk - NA_KBLK))
        per_qr, ok_qr = [], []
        for qr in range(NA_Q_ROWS):
            qrow = NA_Q_ROWS * i + qr
            qr0 = int(np.clip(qrow - wr // 2, 0, rows - wr))
            krows = NA_Q_ROWS * base + np.arange(NA_KBLK * NA_Q_ROWS)
            drow = np.clip(krows - qrow + wr - 1, 0, 2 * wr - 2)
            row_ok = (krows >= qr0) & (krows < qr0 + wr)
            per_qr.append(jnp.concatenate([e[:, int(d)] for d in drow], axis=-1))
            ok_qr.append(np.concatenate([col_ok & bool(r) for r in row_ok], axis=-1))
        tabs.append(jnp.concatenate(per_qr, axis=1))
        oks.append(np.concatenate(ok_qr, axis=0))
    tab = jnp.where(np.stack(oks)[:, None], jnp.stack(tabs), NEG_BIG)
    return jnp.swapaxes(tab, 2, 3)


def _na_body(q_ref, k0, k1, k2, k3, k4, v0, v1, v2, v3, v4, b_ref, o_ref):
    krefs = (k0, k1, k2, k3, k4)
    vrefs = (v0, v1, v2, v3, v4)
    nkb = NA_Q_ROWS * GRID_W
    lane_k = lax.broadcasted_iota(jnp.int32, (NA_KBLK * nkb, 128), 1)
    ones_blk = jnp.where(lane_k == 0, 1.0, 0.0).astype(BF16)
    first = lax.broadcasted_iota(jnp.int32, (nkb, 128), 1) < NA_DH
    for pr in range(NA_HEADS // 2):
        sl = slice(pr * 128, (pr + 1) * 128)
        k_all = jnp.concatenate([kr[:, sl] for kr in krefs], axis=0)
        v_aug = jnp.concatenate([jnp.concatenate([vr[:, sl] for vr in vrefs], axis=0), ones_blk], axis=1)
        q2 = q_ref[:, sl]
        outs = []
        for hh in range(2):
            qh = jnp.where(first if hh == 0 else jnp.logical_not(first), q2, jnp.zeros_like(q2))
            s_t = _dot_nt(k_all, qh) + b_ref[0, 2 * pr + hh]
            m = jnp.max(s_t, axis=0, keepdims=True)
            p_t = jnp.exp(s_t - m).astype(BF16)
            o = _dot_tn(p_t, v_aug)
            outs.append(o[:, :128] / o[:, 128:129])
        o_ref[:, sl] = jnp.where(first, outs[0], outs[1]).astype(o_ref.dtype)


def na_attention(qkv, bias_tab):
    s = qkv.shape[0]
    d = NA_HEADS * NA_DH
    tq = NA_Q_ROWS * GRID_W
    n_qblk = s // tq
    top = n_qblk - NA_KBLK

    def base(i):
        return jnp.clip(i - 2, 0, top)

    kv_specs = [pl.BlockSpec((tq, d), functools.partial(lambda i, dd, c: (base(i) + dd, c), dd=dd, c=c))
                for c in (1, 2) for dd in range(NA_KBLK)]
    return pl.pallas_call(
        _na_body,
        grid=(n_qblk,),
        in_specs=[pl.BlockSpec((tq, d), lambda i: (i, 0))] + kv_specs
                 + [pl.BlockSpec((1, NA_HEADS, NA_KBLK * tq, tq), lambda i: (base(i) - i + 4, 0, 0, 0))],
        out_specs=pl.BlockSpec((tq, d), lambda i: (i, 0)),
        out_shape=jax.ShapeDtypeStruct((s, d), BF16),
        compiler_params=_cparams("arbitrary"),
        name="na_attention",
    )(qkv, *([qkv] * (2 * NA_KBLK)), bias_tab)


def _gdn_conv_body(x_ref, xp_ref, xn_ref, w_ref, o_ref, xe_ref, *, tm, nt):
    i = pl.program_id(0)
    j = pl.program_id(1)
    xe_ref[0:8, :] = jnp.where(i > 0, xp_ref[...], 0.0)
    xe_ref[8:8 + tm, :] = x_ref[...]
    xe_ref[8 + tm:16 + tm, :] = jnp.where(i < nt - 1, xn_ref[...], 0.0)
    w = w_ref[...]
    half = GDN_CONV // 2
    acc = xe_ref[8 - half:8 - half + tm, :] * w[0:1]
    for t in range(1, GDN_CONV):
        acc = acc + xe_ref[8 - half + t:8 - half + t + tm, :] * w[t:t + 1]
    y = acc * _sigmoid(acc)
    for h in range(GDN_HEADS):
        sl = slice(h * GDN_DK, (h + 1) * GDN_DK)
        yh = y[:, sl]
        nrm = lax.rsqrt(jnp.sum(yh * yh, axis=1, keepdims=True) + EPS)
        fac = jnp.where(j == 0, nrm * GDN_DK ** -0.5, jnp.where(j == 1, nrm, 1.0))
        o_ref[:, sl] = (yh * fac).astype(o_ref.dtype)


def gdn_conv(proj, conv_w_t, tm):
    s = proj.shape[0]
    d = GDN_HEADS * GDN_DK
    nt = s // tm
    r8 = tm // 8
    return pl.pallas_call(
        functools.partial(_gdn_conv_body, tm=tm, nt=nt),
        grid=(nt, 3),
        in_specs=[pl.BlockSpec((tm, d), lambda i, j: (i, j)),
                  pl.BlockSpec((8, d), lambda i, j: (jnp.maximum(i * r8 - 1, 0), j)),
                  pl.BlockSpec((8, d), lambda i, j: (jnp.minimum((i + 1) * r8, nt * r8 - 1), j)),
                  pl.BlockSpec((GDN_CONV, d), lambda i, j: (0, j))],
        out_specs=pl.BlockSpec((tm, d), lambda i, j: (i, j)),
        out_shape=jax.ShapeDtypeStruct((s, 3 * d), BF16),
        scratch_shapes=[pltpu.VMEM((tm + 16, d), F32)],
        compiler_params=_cparams("arbitrary", "arbitrary"),
        name="gdn_conv",
    )(proj, proj, proj, conv_w_t)


def _gdn_gates_body(gp_ref, alog_ref, dtb_ref, col_ref, row_ref, *, tm):
    c = GDN_CHUNK
    gp = gp_ref[...]
    lane = lax.broadcasted_iota(jnp.int32, (tm, 128), 1)
    is_f = (lane >= 16) & (lane < 24)
    is_b = (lane >= 24) & (lane < 32)
    beta = _sigmoid(gp)
    z = gp + dtb_ref[...]
    softplus = jnp.maximum(z, 0.0) + jnp.log1p(jnp.exp(-jnp.abs(z)))
    g = jnp.where(is_f | is_b, -jnp.exp(alog_ref[...]) * softplus, 0.0)
    r = lax.broadcasted_iota(jnp.int32, (c, c), 0)
    cc = lax.broadcasted_iota(jnp.int32, (c, c), 1)
    tri_lo = (r >= cc).astype(F32)
    tri_up = (r <= cc).astype(F32)
    lane_c = lax.broadcasted_iota(jnp.int32, (c, 128), 1)
    is_f_c = (lane_c >= 16) & (lane_c < 24)
    is_g_c = (lane_c >= 16) & (lane_c < 32)
    for ch in range(tm // c):
        gc = g[ch * c:(ch + 1) * c]
        cum = jnp.where(is_f_c, _dot(tri_lo, gc, precision=HI), _dot(tri_up, gc, precision=HI))
        colc = jnp.where(is_g_c, cum, beta[ch * c:(ch + 1) * c])
        col_ref[ch * c:(ch + 1) * c, :] = colc
        row_ref[ch] = colc.T[:GDN_GATE_LANES, :]


def gdn_gates(gp, gate_blk, alog_vec, dtb_vec, tm):
    s = gp.shape[0]
    c = GDN_CHUNK
    return pl.pallas_call(
        functools.partial(_gdn_gates_body, tm=tm),
        grid=(s // tm,),
        in_specs=[pl.BlockSpec((tm, 128), lambda i: (i, gate_blk)),
                  pl.BlockSpec((1, 128), lambda i: (0, 0)),
                  pl.BlockSpec((1, 128), lambda i: (0, 0))],
        out_specs=[pl.BlockSpec((tm, 128), lambda i: (i, 0)),
                   pl.BlockSpec((tm // c, GDN_GATE_LANES, c), lambda i: (i, 0, 0))],
        out_shape=[jax.ShapeDtypeStruct((s, 128), F32), jax.ShapeDtypeStruct((s // c, GDN_GATE_LANES, c), F32)],
        compiler_params=_cparams("arbitrary"),
        name="gdn_gates",
    )(gp, alog_vec, dtb_vec)


def _bmm(a, b):
    return jnp.einsum("nik,nkj->nij", a, b, preferred_element_type=F32)


def _bmm_nt(a, b):
    return jnp.einsum("nik,njk->nij", a, b, preferred_element_type=F32)


def _unit_tri_inverse(lmat, r, cc):
    eye = (r == cc).astype(F32)
    diag_blk = (r // 16) == (cc // 16)
    ld = jnp.where(diag_blk, lmat, 0.0)
    lo = lmat - ld
    x = eye - ld
    p = _bmm(ld, ld)
    x = x + _bmm(x, p)
    p = _bmm(p, p)
    x = x + _bmm(x, p)
    p = _bmm(p, p)
    dinv = x + _bmm(x, p)
    n1 = _bmm(dinv, lo)
    n2 = _bmm(n1, n1)
    y = eye - n1 + n2 - _bmm(n1, n2)
    return _bmm(y, dinv)


def _gdn_local_body(qkv_ref, col_ref, row_ref, *out_refs, tb):
    c, dk, hh = GDN_CHUNK, GDN_DK, GDN_HEADS
    nc = tb // c
    nb = hh * nc
    r = lax.broadcasted_iota(jnp.int32, (nb, c, c), 1)
    cc = lax.broadcasted_iota(jnp.int32, (nb, c, c), 2)
    col3 = col_ref[...].reshape(nc, c, 128)

    def heads(base):
        return jnp.concatenate([qkv_ref[:, base + h * dk:base + (h + 1) * dk].astype(F32).reshape(nc, c, dk)
                                for h in range(hh)], axis=0)

    qc, kc, vc = heads(0), heads(hh * dk), heads(2 * hh * dk)
    kc16 = kc.astype(BF16)
    qk = _bmm_nt(qc.astype(BF16), kc16)
    for d in range(2):
        reverse = d == 1
        u_ref, wq_ref, ak_ref, gl_ref = out_refs[4 * d:4 * d + 4]
        incl = (r <= cc) if reverse else (r >= cc)
        strict = (r < cc) if reverse else (r > cc)
        b0 = 8 if reverse else 0
        g0 = 24 if reverse else 16
        beta = jnp.concatenate([col3[:, :, b0 + h:b0 + h + 1] for h in range(hh)], axis=0)
        gcol = jnp.concatenate([col3[:, :, g0 + h:g0 + h + 1] for h in range(hh)], axis=0)
        grow = jnp.concatenate([row_ref[:, g0 + h:g0 + h + 1, :] for h in range(hh)], axis=0)
        g_last = gcol[:, 0:1, :] if reverse else gcol[:, c - 1:c, :]
        decay = jnp.where(incl, jnp.exp(jnp.where(incl, gcol - grow, 0.0)), 0.0)
        eg = jnp.exp(gcol)
        kb = kc * beta
        lmat = jnp.where(strict, _bmm_nt(kb.astype(BF16), kc16) * decay, 0.0)
        tinv = _unit_tri_inverse(lmat, r, cc)
        sol = _bmm(tinv, jnp.concatenate([vc * beta, kb * eg], axis=2))
        a_intra = jnp.where(incl, qk * decay, 0.0).astype(BF16)
        w16 = sol[:, :, dk:].astype(BF16)
        qd16 = (qc * eg).astype(BF16)
        kdt16 = jnp.swapaxes(kc * jnp.exp(g_last - gcol), 1, 2).astype(BF16)
        gl = jnp.broadcast_to(jnp.exp(g_last), (nb, 1, dk))
        for h in range(hh):
            sl = slice(h * dk, (h + 1) * dk)
            hb = slice(h * nc, (h + 1) * nc)
            u_ref[:, sl] = sol[hb, :, :dk].reshape(tb, dk)
            wq_ref[:, 0:c, sl] = w16[hb]
            wq_ref[:, c:2 * c, sl] = qd16[hb]
            ak_ref[:, 0:c, h * c:(h + 1) * c] = a_intra[hb]
            ak_ref[:, c:c + dk, h * c:(h + 1) * c] = kdt16[hb]
            gl_ref[:, h:h + 1, :] = gl[hb]


def gdn_local(qkv, gcol, grow, tb):
    s = qkv.shape[0]
    c, d = GDN_CHUNK, GDN_HEADS * GDN_DK
    nc = tb // c
    row = lambda w: pl.BlockSpec((tb, w), lambda i: (i, 0))
    per_dir_specs = [row(d),
                     pl.BlockSpec((nc, 2 * c, d), lambda i: (i, 0, 0)),
                     pl.BlockSpec((nc, c + GDN_DK, GDN_HEADS * c), lambda i: (i, 0, 0)),
                     pl.BlockSpec((nc, GDN_HEADS, GDN_DK), lambda i: (i, 0, 0))]
    per_dir_shapes = [jax.ShapeDtypeStruct((s, d), F32),
                      jax.ShapeDtypeStruct((s // c, 2 * c, d), BF16),
                      jax.ShapeDtypeStruct((s // c, c + GDN_DK, GDN_HEADS * c), BF16),
                      jax.ShapeDtypeStruct((s // c, GDN_HEADS, GDN_DK), F32)]
    return pl.pallas_call(
        functools.partial(_gdn_local_body, tb=tb),
        grid=(s // tb,),
        in_specs=[row(3 * d), row(128), pl.BlockSpec((nc, GDN_GATE_LANES, c), lambda i: (i, 0, 0))],
        out_specs=per_dir_specs * 2,
        out_shape=per_dir_shapes * 2,
        compiler_params=_cparams("arbitrary"),
        name="gdn_local",
    )(qkv, gcol, grow)


def _gdn_scan_body(*refs, tb):
    c, dk, hh = GDN_CHUNK, GDN_DK, GDN_HEADS
    nc = tb // c
    ins, (of_ref, ob_ref, st_ref) = refs[:8], refs[8:]

    @pl.when(pl.program_id(0) == 0)
    def _():
        st_ref[...] = jnp.zeros_like(st_ref)

    for step in range(nc):
        for d in range(2):
            u_ref, wq_ref, ak_ref, gl_ref = ins[4 * d:4 * d + 4]
            o_ref = (of_ref, ob_ref)[d]
            ch = nc - 1 - step if d == 1 else step
            rows = slice(ch * c, (ch + 1) * c)
            for h in range(hh):
                sl = slice(h * dk, (h + 1) * dk)
                state = st_ref[d, h]
                ws_qs = _dot(wq_ref[ch, :, sl], state.astype(BF16))
                v_new = u_ref[rows, sl] - ws_qs[:c]
                av_kv = _dot(ak_ref[ch, :, h * c:(h + 1) * c], v_new.astype(BF16))
                o_ref[rows, sl] = ws_qs[c:] + av_kv[:c]
                st_ref[d, h] = state * gl_ref[ch, h:h + 1, :] + av_kv[c:]


def gdn_scan(local_out, tb):
    s, d = local_out[0].shape
    c = GDN_CHUNK
    nb = s // tb
    nc = tb // c

    def dir_specs(blk):
        return [pl.BlockSpec((tb, d), lambda t: (blk(t), 0)),
                pl.BlockSpec((nc, 2 * c, d), lambda t: (blk(t), 0, 0)),
                pl.BlockSpec((nc, c + GDN_DK, GDN_HEADS * c), lambda t: (blk(t), 0, 0)),
                pl.BlockSpec((nc, GDN_HEADS, GDN_DK), lambda t: (blk(t), 0, 0))]

    fwd = lambda t: t
    bwd = lambda t: nb - 1 - t
    return pl.pallas_call(
        functools.partial(_gdn_scan_body, tb=tb),
        grid=(nb,),
        in_specs=dir_specs(fwd) + dir_specs(bwd),
        out_specs=[pl.BlockSpec((tb, d), lambda t: (t, 0)), pl.BlockSpec((tb, d), lambda t: (nb - 1 - t, 0))],
        out_shape=[jax.ShapeDtypeStruct((s, d), F32)] * 2,
        scratch_shapes=[pltpu.VMEM((2, GDN_HEADS, GDN_DK, GDN_DK), F32)],
        compiler_params=_cparams("arbitrary"),
        name="gdn_scan",
    )(*local_out)


def _gdn_post_body(of_ref, ob_ref, z_ref, on_ref, w_ref, r_ref, o_ref):
    on = on_ref[...]
    parts = []
    for h in range(GDN_HEADS):
        sl = slice(h * GDN_DK, (h + 1) * GDN_DK)
        o = of_ref[:, sl] + ob_ref[:, sl]
        z = z_ref[:, sl]
        parts.append((_rms(o, on) * (z * _sigmoid(z))).astype(BF16))
    y = jnp.concatenate(parts, axis=1)
    o_ref[...] = r_ref[...] + _dot(y, w_ref[...])


def gdn_post(o_f, o_b, proj, o_norm, w_o, res, tm):
    s, d = o_f.shape
    return pl.pallas_call(
        _gdn_post_body,
        grid=(s // tm,),
        in_specs=[pl.BlockSpec((tm, d), lambda i: (i, 0)),
                  pl.BlockSpec((tm, d), lambda i: (i, 0)),
                  pl.BlockSpec((tm, d), lambda i: (i, 3)),
                  pl.BlockSpec((1, GDN_DK), lambda i: (0, 0)),
                  pl.BlockSpec((d, d), lambda i: (0, 0)),
                  pl.BlockSpec((tm, d), lambda i: (i, 0))],
        out_specs=pl.BlockSpec((tm, d), lambda i: (i, 0)),
        out_shape=jax.ShapeDtypeStruct((s, d), F32),
        compiler_params=_cparams("arbitrary"),
        name="gdn_post",
    )(o_f, o_b, proj, o_norm.reshape(1, GDN_DK), w_o, res)


def _mla_pre_body(x_ref, g_ref, win_ref, qn_ref, kvn_ref, wuq_ref, wuk_ref, wuv_ref, vone_ref, ct_ref, sn_ref,
                  ctt_ref, snt_ref, q_ref, k_ref, v_ref):
    xn = _rms(x_ref[...], g_ref[...]).astype(BF16)
    proj = _dot(xn, win_ref[...])
    cq = _rms(proj[:, :MLA_Q_RANK], qn_ref[...]).astype(BF16)
    ckv = _rms(proj[:, MLA_Q_RANK:MLA_Q_RANK + MLA_KV_RANK], kvn_ref[...]).astype(BF16)
    pr = proj[:, MLA_Q_RANK + MLA_KV_RANK:]
    ct, sn = ct_ref[...], sn_ref[...]
    scale = (MLA_NOPE + MLA_ROPE) ** -0.5 * math.log2(math.e)
    ctt, snt = ctt_ref[...], snt_ref[...]
    q_t = _dot_nt(wuq_ref[...], cq)
    for h in range(MLA_HEADS):
        qh = q_t[h * 128:(h + 1) * 128, :]
        partner = jnp.concatenate([qh[64:, :], qh[:64, :]], axis=0)
        q_ref[h * 128:(h + 1) * 128, :] = ((qh * ctt + partner * snt) * scale).astype(BF16)
    kr = pr * ct + pltpu.roll(pr, 64, 1) * sn
    k = _dot(ckv, wuk_ref[...])
    for h in range(MLA_HEADS):
        sl = slice(h * 128, (h + 1) * 128)
        k_ref[:, sl] = (k[:, sl] + kr).astype(BF16)
    v_t = (_dot_nt(wuv_ref[...], ckv) + vone_ref[...]).astype(BF16)
    v_ref[...] = v_t.reshape(v_ref.shape)


def mla_pre(x, g, win_p, q_norm, kv_norm, wuq_t, wuk_p, wuv_t, vone, ct, sn, tm):
    s, d = x.shape
    hp = MLA_HEADS * 128
    npair = MLA_HEADS // 2
    full = lambda a: pl.BlockSpec(a.shape, lambda i: (0,) * a.ndim)
    g2, qn2, kvn2 = g.reshape(1, d), q_norm.reshape(1, -1), kv_norm.reshape(1, -1)
    ct_t, sn_t = jnp.transpose(ct), jnp.transpose(sn)
    return pl.pallas_call(
        _mla_pre_body,
        grid=(s // tm,),
        in_specs=[pl.BlockSpec((tm, d), lambda i: (i, 0)), full(g2), full(win_p), full(qn2), full(kvn2),
                  full(wuq_t), full(wuk_p), full(wuv_t), full(vone),
                  pl.BlockSpec((tm, 128), lambda i: (i, 0)), pl.BlockSpec((tm, 128), lambda i: (i, 0)),
                  pl.BlockSpec((128, tm), lambda i: (0, i)), pl.BlockSpec((128, tm), lambda i: (0, i))],
        out_specs=[pl.BlockSpec((hp, tm), lambda i: (0, i)), pl.BlockSpec((tm, hp), lambda i: (i, 0)),
                   pl.BlockSpec((npair, 1, 2 * MLA_VX, tm), lambda i: (0, i, 0, 0))],
        out_shape=[jax.ShapeDtypeStruct((hp, s), BF16), jax.ShapeDtypeStruct((s, hp), BF16),
                   jax.ShapeDtypeStruct((npair, s // tm, 2 * MLA_VX, tm), BF16)],
        compiler_params=_cparams("arbitrary"),
        name="mla_pre",
    )(x, g2, win_p, qn2, kvn2, wuq_t, wuk_p, wuv_t, vone, ct, sn, ct_t, sn_t)


def _mla_attn_body(qt_ref, k_ref, vt_ref, o_ref, acc_ref, s_ref, *, tq, tk, nk):
    acc_ref[...] = jnp.zeros_like(acc_ref)

    def scores(j, slot):
        off = pl.multiple_of(j * tk, tk)
        for hh in range(2):
            sl = slice(hh * 128, (hh + 1) * 128)
            s_ref[slot, hh] = _dot(k_ref[pl.ds(off, tk), sl], qt_ref[sl, :])

    def consume(j, slot, carry):
        new = []
        for hh in range(2):
            m_old = carry[hh]
            rows = slice(hh * MLA_VX, (hh + 1) * MLA_VX)
            s = s_ref[slot, hh]
            m_new = jnp.maximum(m_old, jnp.max(s, axis=0, keepdims=True))
            p = jnp.exp2(s - m_new).astype(BF16)
            acc_ref[rows, :] = acc_ref[rows, :] * jnp.exp2(m_old - m_new) + _dot(vt_ref[0, j, rows, :], p)
            new.append(m_new)
        return tuple(new)

    scores(0, 0)

    def body(jj, carry):
        j = 2 * jj
        scores(j + 1, 1)
        carry = consume(j, 0, carry)
        scores(jnp.minimum(j + 2, nk - 1), 0)
        return consume(j + 1, 1, carry)

    lax.fori_loop(0, nk // 2, body, (jnp.full((1, tq), -jnp.inf, F32),) * 2)
    o_t = jnp.concatenate([acc_ref[hh * MLA_VX:hh * MLA_VX + MLA_V, :]
                           / acc_ref[hh * MLA_VX + MLA_V:hh * MLA_VX + MLA_V + 1, :] for hh in range(2)], axis=0)
    o_ref[...] = o_t.T.astype(o_ref.dtype)


def mla_attention(q_t, k, v_t, tq):
    s = k.shape[0]
    npair, nk, _, tk = v_t.shape
    return pl.pallas_call(
        functools.partial(_mla_attn_body, tq=tq, tk=tk, nk=nk),
        grid=(npair, s // tq),
        in_specs=[pl.BlockSpec((256, tq), lambda p, i: (p, i)),
                  pl.BlockSpec((s, 256), lambda p, i: (0, p)),
                  pl.BlockSpec((1, nk, 2 * MLA_VX, tk), lambda p, i: (p, 0, 0, 0))],
        out_specs=pl.BlockSpec((tq, 128), lambda p, i: (i, p)),
        out_shape=jax.ShapeDtypeStruct((s, MLA_HEADS * MLA_V), BF16),
        scratch_shapes=[pltpu.VMEM((2 * MLA_VX, tq), F32), pltpu.VMEM((2, 2, tk, tq), F32)],
        compiler_params=_cparams("arbitrary", "arbitrary"),
        name="mla_attention",
    )(q_t, k, v_t)


def _ret_direction(q_ref, k_ref, v_ref, ct_ref, sn_ref, dm_ref, qd_ref, kd_ref, gc_ref, o_ref, st_ref):
    ct, sn = ct_ref[...], sn_ref[...]
    for h in range(RET_HEADS):
        sk = slice(h * RET_DK, (h + 1) * RET_DK)
        sv = slice(h * RET_DV, (h + 1) * RET_DV)
        qh = q_ref[:, sk]
        kh = k_ref[:, sk]
        qr = qh * ct + pltpu.roll(qh, 64, 1) * sn
        kr = (kh * ct + pltpu.roll(kh, 64, 1) * sn) * RET_DK ** -0.5
        v16 = v_ref[:, sv].astype(BF16)
        state = st_ref[h]
        inner = _dot_nt(qr.astype(BF16), kr.astype(BF16)) * dm_ref[h]
        o_ref[:, sv] = (_dot(inner.astype(BF16), v16)
                        + _dot((qr * qd_ref[h]).astype(BF16), state.astype(BF16)))
        st_ref[h] = state * gc_ref[h] + _dot_tn((kr * kd_ref[h]).astype(BF16), v16)


def _ret_scan_body(qf, kf, vf, ctf, snf, qb, kb, vb, ctb, snb,
                   dmf, qdf, kdf, gcf, dmb, qdb, kdb, gcb, of_ref, ob_ref, sf_ref, sb_ref):
    @pl.when(pl.program_id(0) == 0)
    def _():
        sf_ref[...] = jnp.zeros_like(sf_ref)
        sb_ref[...] = jnp.zeros_like(sb_ref)

    _ret_direction(qf, kf, vf, ctf, snf, dmf, qdf, kdf, gcf, of_ref, sf_ref)
    _ret_direction(qb, kb, vb, ctb, snb, dmb, qdb, kdb, gcb, ob_ref, sb_ref)


def ret_scan(proj, ct, sn, tabs_f, tabs_b):
    s = proj.shape[0]
    c = RET_CHUNK
    nb = s // c
    dq = RET_HEADS * RET_DK
    dv = RET_HEADS * RET_DV

    def dir_specs(blk):
        return [pl.BlockSpec((c, dq), lambda t: (blk(t), 0)),
                pl.BlockSpec((c, dq), lambda t: (blk(t), 1)),
                pl.BlockSpec((c, dv), lambda t: (blk(t), 1)),
                pl.BlockSpec((c, 128), lambda t: (blk(t), 0)),
                pl.BlockSpec((c, 128), lambda t: (blk(t), 0))]

    full = lambda a: pl.BlockSpec(a.shape, lambda t: (0,) * a.ndim)
    fwd = lambda t: t
    bwd = lambda t: nb - 1 - t
    return pl.pallas_call(
        _ret_scan_body,
        grid=(nb,),
        in_specs=dir_specs(fwd) + dir_specs(bwd) + [full(a) for a in tabs_f] + [full(a) for a in tabs_b],
        out_specs=[pl.BlockSpec((c, dv), lambda t: (t, 0)), pl.BlockSpec((c, dv), lambda t: (nb - 1 - t, 0))],
        out_shape=[jax.ShapeDtypeStruct((s, dv), F32)] * 2,
        scratch_shapes=[pltpu.VMEM((RET_HEADS, RET_DK, RET_DV), F32)] * 2,
        compiler_params=_cparams("arbitrary"),
        name="ret_scan",
    )(proj, proj, proj, ct, sn, proj, proj, proj, ct, sn, *tabs_f, *tabs_b)


def _ret_post_body(of_ref, ob_ref, gate_ref, gn_ref, w_ref, r_ref, o_ref):
    parts = []
    for h in range(RET_HEADS):
        sv = slice(h * RET_DV, (h + 1) * RET_DV)
        o = of_ref[:, sv] + ob_ref[:, sv]
        mu = jnp.mean(o, axis=1, keepdims=True)
        oc = o - mu
        var = jnp.mean(oc * oc, axis=1, keepdims=True)
        gate = gate_ref[:, sv]
        parts.append((gate * _sigmoid(gate) * (oc * lax.rsqrt(var + EPS) * gn_ref[:, sv])).astype(BF16))
    y = jnp.concatenate(parts, axis=1)
    o_ref[...] = r_ref[...] + _dot(y, w_ref[...])


def ret_post(o_f, o_b, proj, gn_w, w_o, res, tm):
    s, dv = o_f.shape
    d = res.shape[1]
    return pl.pallas_call(
        _ret_post_body,
        grid=(s // tm,),
        in_specs=[pl.BlockSpec((tm, dv), lambda i: (i, 0)),
                  pl.BlockSpec((tm, dv), lambda i: (i, 0)),
                  pl.BlockSpec((tm, dv), lambda i: (i, 2)),
                  pl.BlockSpec((1, dv), lambda i: (0, 0)),
                  pl.BlockSpec((dv, d), lambda i: (0, 0)),
                  pl.BlockSpec((tm, d), lambda i: (i, 0))],
        out_specs=pl.BlockSpec((tm, d), lambda i: (i, 0)),
        out_shape=jax.ShapeDtypeStruct((s, d), F32),
        compiler_params=_cparams("arbitrary"),
        name="ret_post",
    )(o_f, o_b, proj, gn_w.reshape(1, dv), w_o, res)


def _na_layer(h, norm, w_qkv, rpb, w_o):
    s = h.shape[0]
    d = NA_HEADS * NA_DH
    col_scale = jnp.where(jnp.arange(3 * d) < d, NA_DH ** -0.5, 1.0).astype(F32)
    w16 = (w_qkv * col_scale[None, :]).astype(BF16)
    qkv = norm_matmul(h, norm, w16, BF16, tm=TM_DENSE, tn=3 * d, name="na_qkv")
    o = na_attention(qkv, _na_bias_table(rpb, s // GRID_W))
    return matmul_residual(o, w_o.astype(BF16), h, tm=min(TM_OUT, s), name="na_out")


def _gdn_layer(h, norm, w_in, conv_w, a_log_f, a_log_b, dt_bias_f, dt_bias_b, o_norm, w_o):
    d = GDN_HEADS * GDN_DK
    n_main = 4 * d
    w_all = jnp.pad(w_in, ((0, 0), (0, 128 - 4 * GDN_HEADS))).astype(BF16)
    proj = norm_matmul(h, norm, w_all, F32, tm=TM_DENSE, tn=n_main + 128, name="gdn_in")
    z8 = jnp.zeros((2 * GDN_HEADS,), F32)
    pad = jnp.zeros((128 - 4 * GDN_HEADS,), F32)
    alog_vec = jnp.concatenate([z8, a_log_f.astype(F32), a_log_b.astype(F32), pad]).reshape(1, 128)
    dtb_vec = jnp.concatenate([z8, dt_bias_f.astype(F32), dt_bias_b.astype(F32), pad]).reshape(1, 128)
    gcol, grow = gdn_gates(proj, n_main // 128, alog_vec, dtb_vec, tm=TM_DENSE)
    qkv = gdn_conv(proj, jnp.transpose(conv_w).astype(F32), tm=TM_DENSE)
    o_f, o_b = gdn_scan(gdn_local(qkv, gcol, grow, tb=TB_GDN), tb=TB_GDN)
    return gdn_post(o_f, o_b, proj, o_norm, w_o.astype(BF16), h, tm=TM_DENSE)


def _mla_lane_maps():
    lane = np.arange(128)
    rope1 = lane < 16
    rope2 = (lane >= 64) & (lane < 80)
    nope_a = (lane >= 16) & (lane < 64)
    nope_b = (lane >= 80) & (lane < 96)
    q_dim = np.where(rope1, MLA_NOPE + lane, np.where(nope_a, lane - 16, np.where(rope2, lane + 16, lane - 32)))
    q_ok = lane < 96
    nope_dim = np.where(nope_a, lane - 16, lane - 32)
    nope_ok = nope_a | nope_b
    rope_dim = np.where(rope1, lane, lane - 64 + 16)
    rope_ok = rope1 | rope2
    return q_dim, q_ok, nope_dim, nope_ok, rope_dim, rope_ok


def _mla_layer(h, norm, w_in, q_norm, w_uq, kv_norm, w_ukv, w_o):
    s = h.shape[0]
    hh = MLA_HEADS
    dqk = MLA_NOPE + MLA_ROPE
    q_dim, q_ok, nope_dim, nope_ok, rope_dim, rope_ok = _mla_lane_maps()
    head = np.arange(hh)[:, None]
    q_cols = (head * dqk + np.where(q_ok, q_dim, 0)[None, :]).reshape(-1)
    wuq_t = jnp.transpose(jnp.where(np.tile(q_ok, hh)[None, :], w_uq[:, q_cols], 0.0)).astype(BF16)
    k_cols = (head * (MLA_NOPE + MLA_V) + np.where(nope_ok, nope_dim, 0)[None, :]).reshape(-1)
    wuk_p = jnp.where(np.tile(nope_ok, hh)[None, :], w_ukv[:, k_cols], 0.0).astype(BF16)
    vx = np.arange(MLA_VX)[None, :]
    v_cols = (head * (MLA_NOPE + MLA_V) + MLA_NOPE + np.minimum(vx, MLA_V - 1)).reshape(-1)
    v_ok = np.broadcast_to(vx < MLA_V, (hh, MLA_VX)).reshape(-1)
    wuv_t = jnp.where(v_ok[:, None], jnp.transpose(w_ukv[:, v_cols]), 0.0).astype(BF16)
    vone = jnp.asarray(np.broadcast_to(vx == MLA_V, (hh, MLA_VX)).reshape(-1, 1), F32)
    n_c = MLA_Q_RANK + MLA_KV_RANK
    w_rope = jnp.where(rope_ok[None, :], w_in[:, n_c + np.where(rope_ok, rope_dim, 0)], 0.0)
    win_p = jnp.concatenate([w_in[:, :n_c], w_rope], axis=1).astype(BF16)
    half = MLA_ROPE // 2
    inv = 1.0 / (ROPE_THETA ** (jnp.arange(half, dtype=F32) / half))
    ang = jnp.arange(s, dtype=jnp.int32).astype(F32)[:, None] * inv[None, :]
    cos, sin = jnp.cos(ang), jnp.sin(ang)
    ones = jnp.ones((s, 48), F32)
    zeros = jnp.zeros((s, 48), F32)
    ct = jnp.concatenate([cos, ones, cos, ones], axis=1)
    sn = jnp.concatenate([-sin, zeros, sin, zeros], axis=1)
    q_t, k, v_t = mla_pre(h, norm, win_p, q_norm, kv_norm, wuq_t, wuk_p, wuv_t, vone, ct, sn, tm=TK_MLA)
    o = mla_attention(q_t, k, v_t, tq=min(TQ_MLA, s))
    return matmul_residual(o, w_o.astype(BF16), h, tm=min(TM_OUT, s), name="mla_out")


def _ret_tables(reverse):
    c = RET_CHUNK
    lg = jnp.log1p(-jnp.exp2(-5.0 - jnp.arange(RET_HEADS, dtype=F32)))
    if reverse:
        lg = lg[::-1]
    idx = np.arange(c)
    diff = idx[:, None] - idx[None, :]
    if reverse:
        keep, expo = diff < 0, -diff
        q_pow, k_pow = c - idx, idx
    else:
        keep, expo = diff >= 0, diff
        q_pow, k_pow = idx + 1, c - 1 - idx
    dmask = jnp.where(keep, jnp.exp(jnp.where(keep, expo, 0).astype(F32)[None] * lg[:, None, None]), 0.0)
    q_dec = jnp.exp(jnp.asarray(q_pow, F32)[None, :] * lg[:, None])
    k_dec = jnp.exp(jnp.asarray(k_pow, F32)[None, :] * lg[:, None])
    g_chunk = jnp.exp(c * lg)
    bc = lambda t: jnp.broadcast_to(t[:, :, None], (RET_HEADS, c, RET_DK))
    return [dmask, bc(q_dec), bc(k_dec), jnp.broadcast_to(g_chunk[:, None, None], (RET_HEADS, 1, RET_DV))]


def _ret_layer(h, norm, w_in, gn_w, w_o):
    s = h.shape[0]
    proj = norm_matmul(h, norm, w_in.astype(BF16), F32, tm=TM_WIDE, tn=w_in.shape[1], name="ret_in")
    half = RET_DK // 2
    inv = 1.0 / (ROPE_THETA ** (jnp.arange(half, dtype=F32) / half))
    ang = jnp.arange(s, dtype=jnp.int32).astype(F32)[:, None] * inv[None, :]
    cos, sin = jnp.cos(ang), jnp.sin(ang)
    ct = jnp.concatenate([cos, cos], axis=1)
    sn = jnp.concatenate([-sin, sin], axis=1)
    o_f, o_b = ret_scan(proj, ct, sn, _ret_tables(False), _ret_tables(True))
    return ret_post(o_f, o_b, proj, gn_w, w_o.astype(BF16), h, tm=TM_DENSE)


def kernel(x, na_norm, na_w_qkv, na_rpb, na_w_o, gdn_norm, gdn_w_in, gdn_conv, gdn_a_log_f, gdn_a_log_b, gdn_dt_bias_f, gdn_dt_bias_b, gdn_o_norm, gdn_w_o, mla_norm, mla_w_in, mla_q_norm, mla_w_uq, mla_kv_norm, mla_w_ukv, mla_w_o, ret_norm, ret_w_in, ret_gn, ret_w_o, mlp_norm, mlp_w1, mlp_w2, final_norm):
    b, s, d = x.shape
    depth = mlp_norm.shape[0]
    outs = []
    for bi in range(b):
        h = x[bi].astype(F32)
        for i in range(depth):
            m, j = i % 4, i // 4
            if m == 0:
                h = _na_layer(h, na_norm[j], na_w_qkv[j], na_rpb[j], na_w_o[j])
            elif m == 1:
                h = _gdn_layer(h, gdn_norm[j], gdn_w_in[j], gdn_conv[j], gdn_a_log_f[j], gdn_a_log_b[j],
                               gdn_dt_bias_f[j], gdn_dt_bias_b[j], gdn_o_norm[j], gdn_w_o[j])
            elif m == 2:
                h = _mla_layer(h, mla_norm[j], mla_w_in[j], mla_q_norm[j], mla_w_uq[j], mla_kv_norm[j],
                               mla_w_ukv[j], mla_w_o[j])
            else:
                h = _ret_layer(h, ret_norm[j], ret_w_in[j], ret_gn[j], ret_w_o[j])
            h = mlp_block(h, mlp_norm[i], mlp_w1[i].astype(BF16), mlp_w2[i].astype(BF16), final_norm,
                          final=(i == depth - 1), tm=TM_DENSE, tf=mlp_w1.shape[2], name=f"mlp_{i}")
        outs.append(h)
    return jnp.stack(outs).astype(x.dtype)
```

```python
import functools
import math

import numpy as np
import jax
import jax.numpy as jnp
from jax import lax
from jax.experimental import pallas as pl
from jax.experimental.pallas import tpu as pltpu

F32 = jnp.float32
BF16 = jnp.bfloat16
EPS = 1e-6
ROPE_THETA = 10000.0
GRID_W = 64
NEG_BIG = -1e30
HI = lax.Precision.HIGHEST

NA_HEADS, NA_DH, NA_WIN_ROWS, NA_WIN_COLS, NA_Q_ROWS = 16, 64, 8, 16, 2
NA_KBLK = 5
GDN_HEADS, GDN_DK, GDN_CONV, GDN_CHUNK = 8, 128, 5, 64
GDN_GATE_LANES = 32
MLA_HEADS, MLA_Q_RANK, MLA_KV_RANK, MLA_NOPE, MLA_ROPE, MLA_V = 16, 768, 256, 64, 32, 64
MLA_VX = 80
RET_HEADS, RET_DK, RET_DV, RET_CHUNK = 8, 128, 256, 128

VMEM_LIMIT = 52 * 1024 * 1024

TM_DENSE = 512
TM_WIDE = 256
TM_OUT = 1024
TB_GDN = 256
TQ_MLA, TK_MLA = 2048, 512


def _cparams(*sem):
    return pltpu.CompilerParams(dimension_semantics=sem, vmem_limit_bytes=VMEM_LIMIT)


def _rms(x, g):
    return x * lax.rsqrt(jnp.mean(x * x, axis=-1, keepdims=True) + EPS) * g


def _sigmoid(x):
    return 1.0 / (1.0 + jnp.exp(-x))


def _dot(a, b, **kw):
    return jnp.dot(a, b, preferred_element_type=F32, **kw)


def _dot_nt(a, b, **kw):
    return lax.dot_general(a, b, (((1,), (1,)), ((), ())), preferred_element_type=F32, **kw)


def _dot_tn(a, b, **kw):
    return lax.dot_general(a, b, (((0,), (0,)), ((), ())), preferred_element_type=F32, **kw)


def _norm_matmul_body(x_ref, g_ref, w_ref, o_ref, xn_ref):
    @pl.when(pl.program_id(1) == 0)
    def _():
        xn_ref[...] = _rms(x_ref[...], g_ref[...]).astype(BF16)

    o_ref[...] = _dot(xn_ref[...], w_ref[...]).astype(o_ref.dtype)


def norm_matmul(x, g, w, out_dtype, tm, tn, name):
    s, d = x.shape
    n = w.shape[1]
    return pl.pallas_call(
        _norm_matmul_body,
        grid=(s // tm, n // tn),
        in_specs=[pl.BlockSpec((tm, d), lambda i, j: (i, 0)),
                  pl.BlockSpec((1, d), lambda i, j: (0, 0)),
                  pl.BlockSpec((d, tn), lambda i, j: (0, j))],
        out_specs=pl.BlockSpec((tm, tn), lambda i, j: (i, j)),
        out_shape=jax.ShapeDtypeStruct((s, n), out_dtype),
        scratch_shapes=[pltpu.VMEM((tm, d), BF16)],
        compiler_params=_cparams("arbitrary", "arbitrary"),
        name=name,
    )(x, g.reshape(1, d), w)


def _matmul_res_body(a_ref, w_ref, r_ref, o_ref):
    o_ref[...] = r_ref[...] + _dot(a_ref[...], w_ref[...])


def matmul_residual(a, w, res, tm, name):
    s, k = a.shape
    n = w.shape[1]
    return pl.pallas_call(
        _matmul_res_body,
        grid=(s // tm,),
        in_specs=[pl.BlockSpec((tm, k), lambda i: (i, 0)),
                  pl.BlockSpec((k, n), lambda i: (0, 0)),
                  pl.BlockSpec((tm, n), lambda i: (i, 0))],
        out_specs=pl.BlockSpec((tm, n), lambda i: (i, 0)),
        out_shape=jax.ShapeDtypeStruct((s, n), F32),
        compiler_params=_cparams("arbitrary"),
        name=name,
    )(a, w, res)


def _mlp_body(x_ref, g_ref, w1_ref, w2_ref, fg_ref, o_ref, xn_ref, acc_ref, *, nk, final):
    k = pl.program_id(1)

    @pl.when(k == 0)
    def _():
        xn_ref[...] = _rms(x_ref[...], g_ref[...]).astype(BF16)
        acc_ref[...] = jnp.zeros_like(acc_ref)

    a = jnp.maximum(_dot(xn_ref[...], w1_ref[...]), 0.0)
    acc_ref[...] += _dot((a * a).astype(BF16), w2_ref[...])

    @pl.when(k == nk - 1)
    def _():
        y = x_ref[...] + acc_ref[...]
        if final:
            y = _rms(y, fg_ref[...])
        o_ref[...] = y


def mlp_block(x, g, w1, w2, final_g, final, tm, tf, name):
    s, d = x.shape
    f = w1.shape[1]
    nk = f // tf
    return pl.pallas_call(
        functools.partial(_mlp_body, nk=nk, final=final),
        grid=(s // tm, nk),
        in_specs=[pl.BlockSpec((tm, d), lambda i, k: (i, 0)),
                  pl.BlockSpec((1, d), lambda i, k: (0, 0)),
                  pl.BlockSpec((d, tf), lambda i, k: (0, k)),
                  pl.BlockSpec((tf, d), lambda i, k: (k, 0)),
                  pl.BlockSpec((1, d), lambda i, k: (0, 0))],
        out_specs=pl.BlockSpec((tm, d), lambda i, k: (i, 0)),
        out_shape=jax.ShapeDtypeStruct((s, d), F32),
        scratch_shapes=[pltpu.VMEM((tm, d), BF16), pltpu.VMEM((tm, d), F32)],
        compiler_params=_cparams("arbitrary", "arbitrary"),
        name=name,
    )(x, g.reshape(1, d), w1, w2, final_g.reshape(1, d))


def _na_bias_table(rpb, rows):
    n_qblk = rows // NA_Q_ROWS
    reps = {4: 0, 3: 1, 2: 2, 1: n_qblk - 2, 0: n_qblk - 1}
    w, wc, wr = GRID_W, NA_WIN_COLS, NA_WIN_ROWS
    head_order = np.r_[0:NA_HEADS:2, 1:NA_HEADS:2]
    prev = jnp.flip(jnp.pad(rpb.astype(F32)[head_order], ((0, 0), (0, 0), (w - wc, w - wc)), mode="edge"), axis=2)
    e = jnp.stack([prev[:, :, w - 1 - kc:2 * w - 1 - kc] for kc in range(w)], axis=2)
    qc = np.arange(w)
    qc0 = np.clip(qc - wc // 2, 0, w - wc)
    col_ok = (qc[:, None] >= qc0[None, :]) & (qc[:, None] < qc0[None, :] + wc)
    tabs, oks = [], []
    for t in range(5):
        i = reps[t]
        base = int(np.clip(i - 2, 0, n_qblk - NA_KBLK))
        per_qr, ok_qr = [], []
        for qr in range(NA_Q_ROWS):
            qrow = NA_Q_ROWS * i + qr
            qr0 = int(np.clip(qrow - wr // 2, 0, rows - wr))
            krows = NA_Q_ROWS * base + np.arange(NA_KBLK * NA_Q_ROWS)
            drow = np.clip(krows - qrow + wr - 1, 0, 2 * wr - 2)
            row_ok = (krows >= qr0) & (krows < qr0 + wr)
            per_qr.append(jnp.concatenate([e[:, int(d)] for d in drow], axis=1))
            ok_qr.append(np.concatenate([col_ok & bool(r) for r in row_ok], axis=0))
        tabs.append(jnp.concatenate(per_qr, axis=2))
        oks.append(np.concatenate(ok_qr, axis=1))
    return jnp.where(np.stack(oks)[:, None], jnp.stack(tabs), NEG_BIG)


def _na_body(q_ref, k0, k1, k2, k3, k4, v0, v1, v2, v3, v4, b_ref, o_ref):
    krefs = (k0, k1, k2, k3, k4)
    vrefs = (v0, v1, v2, v3, v4)
    nkb = NA_Q_ROWS * GRID_W
    npair = NA_HEADS // 2
    pairs = [slice(pr * 128, (pr + 1) * 128) for pr in range(npair)]
    lane_k = lax.broadcasted_iota(jnp.int32, (NA_KBLK * nkb, 128), 1)
    ones_blk = jnp.where(lane_k == 0, 1.0, 0.0).astype(BF16)
    first = lax.broadcasted_iota(jnp.int32, (npair, nkb, 128), 2) < NA_DH
    k_all = jnp.stack([jnp.concatenate([kr[:, sl] for kr in krefs], axis=0) for sl in pairs], axis=0)
    v_aug = jnp.stack([jnp.concatenate([jnp.concatenate([vr[:, sl] for vr in vrefs], axis=0), ones_blk], axis=1)
                       for sl in pairs], axis=0)
    q2 = jnp.stack([q_ref[:, sl] for sl in pairs], axis=0)
    outs = []
    for hh in range(2):
        qh = jnp.where(first if hh == 0 else jnp.logical_not(first), q2, jnp.zeros_like(q2))
        s_t = _bmm_nt(k_all, qh) + b_ref[0, hh * npair:(hh + 1) * npair]
        m = jnp.max(s_t, axis=1, keepdims=True)
        p = jnp.swapaxes(jnp.exp(s_t - m), 1, 2).astype(BF16)
        o = _bmm(p, v_aug)
        outs.append(o[:, :, :128] / o[:, :, 128:129])
    res = jnp.where(first, outs[0], outs[1]).astype(o_ref.dtype)
    for pr, sl in enumerate(pairs):
        o_ref[:, sl] = res[pr]


def na_attention(qkv, bias_tab):
    s = qkv.shape[0]
    d = NA_HEADS * NA_DH
    tq = NA_Q_ROWS * GRID_W
    n_qblk = s // tq
    top = n_qblk - NA_KBLK

    def base(i):
        return jnp.clip(i - 2, 0, top)

    kv_specs = [pl.BlockSpec((tq, d), functools.partial(lambda i, dd, c: (base(i) + dd, c), dd=dd, c=c))
                for c in (1, 2) for dd in range(NA_KBLK)]
    return pl.pallas_call(
        _na_body,
        grid=(n_qblk,),
        in_specs=[pl.BlockSpec((tq, d), lambda i: (i, 0))] + kv_specs
                 + [pl.BlockSpec((1, NA_HEADS, NA_KBLK * tq, tq), lambda i: (base(i) - i + 4, 0, 0, 0))],
        out_specs=pl.BlockSpec((tq, d), lambda i: (i, 0)),
        out_shape=jax.ShapeDtypeStruct((s, d), BF16),
        compiler_params=_cparams("arbitrary"),
        name="na_attention",
    )(qkv, *([qkv] * (2 * NA_KBLK)), bias_tab)


def _gdn_conv_body(x_ref, xp_ref, xn_ref, w_ref, o_ref, xe_ref, *, tm, nt):
    i = pl.program_id(0)
    j = pl.program_id(1)
    xe_ref[0:8, :] = jnp.where(i > 0, xp_ref[...], 0.0)
    xe_ref[8:8 + tm, :] = x_ref[...]
    xe_ref[8 + tm:16 + tm, :] = jnp.where(i < nt - 1, xn_ref[...], 0.0)
    w = w_ref[...]
    half = GDN_CONV // 2
    acc = xe_ref[8 - half:8 - half + tm, :] * w[0:1]
    for t in range(1, GDN_CONV):
        acc = acc + xe_ref[8 - half + t:8 - half + t + tm, :] * w[t:t + 1]
    y = acc * _sigmoid(acc)
    for h in range(GDN_HEADS):
        sl = slice(h * GDN_DK, (h + 1) * GDN_DK)
        yh = y[:, sl]
        nrm = lax.rsqrt(jnp.sum(yh * yh, axis=1, keepdims=True) + EPS)
        fac = jnp.where(j == 0, nrm * GDN_DK ** -0.5, jnp.where(j == 1, nrm, 1.0))
        o_ref[:, sl] = (yh * fac).astype(o_ref.dtype)


def gdn_conv(proj, conv_w_t, tm):
    s = proj.shape[0]
    d = GDN_HEADS * GDN_DK
    nt = s // tm
    r8 = tm // 8
    return pl.pallas_call(
        functools.partial(_gdn_conv_body, tm=tm, nt=nt),
        grid=(nt, 3),
        in_specs=[pl.BlockSpec((tm, d), lambda i, j: (i, j)),
                  pl.BlockSpec((8, d), lambda i, j: (jnp.maximum(i * r8 - 1, 0), j)),
                  pl.BlockSpec((8, d), lambda i, j: (jnp.minimum((i + 1) * r8, nt * r8 - 1), j)),
                  pl.BlockSpec((GDN_CONV, d), lambda i, j: (0, j))],
        out_specs=pl.BlockSpec((tm, d), lambda i, j: (i, j)),
        out_shape=jax.ShapeDtypeStruct((s, 3 * d), BF16),
        scratch_shapes=[pltpu.VMEM((tm + 16, d), F32)],
        compiler_params=_cparams("arbitrary", "arbitrary"),
        name="gdn_conv",
    )(proj, proj, proj, conv_w_t)


def _gdn_gates_body(gp_ref, alog_ref, dtb_ref, col_ref, row_ref, *, tm):
    c = GDN_CHUNK
    gp = gp_ref[...]
    lane = lax.broadcasted_iota(jnp.int32, (tm, 128), 1)
    is_f = (lane >= 16) & (lane < 24)
    is_b = (lane >= 24) & (lane < 32)
    beta = _sigmoid(gp)
    z = gp + dtb_ref[...]
    softplus = jnp.maximum(z, 0.0) + jnp.log1p(jnp.exp(-jnp.abs(z)))
    g = jnp.where(is_f | is_b, -jnp.exp(alog_ref[...]) * softplus, 0.0)
    r = lax.broadcasted_iota(jnp.int32, (c, c), 0)
    cc = lax.broadcasted_iota(jnp.int32, (c, c), 1)
    tri_lo = (r >= cc).astype(F32)
    tri_up = (r <= cc).astype(F32)
    lane_c = lax.broadcasted_iota(jnp.int32, (c, 128), 1)
    is_f_c = (lane_c >= 16) & (lane_c < 24)
    is_g_c = (lane_c >= 16) & (lane_c < 32)
    for ch in range(tm // c):
        gc = g[ch * c:(ch + 1) * c]
        cum = jnp.where(is_f_c, _dot(tri_lo, gc, precision=HI), _dot(tri_up, gc, precision=HI))
        colc = jnp.where(is_g_c, cum, beta[ch * c:(ch + 1) * c])
        col_ref[ch * c:(ch + 1) * c, :] = colc
        row_ref[ch] = colc.T[:GDN_GATE_LANES, :]


def gdn_gates(gp, gate_blk, alog_vec, dtb_vec, tm):
    s = gp.shape[0]
    c = GDN_CHUNK
    return pl.pallas_call(
        functools.partial(_gdn_gates_body, tm=tm),
        grid=(s // tm,),
        in_specs=[pl.BlockSpec((tm, 128), lambda i: (i, gate_blk)),
                  pl.BlockSpec((1, 128), lambda i: (0, 0)),
                  pl.BlockSpec((1, 128), lambda i: (0, 0))],
        out_specs=[pl.BlockSpec((tm, 128), lambda i: (i, 0)),
                   pl.BlockSpec((tm // c, GDN_GATE_LANES, c), lambda i: (i, 0, 0))],
        out_shape=[jax.ShapeDtypeStruct((s, 128), F32), jax.ShapeDtypeStruct((s // c, GDN_GATE_LANES, c), F32)],
        compiler_params=_cparams("arbitrary"),
        name="gdn_gates",
    )(gp, alog_vec, dtb_vec)


def _bmm(a, b):
    return jnp.einsum("nik,nkj->nij", a, b, preferred_element_type=F32)


def _bmm_nt(a, b):
    return jnp.einsum("nik,njk->nij", a, b, preferred_element_type=F32)


def _unit_tri_inverse(lmat, r, cc):
    eye = (r == cc).astype(F32)
    diag_blk = (r // 16) == (cc // 16)
    ld = jnp.where(diag_blk, lmat, 0.0)
    lo = lmat - ld
    x = eye - ld
    p = _bmm(ld, ld)
    x = x + _bmm(x, p)
    p = _bmm(p, p)
    x = x + _bmm(x, p)
    p = _bmm(p, p)
    dinv = x + _bmm(x, p)
    n1 = _bmm(dinv, lo)
    n2 = _bmm(n1, n1)
    y = eye - n1 + n2 - _bmm(n1, n2)
    return _bmm(y, dinv)


def _gdn_local_body(qkv_ref, col_ref, row_ref, *out_refs, tb):
    c, dk, hh = GDN_CHUNK, GDN_DK, GDN_HEADS
    nc = tb // c
    nb = hh * nc
    r = lax.broadcasted_iota(jnp.int32, (nb, c, c), 1)
    cc = lax.broadcasted_iota(jnp.int32, (nb, c, c), 2)
    col3 = col_ref[...].reshape(nc, c, 128)

    def heads(base):
        return jnp.concatenate([qkv_ref[:, base + h * dk:base + (h + 1) * dk].astype(F32).reshape(nc, c, dk)
                                for h in range(hh)], axis=0)

    qc, kc, vc = heads(0), heads(hh * dk), heads(2 * hh * dk)
    kc16 = kc.astype(BF16)
    qk = _bmm_nt(qc.astype(BF16), kc16)
    for d in range(2):
        reverse = d == 1
        u_ref, wq_ref, ak_ref, gl_ref = out_refs[4 * d:4 * d + 4]
        incl = (r <= cc) if reverse else (r >= cc)
        strict = (r < cc) if reverse else (r > cc)
        b0 = 8 if reverse else 0
        g0 = 24 if reverse else 16
        beta = jnp.concatenate([col3[:, :, b0 + h:b0 + h + 1] for h in range(hh)], axis=0)
        gcol = jnp.concatenate([col3[:, :, g0 + h:g0 + h + 1] for h in range(hh)], axis=0)
        grow = jnp.concatenate([row_ref[:, g0 + h:g0 + h + 1, :] for h in range(hh)], axis=0)
        g_last = gcol[:, 0:1, :] if reverse else gcol[:, c - 1:c, :]
        decay = jnp.where(incl, jnp.exp(jnp.where(incl, gcol - grow, 0.0)), 0.0)
        eg = jnp.exp(gcol)
        kb = kc * beta
        lmat = jnp.where(strict, _bmm_nt(kb.astype(BF16), kc16) * decay, 0.0)
        tinv = _unit_tri_inverse(lmat, r, cc)
        sol = _bmm(tinv, jnp.concatenate([vc * beta, kb * eg], axis=2))
        a_intra = jnp.where(incl, qk * decay, 0.0).astype(BF16)
        w16 = sol[:, :, dk:].astype(BF16)
        qd16 = (qc * eg).astype(BF16)
        kdt16 = jnp.swapaxes(kc * jnp.exp(g_last - gcol), 1, 2).astype(BF16)
        gl = jnp.broadcast_to(jnp.exp(g_last), (nb, 1, dk))
        for h in range(hh):
            sl = slice(h * dk, (h + 1) * dk)
            hb = slice(h * nc, (h + 1) * nc)
            u_ref[:, sl] = sol[hb, :, :dk].reshape(tb, dk)
            wq_ref[:, 0:c, sl] = w16[hb]
            wq_ref[:, c:2 * c, sl] = qd16[hb]
            ak_ref[:, 0:c, h * c:(h + 1) * c] = a_intra[hb]
            ak_ref[:, c:c + dk, h * c:(h + 1) * c] = kdt16[hb]
            gl_ref[:, h:h + 1, :] = gl[hb]


def gdn_local(qkv, gcol, grow, tb):
    s = qkv.shape[0]
    c, d = GDN_CHUNK, GDN_HEADS * GDN_DK
    nc = tb // c
    row = lambda w: pl.BlockSpec((tb, w), lambda i: (i, 0))
    per_dir_specs = [row(d),
                     pl.BlockSpec((nc, 2 * c, d), lambda i: (i, 0, 0)),
                     pl.BlockSpec((nc, c + GDN_DK, GDN_HEADS * c), lambda i: (i, 0, 0)),
                     pl.BlockSpec((nc, GDN_HEADS, GDN_DK), lambda i: (i, 0, 0))]
    per_dir_shapes = [jax.ShapeDtypeStruct((s, d), F32),
                      jax.ShapeDtypeStruct((s // c, 2 * c, d), BF16),
                      jax.ShapeDtypeStruct((s // c, c + GDN_DK, GDN_HEADS * c), BF16),
                      jax.ShapeDtypeStruct((s // c, GDN_HEADS, GDN_DK), F32)]
    return pl.pallas_call(
        functools.partial(_gdn_local_body, tb=tb),
        grid=(s // tb,),
        in_specs=[row(3 * d), row(128), pl.BlockSpec((nc, GDN_GATE_LANES, c), lambda i: (i, 0, 0))],
        out_specs=per_dir_specs * 2,
        out_shape=per_dir_shapes * 2,
        compiler_params=_cparams("arbitrary"),
        name="gdn_local",
    )(qkv, gcol, grow)


def _gdn_scan_body(*refs, tb):
    c, dk, hh = GDN_CHUNK, GDN_DK, GDN_HEADS
    nc = tb // c
    ins, (of_ref, ob_ref, st_ref) = refs[:8], refs[8:]

    @pl.when(pl.program_id(0) == 0)
    def _():
        st_ref[...] = jnp.zeros_like(st_ref)

    chains = [(d, h) for d in range(2) for h in range(hh)]
    for step in range(nc):
        ch = (step, nc - 1 - step)

        def gather(idx, pick):
            return jnp.stack([pick(ins[4 * d + idx], ch[d], h) for d, h in chains], axis=0)

        u = gather(0, lambda r, cpos, h: r[cpos * c:(cpos + 1) * c, h * dk:(h + 1) * dk])
        wq = gather(1, lambda r, cpos, h: r[cpos, :, h * dk:(h + 1) * dk])
        ak = gather(2, lambda r, cpos, h: r[cpos, :, h * c:(h + 1) * c])
        gl = gather(3, lambda r, cpos, h: r[cpos, h:h + 1, :])
        state = st_ref[...]
        ws_qs = _bmm(wq, state.astype(BF16))
        v_new = u - ws_qs[:, :c]
        av_kv = _bmm(ak, v_new.astype(BF16))
        o = ws_qs[:, c:] + av_kv[:, :c]
        for n, (d, h) in enumerate(chains):
            (of_ref, ob_ref)[d][ch[d] * c:(ch[d] + 1) * c, h * dk:(h + 1) * dk] = o[n]
        st_ref[...] = state * gl + av_kv[:, c:]


def gdn_scan(local_out, tb):
    s, d = local_out[0].shape
    c = GDN_CHUNK
    nb = s // tb
    nc = tb // c

    def dir_specs(blk):
        return [pl.BlockSpec((tb, d), lambda t: (blk(t), 0)),
                pl.BlockSpec((nc, 2 * c, d), lambda t: (blk(t), 0, 0)),
                pl.BlockSpec((nc, c + GDN_DK, GDN_HEADS * c), lambda t: (blk(t), 0, 0)),
                pl.BlockSpec((nc, GDN_HEADS, GDN_DK), lambda t: (blk(t), 0, 0))]

    fwd = lambda t: t
    bwd = lambda t: nb - 1 - t
    return pl.pallas_call(
        functools.partial(_gdn_scan_body, tb=tb),
        grid=(nb,),
        in_specs=dir_specs(fwd) + dir_specs(bwd),
        out_specs=[pl.BlockSpec((tb, d), lambda t: (t, 0)), pl.BlockSpec((tb, d), lambda t: (nb - 1 - t, 0))],
        out_shape=[jax.ShapeDtypeStruct((s, d), F32)] * 2,
        scratch_shapes=[pltpu.VMEM((2 * GDN_HEADS, GDN_DK, GDN_DK), F32)],
        compiler_params=_cparams("arbitrary"),
        name="gdn_scan",
    )(*local_out)


def _gdn_post_body(of_ref, ob_ref, z_ref, on_ref, w_ref, r_ref, o_ref):
    on = on_ref[...]
    parts = []
    for h in range(GDN_HEADS):
        sl = slice(h * GDN_DK, (h + 1) * GDN_DK)
        o = of_ref[:, sl] + ob_ref[:, sl]
        z = z_ref[:, sl]
        parts.append((_rms(o, on) * (z * _sigmoid(z))).astype(BF16))
    y = jnp.concatenate(parts, axis=1)
    o_ref[...] = r_ref[...] + _dot(y, w_ref[...])


def gdn_post(o_f, o_b, proj, o_norm, w_o, res, tm):
    s, d = o_f.shape
    return pl.pallas_call(
        _gdn_post_body,
        grid=(s // tm,),
        in_specs=[pl.BlockSpec((tm, d), lambda i: (i, 0)),
                  pl.BlockSpec((tm, d), lambda i: (i, 0)),
                  pl.BlockSpec((tm, d), lambda i: (i, 3)),
                  pl.BlockSpec((1, GDN_DK), lambda i: (0, 0)),
                  pl.BlockSpec((d, d), lambda i: (0, 0)),
                  pl.BlockSpec((tm, d), lambda i: (i, 0))],
        out_specs=pl.BlockSpec((tm, d), lambda i: (i, 0)),
        out_shape=jax.ShapeDtypeStruct((s, d), F32),
        compiler_params=_cparams("arbitrary"),
        name="gdn_post",
    )(o_f, o_b, proj, o_norm.reshape(1, GDN_DK), w_o, res)


def _mla_pre_body(x_ref, g_ref, win_ref, qn_ref, kvn_ref, wuq_ref, wuk_ref, wuv_ref, vone_ref, ct_ref, sn_ref,
                  ctt_ref, snt_ref, q_ref, k_ref, v_ref):
    xn = _rms(x_ref[...], g_ref[...]).astype(BF16)
    proj = _dot(xn, win_ref[...])
    cq = _rms(proj[:, :MLA_Q_RANK], qn_ref[...]).astype(BF16)
    ckv = _rms(proj[:, MLA_Q_RANK:MLA_Q_RANK + MLA_KV_RANK], kvn_ref[...]).astype(BF16)
    pr = proj[:, MLA_Q_RANK + MLA_KV_RANK:]
    ct, sn = ct_ref[...], sn_ref[...]
    scale = (MLA_NOPE + MLA_ROPE) ** -0.5 * math.log2(math.e)
    ctt, snt = ctt_ref[...], snt_ref[...]
    q_t = _dot_nt(wuq_ref[...], cq)
    for h in range(MLA_HEADS):
        qh = q_t[h * 128:(h + 1) * 128, :]
        partner = jnp.concatenate([qh[64:, :], qh[:64, :]], axis=0)
        q_ref[h * 128:(h + 1) * 128, :] = ((qh * ctt + partner * snt) * scale).astype(BF16)
    kr = pr * ct + pltpu.roll(pr, 64, 1) * sn
    k = _dot(ckv, wuk_ref[...])
    for h in range(MLA_HEADS):
        sl = slice(h * 128, (h + 1) * 128)
        k_ref[:, sl] = (k[:, sl] + kr).astype(BF16)
    v_t = (_dot_nt(wuv_ref[...], ckv) + vone_ref[...]).astype(BF16)
    v_ref[...] = v_t.reshape(v_ref.shape)


def mla_pre(x, g, win_p, q_norm, kv_norm, wuq_t, wuk_p, wuv_t, vone, ct, sn, tm):
    s, d = x.shape
    hp = MLA_HEADS * 128
    npair = MLA_HEADS // 2
    full = lambda a: pl.BlockSpec(a.shape, lambda i: (0,) * a.ndim)
    g2, qn2, kvn2 = g.reshape(1, d), q_norm.reshape(1, -1), kv_norm.reshape(1, -1)
    ct_t, sn_t = jnp.transpose(ct), jnp.transpose(sn)
    return pl.pallas_call(
        _mla_pre_body,
        grid=(s // tm,),
        in_specs=[pl.BlockSpec((tm, d), lambda i: (i, 0)), full(g2), full(win_p), full(qn2), full(kvn2),
                  full(wuq_t), full(wuk_p), full(wuv_t), full(vone),
                  pl.BlockSpec((tm, 128), lambda i: (i, 0)), pl.BlockSpec((tm, 128), lambda i: (i, 0)),
                  pl.BlockSpec((128, tm), lambda i: (0, i)), pl.BlockSpec((128, tm), lambda i: (0, i))],
        out_specs=[pl.BlockSpec((hp, tm), lambda i: (0, i)), pl.BlockSpec((tm, hp), lambda i: (i, 0)),
                   pl.BlockSpec((npair, 1, 2 * MLA_VX, tm), lambda i: (0, i, 0, 0))],
        out_shape=[jax.ShapeDtypeStruct((hp, s), BF16), jax.ShapeDtypeStruct((s, hp), BF16),
                   jax.ShapeDtypeStruct((npair, s // tm, 2 * MLA_VX, tm), BF16)],
        compiler_params=_cparams("arbitrary"),
        name="mla_pre",
    )(x, g2, win_p, qn2, kvn2, wuq_t, wuk_p, wuv_t, vone, ct, sn, ct_t, sn_t)


def _mla_attn_body(qt_ref, k_ref, vt_ref, o_ref, acc_ref, s_ref, *, tq, tk, nk):
    acc_ref[...] = jnp.zeros_like(acc_ref)

    def scores(j, slot):
        off = pl.multiple_of(j * tk, tk)
        for hh in range(2):
            sl = slice(hh * 128, (hh + 1) * 128)
            s_ref[slot, hh] = _dot(k_ref[pl.ds(off, tk), sl], qt_ref[sl, :])

    def consume(j, slot, carry):
        new = []
        for hh in range(2):
            m_old = carry[hh]
            rows = slice(hh * MLA_VX, (hh + 1) * MLA_VX)
            s = s_ref[slot, hh]
            m_new = jnp.maximum(m_old, jnp.max(s, axis=0, keepdims=True))
            p = jnp.exp2(s - m_new).astype(BF16)
            acc_ref[rows, :] = acc_ref[rows, :] * jnp.exp2(m_old - m_new) + _dot(vt_ref[0, j, rows, :], p)
            new.append(m_new)
        return tuple(new)

    scores(0, 0)

    def body(jj, carry):
        j = 2 * jj
        scores(j + 1, 1)
        carry = consume(j, 0, carry)
        scores(jnp.minimum(j + 2, nk - 1), 0)
        return consume(j + 1, 1, carry)

    lax.fori_loop(0, nk // 2, body, (jnp.full((1, tq), -jnp.inf, F32),) * 2)
    o_t = jnp.concatenate([acc_ref[hh * MLA_VX:hh * MLA_VX + MLA_V, :]
                           / acc_ref[hh * MLA_VX + MLA_V:hh * MLA_VX + MLA_V + 1, :] for hh in range(2)], axis=0)
    o_ref[...] = o_t.T.astype(o_ref.dtype)


def mla_attention(q_t, k, v_t, tq):
    s = k.shape[0]
    npair, nk, _, tk = v_t.shape
    return pl.pallas_call(
        functools.partial(_mla_attn_body, tq=tq, tk=tk, nk=nk),
        grid=(npair, s // tq),
        in_specs=[pl.BlockSpec((256, tq), lambda p, i: (p, i)),
                  pl.BlockSpec((s, 256), lambda p, i: (0, p)),
                  pl.BlockSpec((1, nk, 2 * MLA_VX, tk), lambda p, i: (p, 0, 0, 0))],
        out_specs=pl.BlockSpec((tq, 128), lambda p, i: (i, p)),
        out_shape=jax.ShapeDtypeStruct((s, MLA_HEADS * MLA_V), BF16),
        scratch_shapes=[pltpu.VMEM((2 * MLA_VX, tq), F32), pltpu.VMEM((2, 2, tk, tq), F32)],
        compiler_params=_cparams("arbitrary", "arbitrary"),
        name="mla_attention",
    )(q_t, k, v_t)


def _ret_direction(q_ref, k_ref, v_ref, ct_ref, sn_ref, dm_ref, qd_ref, kd_ref, gc_ref, o_ref, st_ref):
    ct, sn = ct_ref[...], sn_ref[...]

    def heads(ref, width):
        return jnp.stack([ref[:, h * width:(h + 1) * width] for h in range(RET_HEADS)], axis=0)

    q, k = heads(q_ref, RET_DK), heads(k_ref, RET_DK)
    v16 = heads(v_ref, RET_DV).astype(BF16)
    qr = q * ct + pltpu.roll(q, 64, 2) * sn
    kr = (k * ct + pltpu.roll(k, 64, 2) * sn) * RET_DK ** -0.5
    state = st_ref[...]
    inner = _bmm_nt(qr.astype(BF16), kr.astype(BF16)) * dm_ref[...]
    o = _bmm(inner.astype(BF16), v16) + _bmm((qr * qd_ref[...]).astype(BF16), state.astype(BF16))
    for h in range(RET_HEADS):
        o_ref[:, h * RET_DV:(h + 1) * RET_DV] = o[h]
    kd_t = jnp.swapaxes(kr * kd_ref[...], 1, 2).astype(BF16)
    st_ref[...] = state * gc_ref[...] + _bmm(kd_t, v16)


def _ret_scan_body(qf, kf, vf, ctf, snf, qb, kb, vb, ctb, snb,
                   dmf, qdf, kdf, gcf, dmb, qdb, kdb, gcb, of_ref, ob_ref, sf_ref, sb_ref):
    @pl.when(pl.program_id(0) == 0)
    def _():
        sf_ref[...] = jnp.zeros_like(sf_ref)
        sb_ref[...] = jnp.zeros_like(sb_ref)

    _ret_direction(qf, kf, vf, ctf, snf, dmf, qdf, kdf, gcf, of_ref, sf_ref)
    _ret_direction(qb, kb, vb, ctb, snb, dmb, qdb, kdb, gcb, ob_ref, sb_ref)


def ret_scan(proj, ct, sn, tabs_f, tabs_b):
    s = proj.shape[0]
    c = RET_CHUNK
    nb = s // c
    dq = RET_HEADS * RET_DK
    dv = RET_HEADS * RET_DV

    def dir_specs(blk):
        return [pl.BlockSpec((c, dq), lambda t: (blk(t), 0)),
                pl.BlockSpec((c, dq), lambda t: (blk(t), 1)),
                pl.BlockSpec((c, dv), lambda t: (blk(t), 1)),
                pl.BlockSpec((c, 128), lambda t: (blk(t), 0)),
                pl.BlockSpec((c, 128), lambda t: (blk(t), 0))]

    full = lambda a: pl.BlockSpec(a.shape, lambda t: (0,) * a.ndim)
    fwd = lambda t: t
    bwd = lambda t: nb - 1 - t
    return pl.pallas_call(
        _ret_scan_body,
        grid=(nb,),
        in_specs=dir_specs(fwd) + dir_specs(bwd) + [full(a) for a in tabs_f] + [full(a) for a in tabs_b],
        out_specs=[pl.BlockSpec((c, dv), lambda t: (t, 0)), pl.BlockSpec((c, dv), lambda t: (nb - 1 - t, 0))],
        out_shape=[jax.ShapeDtypeStruct((s, dv), F32)] * 2,
        scratch_shapes=[pltpu.VMEM((RET_HEADS, RET_DK, RET_DV), F32)] * 2,
        compiler_params=_cparams("arbitrary"),
        name="ret_scan",
    )(proj, proj, proj, ct, sn, proj, proj, proj, ct, sn, *tabs_f, *tabs_b)


def _ret_post_body(of_ref, ob_ref, gate_ref, gn_ref, w_ref, r_ref, o_ref):
    parts = []
    for h in range(RET_HEADS):
        sv = slice(h * RET_DV, (h + 1) * RET_DV)
        o = of_ref[:, sv] + ob_ref[:, sv]
        mu = jnp.mean(o, axis=1, keepdims=True)
        oc = o - mu
        var = jnp.mean(oc * oc, axis=1, keepdims=True)
        gate = gate_ref[:, sv]
        parts.append((gate * _sigmoid(gate) * (oc * lax.rsqrt(var + EPS) * gn_ref[:, sv])).astype(BF16))
    y = jnp.concatenate(parts, axis=1)
    o_ref[...] = r_ref[...] + _dot(y, w_ref[...])


def ret_post(o_f, o_b, proj, gn_w, w_o, res, tm):
    s, dv = o_f.shape
    d = res.shape[1]
    return pl.pallas_call(
        _ret_post_body,
        grid=(s // tm,),
        in_specs=[pl.BlockSpec((tm, dv), lambda i: (i, 0)),
                  pl.BlockSpec((tm, dv), lambda i: (i, 0)),
                  pl.BlockSpec((tm, dv), lambda i: (i, 2)),
                  pl.BlockSpec((1, dv), lambda i: (0, 0)),
                  pl.BlockSpec((dv, d), lambda i: (0, 0)),
                  pl.BlockSpec((tm, d), lambda i: (i, 0))],
        out_specs=pl.BlockSpec((tm, d), lambda i: (i, 0)),
        out_shape=jax.ShapeDtypeStruct((s, d), F32),
        compiler_params=_cparams("arbitrary"),
        name="ret_post",
    )(o_f, o_b, proj, gn_w.reshape(1, dv), w_o, res)


def _na_layer(h, norm, w_qkv, rpb, w_o):
    s = h.shape[0]
    d = NA_HEADS * NA_DH
    col_scale = jnp.where(jnp.arange(3 * d) < d, NA_DH ** -0.5, 1.0).astype(F32)
    w16 = (w_qkv * col_scale[None, :]).astype(BF16)
    qkv = norm_matmul(h, norm, w16, BF16, tm=TM_DENSE, tn=3 * d, name="na_qkv")
    o = na_attention(qkv, _na_bias_table(rpb, s // GRID_W))
    return matmul_residual(o, w_o.astype(BF16), h, tm=min(TM_OUT, s), name="na_out")


def _gdn_layer(h, norm, w_in, conv_w, a_log_f, a_log_b, dt_bias_f, dt_bias_b, o_norm, w_o):
    d = GDN_HEADS * GDN_DK
    n_main = 4 * d
    w_all = jnp.pad(w_in, ((0, 0), (0, 128 - 4 * GDN_HEADS))).astype(BF16)
    proj = norm_matmul(h, norm, w_all, F32, tm=TM_DENSE, tn=n_main + 128, name="gdn_in")
    z8 = jnp.zeros((2 * GDN_HEADS,), F32)
    pad = jnp.zeros((128 - 4 * GDN_HEADS,), F32)
    alog_vec = jnp.concatenate([z8, a_log_f.astype(F32), a_log_b.astype(F32), pad]).reshape(1, 128)
    dtb_vec = jnp.concatenate([z8, dt_bias_f.astype(F32), dt_bias_b.astype(F32), pad]).reshape(1, 128)
    gcol, grow = gdn_gates(proj, n_main // 128, alog_vec, dtb_vec, tm=TM_DENSE)
    qkv = gdn_conv(proj, jnp.transpose(conv_w).astype(F32), tm=TM_DENSE)
    o_f, o_b = gdn_scan(gdn_local(qkv, gcol, grow, tb=TB_GDN), tb=TB_GDN)
    return gdn_post(o_f, o_b, proj, o_norm, w_o.astype(BF16), h, tm=TM_DENSE)


def _mla_lane_maps():
    lane = np.arange(128)
    rope1 = lane < 16
    rope2 = (lane >= 64) & (lane < 80)
    nope_a = (lane >= 16) & (lane < 64)
    nope_b = (lane >= 80) & (lane < 96)
    q_dim = np.where(rope1, MLA_NOPE + lane, np.where(nope_a, lane - 16, np.where(rope2, lane + 16, lane - 32)))
    q_ok = lane < 96
    nope_dim = np.where(nope_a, lane - 16, lane - 32)
    nope_ok = nope_a | nope_b
    rope_dim = np.where(rope1, lane, lane - 64 + 16)
    rope_ok = rope1 | rope2
    return q_dim, q_ok, nope_dim, nope_ok, rope_dim, rope_ok


def _mla_layer(h, norm, w_in, q_norm, w_uq, kv_norm, w_ukv, w_o):
    s = h.shape[0]
    hh = MLA_HEADS
    dqk = MLA_NOPE + MLA_ROPE
    q_dim, q_ok, nope_dim, nope_ok, rope_dim, rope_ok = _mla_lane_maps()
    head = np.arange(hh)[:, None]
    q_cols = (head * dqk + np.where(q_ok, q_dim, 0)[None, :]).reshape(-1)
    wuq_t = jnp.transpose(jnp.where(np.tile(q_ok, hh)[None, :], w_uq[:, q_cols], 0.0)).astype(BF16)
    k_cols = (head * (MLA_NOPE + MLA_V) + np.where(nope_ok, nope_dim, 0)[None, :]).reshape(-1)
    wuk_p = jnp.where(np.tile(nope_ok, hh)[None, :], w_ukv[:, k_cols], 0.0).astype(BF16)
    vx = np.arange(MLA_VX)[None, :]
    v_cols = (head * (MLA_NOPE + MLA_V) + MLA_NOPE + np.minimum(vx, MLA_V - 1)).reshape(-1)
    v_ok = np.broadcast_to(vx < MLA_V, (hh, MLA_VX)).reshape(-1)
    wuv_t = jnp.where(v_ok[:, None], jnp.transpose(w_ukv[:, v_cols]), 0.0).astype(BF16)
    vone = jnp.asarray(np.broadcast_to(vx == MLA_V, (hh, MLA_VX)).reshape(-1, 1), F32)
    n_c = MLA_Q_RANK + MLA_KV_RANK
    w_rope = jnp.where(rope_ok[None, :], w_in[:, n_c + np.where(rope_ok, rope_dim, 0)], 0.0)
    win_p = jnp.concatenate([w_in[:, :n_c], w_rope], axis=1).astype(BF16)
    half = MLA_ROPE // 2
    inv = 1.0 / (ROPE_THETA ** (jnp.arange(half, dtype=F32) / half))
    ang = jnp.arange(s, dtype=jnp.int32).astype(F32)[:, None] * inv[None, :]
    cos, sin = jnp.cos(ang), jnp.sin(ang)
    ones = jnp.ones((s, 48), F32)
    zeros = jnp.zeros((s, 48), F32)
    ct = jnp.concatenate([cos, ones, cos, ones], axis=1)
    sn = jnp.concatenate([-sin, zeros, sin, zeros], axis=1)
    q_t, k, v_t = mla_pre(h, norm, win_p, q_norm, kv_norm, wuq_t, wuk_p, wuv_t, vone, ct, sn, tm=TK_MLA)
    o = mla_attention(q_t, k, v_t, tq=min(TQ_MLA, s))
    return matmul_residual(o, w_o.astype(BF16), h, tm=min(TM_OUT, s), name="mla_out")


def _ret_tables(reverse):
    c = RET_CHUNK
    lg = jnp.log1p(-jnp.exp2(-5.0 - jnp.arange(RET_HEADS, dtype=F32)))
    if reverse:
        lg = lg[::-1]
    idx = np.arange(c)
    diff = idx[:, None] - idx[None, :]
    if reverse:
        keep, expo = diff < 0, -diff
        q_pow, k_pow = c - idx, idx
    else:
        keep, expo = diff >= 0, diff
        q_pow, k_pow = idx + 1, c - 1 - idx
    dmask = jnp.where(keep, jnp.exp(jnp.where(keep, expo, 0).astype(F32)[None] * lg[:, None, None]), 0.0)
    q_dec = jnp.exp(jnp.asarray(q_pow, F32)[None, :] * lg[:, None])
    k_dec = jnp.exp(jnp.asarray(k_pow, F32)[None, :] * lg[:, None])
    g_chunk = jnp.exp(c * lg)
    bc = lambda t: jnp.broadcast_to(t[:, :, None], (RET_HEADS, c, RET_DK))
    return [dmask, bc(q_dec), bc(k_dec), jnp.broadcast_to(g_chunk[:, None, None], (RET_HEADS, 1, RET_DV))]


def _ret_layer(h, norm, w_in, gn_w, w_o):
    s = h.shape[0]
    proj = norm_matmul(h, norm, w_in.astype(BF16), F32, tm=TM_WIDE, tn=w_in.shape[1], name="ret_in")
    half = RET_DK // 2
    inv = 1.0 / (ROPE_THETA ** (jnp.arange(half, dtype=F32) / half))
    ang = jnp.arange(s, dtype=jnp.int32).astype(F32)[:, None] * inv[None, :]
    cos, sin = jnp.cos(ang), jnp.sin(ang)
    ct = jnp.concatenate([cos, cos], axis=1)
    sn = jnp.concatenate([-sin, sin], axis=1)
    o_f, o_b = ret_scan(proj, ct, sn, _ret_tables(False), _ret_tables(True))
    return ret_post(o_f, o_b, proj, gn_w, w_o.astype(BF16), h, tm=TM_DENSE)


def kernel(x, na_norm, na_w_qkv, na_rpb, na_w_o, gdn_norm, gdn_w_in, gdn_conv, gdn_a_log_f, gdn_a_log_b, gdn_dt_bias_f, gdn_dt_bias_b, gdn_o_norm, gdn_w_o, mla_norm, mla_w_in, mla_q_norm, mla_w_uq, mla_kv_norm, mla_w_ukv, mla_w_o, ret_norm, ret_w_in, ret_gn, ret_w_o, mlp_norm, mlp_w1, mlp_w2, final_norm):
    b, s, d = x.shape
    depth = mlp_norm.shape[0]
    outs = []
    for bi in range(b):
        h = x[bi].astype(F32)
        for i in range(depth):
            m, j = i % 4, i // 4
            if m == 0:
                h = _na_layer(h, na_norm[j], na_w_qkv[j], na_rpb[j], na_w_o[j])
            elif m == 1:
                h = _gdn_layer(h, gdn_norm[j], gdn_w_in[j], gdn_conv[j], gdn_a_log_f[j], gdn_a_log_b[j],
                               gdn_dt_bias_f[j], gdn_dt_bias_b[j], gdn_o_norm[j], gdn_w_o[j])
            elif m == 2:
                h = _mla_layer(h, mla_norm[j], mla_w_in[j], mla_q_norm[j], mla_w_uq[j], mla_kv_norm[j],
                               mla_w_ukv[j], mla_w_o[j])
            else:
                h = _ret_layer(h, ret_norm[j], ret_w_in[j], ret_gn[j], ret_w_o[j])
            h = mlp_block(h, mlp_norm[i], mlp_w1[i].astype(BF16), mlp_w2[i].astype(BF16), final_norm,
                          final=(i == depth - 1), tm=TM_DENSE, tf=mlp_w1.shape[2], name=f"mlp_{i}")
        outs.append(h)
    return jnp.stack(outs).astype(x.dtype)
```

```python
import functools
import math

import numpy as np
import jax
import jax.numpy as jnp
from jax import lax
from jax.experimental import pallas as pl
from jax.experimental.pallas import tpu as pltpu

F32 = jnp.float32
BF16 = jnp.bfloat16
EPS = 1e-6
ROPE_THETA = 10000.0
GRID_W = 64
NEG_BIG = -1e30
HI = lax.Precision.HIGHEST

NA_HEADS, NA_DH, NA_WIN_ROWS, NA_WIN_COLS, NA_Q_ROWS = 16, 64, 8, 16, 2
NA_KBLK = 5
GDN_HEADS, GDN_DK, GDN_CONV, GDN_CHUNK = 8, 128, 5, 64
GDN_GATE_LANES = 32
MLA_HEADS, MLA_Q_RANK, MLA_KV_RANK, MLA_NOPE, MLA_ROPE, MLA_V = 16, 768, 256, 64, 32, 64
MLA_VX = 80
RET_HEADS, RET_DK, RET_DV, RET_CHUNK = 8, 128, 256, 128

VMEM_LIMIT = 52 * 1024 * 1024

TM_DENSE = 512
TM_WIDE = 256
TM_OUT = 1024
TB_GDN = 256
TQ_MLA, TK_MLA = 2048, 512


def _cparams(*sem):
    return pltpu.CompilerParams(dimension_semantics=sem, vmem_limit_bytes=VMEM_LIMIT)


def _rms(x, g):
    return x * lax.rsqrt(jnp.mean(x * x, axis=-1, keepdims=True) + EPS) * g


def _sigmoid(x):
    return 1.0 / (1.0 + jnp.exp(-x))


def _dot(a, b, **kw):
    return jnp.dot(a, b, preferred_element_type=F32, **kw)


def _dot_nt(a, b, **kw):
    return lax.dot_general(a, b, (((1,), (1,)), ((), ())), preferred_element_type=F32, **kw)


def _dot_tn(a, b, **kw):
    return lax.dot_general(a, b, (((0,), (0,)), ((), ())), preferred_element_type=F32, **kw)


def _norm_matmul_body(x_ref, g_ref, w_ref, o_ref, xn_ref):
    @pl.when(pl.program_id(1) == 0)
    def _():
        xn_ref[...] = _rms(x_ref[...], g_ref[...]).astype(BF16)

    o_ref[...] = _dot(xn_ref[...], w_ref[...]).astype(o_ref.dtype)


def norm_matmul(x, g, w, out_dtype, tm, tn, name):
    s, d = x.shape
    n = w.shape[1]
    return pl.pallas_call(
        _norm_matmul_body,
        grid=(s // tm, n // tn),
        in_specs=[pl.BlockSpec((tm, d), lambda i, j: (i, 0)),
                  pl.BlockSpec((1, d), lambda i, j: (0, 0)),
                  pl.BlockSpec((d, tn), lambda i, j: (0, j))],
        out_specs=pl.BlockSpec((tm, tn), lambda i, j: (i, j)),
        out_shape=jax.ShapeDtypeStruct((s, n), out_dtype),
        scratch_shapes=[pltpu.VMEM((tm, d), BF16)],
        compiler_params=_cparams("arbitrary", "arbitrary"),
        name=name,
    )(x, g.reshape(1, d), w)


def _matmul_res_body(a_ref, w_ref, r_ref, o_ref):
    o_ref[...] = r_ref[...] + _dot(a_ref[...], w_ref[...])


def matmul_residual(a, w, res, tm, name):
    s, k = a.shape
    n = w.shape[1]
    return pl.pallas_call(
        _matmul_res_body,
        grid=(s // tm,),
        in_specs=[pl.BlockSpec((tm, k), lambda i: (i, 0)),
                  pl.BlockSpec((k, n), lambda i: (0, 0)),
                  pl.BlockSpec((tm, n), lambda i: (i, 0))],
        out_specs=pl.BlockSpec((tm, n), lambda i: (i, 0)),
        out_shape=jax.ShapeDtypeStruct((s, n), F32),
        compiler_params=_cparams("arbitrary"),
        name=name,
    )(a, w, res)


def _mlp_body(x_ref, g_ref, w1_ref, w2_ref, fg_ref, o_ref, xn_ref, acc_ref, *, nk, final):
    k = pl.program_id(1)

    @pl.when(k == 0)
    def _():
        xn_ref[...] = _rms(x_ref[...], g_ref[...]).astype(BF16)
        acc_ref[...] = jnp.zeros_like(acc_ref)

    a = jnp.maximum(_dot(xn_ref[...], w1_ref[...]), 0.0)
    acc_ref[...] += _dot((a * a).astype(BF16), w2_ref[...])

    @pl.when(k == nk - 1)
    def _():
        y = x_ref[...] + acc_ref[...]
        if final:
            y = _rms(y, fg_ref[...])
        o_ref[...] = y


def mlp_block(x, g, w1, w2, final_g, final, tm, tf, name):
    s, d = x.shape
    f = w1.shape[1]
    nk = f // tf
    return pl.pallas_call(
        functools.partial(_mlp_body, nk=nk, final=final),
        grid=(s // tm, nk),
        in_specs=[pl.BlockSpec((tm, d), lambda i, k: (i, 0)),
                  pl.BlockSpec((1, d), lambda i, k: (0, 0)),
                  pl.BlockSpec((d, tf), lambda i, k: (0, k)),
                  pl.BlockSpec((tf, d), lambda i, k: (k, 0)),
                  pl.BlockSpec((1, d), lambda i, k: (0, 0))],
        out_specs=pl.BlockSpec((tm, d), lambda i, k: (i, 0)),
        out_shape=jax.ShapeDtypeStruct((s, d), F32),
        scratch_shapes=[pltpu.VMEM((tm, d), BF16), pltpu.VMEM((tm, d), F32)],
        compiler_params=_cparams("arbitrary", "arbitrary"),
        name=name,
    )(x, g.reshape(1, d), w1, w2, final_g.reshape(1, d))


def _na_bias_table(rpb, rows):
    n_qblk = rows // NA_Q_ROWS
    reps = {4: 0, 3: 1, 2: 2, 1: n_qblk - 2, 0: n_qblk - 1}
    w, wc, wr = GRID_W, NA_WIN_COLS, NA_WIN_ROWS
    head_order = np.r_[0:NA_HEADS:2, 1:NA_HEADS:2]
    prev = jnp.flip(jnp.pad(rpb.astype(F32)[head_order], ((0, 0), (0, 0), (w - wc, w - wc)), mode="edge"), axis=2)
    e = jnp.stack([prev[:, :, w - 1 - kc:2 * w - 1 - kc] for kc in range(w)], axis=2)
    qc = np.arange(w)
    qc0 = np.clip(qc - wc // 2, 0, w - wc)
    col_ok = (qc[:, None] >= qc0[None, :]) & (qc[:, None] < qc0[None, :] + wc)
    tabs, oks = [], []
    for t in range(5):
        i = reps[t]
        base = int(np.clip(i - 2, 0, n_qblk - NA_KBLK))
        per_qr, ok_qr = [], []
        for qr in range(NA_Q_ROWS):
            qrow = NA_Q_ROWS * i + qr
            qr0 = int(np.clip(qrow - wr // 2, 0, rows - wr))
            krows = NA_Q_ROWS * base + np.arange(NA_KBLK * NA_Q_ROWS)
            drow = np.clip(krows - qrow + wr - 1, 0, 2 * wr - 2)
            row_ok = (krows >= qr0) & (krows < qr0 + wr)
            per_qr.append(jnp.concatenate([e[:, int(d)] for d in drow], axis=1))
            ok_qr.append(np.concatenate([col_ok & bool(r) for r in row_ok], axis=0))
        tabs.append(jnp.concatenate(per_qr, axis=2))
        oks.append(np.concatenate(ok_qr, axis=1))
    return jnp.where(np.stack(oks)[:, None], jnp.stack(tabs), NEG_BIG)


def _na_body(q_ref, k0, k1, k2, k3, k4, v0, v1, v2, v3, v4, b_ref, o_ref):
    krefs = (k0, k1, k2, k3, k4)
    vrefs = (v0, v1, v2, v3, v4)
    nkb = NA_Q_ROWS * GRID_W
    npair = NA_HEADS // 2
    pairs = [slice(pr * 128, (pr + 1) * 128) for pr in range(npair)]
    lane_k = lax.broadcasted_iota(jnp.int32, (NA_KBLK * nkb, 128), 1)
    ones_blk = jnp.where(lane_k == 0, 1.0, 0.0).astype(BF16)
    first = lax.broadcasted_iota(jnp.int32, (npair, nkb, 128), 2) < NA_DH
    k_all = jnp.stack([jnp.concatenate([kr[:, sl] for kr in krefs], axis=0) for sl in pairs], axis=0)
    v_aug = jnp.stack([jnp.concatenate([jnp.concatenate([vr[:, sl] for vr in vrefs], axis=0), ones_blk], axis=1)
                       for sl in pairs], axis=0)
    q2 = jnp.stack([q_ref[:, sl] for sl in pairs], axis=0)
    outs = []
    for hh in range(2):
        qh = jnp.where(first if hh == 0 else jnp.logical_not(first), q2, jnp.zeros_like(q2))
        s_t = _bmm_nt(k_all, qh) + b_ref[0, hh * npair:(hh + 1) * npair]
        m = jnp.max(s_t, axis=1, keepdims=True)
        p = jnp.swapaxes(jnp.exp(s_t - m), 1, 2).astype(BF16)
        o = _bmm(p, v_aug)
        outs.append(o[:, :, :128] / o[:, :, 128:129])
    res = jnp.where(first, outs[0], outs[1]).astype(o_ref.dtype)
    for pr, sl in enumerate(pairs):
        o_ref[:, sl] = res[pr]


def na_attention(qkv, bias_tab):
    s = qkv.shape[0]
    d = NA_HEADS * NA_DH
    tq = NA_Q_ROWS * GRID_W
    n_qblk = s // tq
    top = n_qblk - NA_KBLK

    def base(i):
        return jnp.clip(i - 2, 0, top)

    kv_specs = [pl.BlockSpec((tq, d), functools.partial(lambda i, dd, c: (base(i) + dd, c), dd=dd, c=c))
                for c in (1, 2) for dd in range(NA_KBLK)]
    return pl.pallas_call(
        _na_body,
        grid=(n_qblk,),
        in_specs=[pl.BlockSpec((tq, d), lambda i: (i, 0))] + kv_specs
                 + [pl.BlockSpec((1, NA_HEADS, NA_KBLK * tq, tq), lambda i: (base(i) - i + 4, 0, 0, 0))],
        out_specs=pl.BlockSpec((tq, d), lambda i: (i, 0)),
        out_shape=jax.ShapeDtypeStruct((s, d), BF16),
        compiler_params=_cparams("arbitrary"),
        name="na_attention",
    )(qkv, *([qkv] * (2 * NA_KBLK)), bias_tab)


def _gdn_conv_body(x_ref, xp_ref, xn_ref, w_ref, o_ref, xe_ref, *, tm, nt):
    i = pl.program_id(0)
    j = pl.program_id(1)
    xe_ref[0:8, :] = jnp.where(i > 0, xp_ref[...], 0.0)
    xe_ref[8:8 + tm, :] = x_ref[...]
    xe_ref[8 + tm:16 + tm, :] = jnp.where(i < nt - 1, xn_ref[...], 0.0)
    w = w_ref[...]
    half = GDN_CONV // 2
    acc = xe_ref[8 - half:8 - half + tm, :] * w[0:1]
    for t in range(1, GDN_CONV):
        acc = acc + xe_ref[8 - half + t:8 - half + t + tm, :] * w[t:t + 1]
    y = acc * _sigmoid(acc)
    for h in range(GDN_HEADS):
        sl = slice(h * GDN_DK, (h + 1) * GDN_DK)
        yh = y[:, sl]
        nrm = lax.rsqrt(jnp.sum(yh * yh, axis=1, keepdims=True) + EPS)
        fac = jnp.where(j == 0, nrm * GDN_DK ** -0.5, jnp.where(j == 1, nrm, 1.0))
        o_ref[:, sl] = (yh * fac).astype(o_ref.dtype)


def gdn_conv(proj, conv_w_t, tm):
    s = proj.shape[0]
    d = GDN_HEADS * GDN_DK
    nt = s // tm
    r8 = tm // 8
    return pl.pallas_call(
        functools.partial(_gdn_conv_body, tm=tm, nt=nt),
        grid=(nt, 3),
        in_specs=[pl.BlockSpec((tm, d), lambda i, j: (i, j)),
                  pl.BlockSpec((8, d), lambda i, j: (jnp.maximum(i * r8 - 1, 0), j)),
                  pl.BlockSpec((8, d), lambda i, j: (jnp.minimum((i + 1) * r8, nt * r8 - 1), j)),
                  pl.BlockSpec((GDN_CONV, d), lambda i, j: (0, j))],
        out_specs=pl.BlockSpec((tm, d), lambda i, j: (i, j)),
        out_shape=jax.ShapeDtypeStruct((s, 3 * d), BF16),
        scratch_shapes=[pltpu.VMEM((tm + 16, d), F32)],
        compiler_params=_cparams("arbitrary", "arbitrary"),
        name="gdn_conv",
    )(proj, proj, proj, conv_w_t)


def _gdn_gates_body(gp_ref, alog_ref, dtb_ref, col_ref, row_ref, *, tm):
    c = GDN_CHUNK
    gp = gp_ref[...]
    lane = lax.broadcasted_iota(jnp.int32, (tm, 128), 1)
    is_f = (lane >= 16) & (lane < 24)
    is_b = (lane >= 24) & (lane < 32)
    beta = _sigmoid(gp)
    z = gp + dtb_ref[...]
    softplus = jnp.maximum(z, 0.0) + jnp.log1p(jnp.exp(-jnp.abs(z)))
    g = jnp.where(is_f | is_b, -jnp.exp(alog_ref[...]) * softplus, 0.0)
    r = lax.broadcasted_iota(jnp.int32, (c, c), 0)
    cc = lax.broadcasted_iota(jnp.int32, (c, c), 1)
    tri_lo = (r >= cc).astype(F32)
    tri_up = (r <= cc).astype(F32)
    lane_c = lax.broadcasted_iota(jnp.int32, (c, 128), 1)
    is_f_c = (lane_c >= 16) & (lane_c < 24)
    is_g_c = (lane_c >= 16) & (lane_c < 32)
    for ch in range(tm // c):
        gc = g[ch * c:(ch + 1) * c]
        cum = jnp.where(is_f_c, _dot(tri_lo, gc, precision=HI), _dot(tri_up, gc, precision=HI))
        colc = jnp.where(is_g_c, cum, beta[ch * c:(ch + 1) * c])
        col_ref[ch * c:(ch + 1) * c, :] = colc
        row_ref[ch] = colc.T[:GDN_GATE_LANES, :]


def gdn_gates(gp, gate_blk, alog_vec, dtb_vec, tm):
    s = gp.shape[0]
    c = GDN_CHUNK
    return pl.pallas_call(
        functools.partial(_gdn_gates_body, tm=tm),
        grid=(s // tm,),
        in_specs=[pl.BlockSpec((tm, 128), lambda i: (i, gate_blk)),
                  pl.BlockSpec((1, 128), lambda i: (0, 0)),
                  pl.BlockSpec((1, 128), lambda i: (0, 0))],
        out_specs=[pl.BlockSpec((tm, 128), lambda i: (i, 0)),
                   pl.BlockSpec((tm // c, GDN_GATE_LANES, c), lambda i: (i, 0, 0))],
        out_shape=[jax.ShapeDtypeStruct((s, 128), F32), jax.ShapeDtypeStruct((s // c, GDN_GATE_LANES, c), F32)],
        compiler_params=_cparams("arbitrary"),
        name="gdn_gates",
    )(gp, alog_vec, dtb_vec)


def _bmm(a, b):
    return jnp.einsum("nik,nkj->nij", a, b, preferred_element_type=F32)


def _bmm_nt(a, b):
    return jnp.einsum("nik,njk->nij", a, b, preferred_element_type=F32)


def _unit_tri_inverse(lmat, r, cc):
    eye = (r == cc).astype(F32)
    diag_blk = (r // 16) == (cc // 16)
    ld = jnp.where(diag_blk, lmat, 0.0)
    lo = lmat - ld
    x = eye - ld
    p = _bmm(ld, ld)
    x = x + _bmm(x, p)
    p = _bmm(p, p)
    x = x + _bmm(x, p)
    p = _bmm(p, p)
    dinv = x + _bmm(x, p)
    n1 = _bmm(dinv, lo)
    n2 = _bmm(n1, n1)
    y = eye - n1 + n2 - _bmm(n1, n2)
    return _bmm(y, dinv)


def _gdn_local_body(qkv_ref, col_ref, row_ref, *out_refs, tb):
    c, dk, hh = GDN_CHUNK, GDN_DK, GDN_HEADS
    nc = tb // c
    nb = hh * nc
    r = lax.broadcasted_iota(jnp.int32, (nb, c, c), 1)
    cc = lax.broadcasted_iota(jnp.int32, (nb, c, c), 2)
    col3 = col_ref[...].reshape(nc, c, 128)

    def heads(base):
        return jnp.concatenate([qkv_ref[:, base + h * dk:base + (h + 1) * dk].astype(F32).reshape(nc, c, dk)
                                for h in range(hh)], axis=0)

    qc, kc, vc = heads(0), heads(hh * dk), heads(2 * hh * dk)
    kc16 = kc.astype(BF16)
    qk = _bmm_nt(qc.astype(BF16), kc16)
    for d in range(2):
        reverse = d == 1
        u_ref, wq_ref, ak_ref, gl_ref = out_refs[4 * d:4 * d + 4]
        incl = (r <= cc) if reverse else (r >= cc)
        strict = (r < cc) if reverse else (r > cc)
        b0 = 8 if reverse else 0
        g0 = 24 if reverse else 16
        beta = jnp.concatenate([col3[:, :, b0 + h:b0 + h + 1] for h in range(hh)], axis=0)
        gcol = jnp.concatenate([col3[:, :, g0 + h:g0 + h + 1] for h in range(hh)], axis=0)
        grow = jnp.concatenate([row_ref[:, g0 + h:g0 + h + 1, :] for h in range(hh)], axis=0)
        g_last = gcol[:, 0:1, :] if reverse else gcol[:, c - 1:c, :]
        decay = jnp.where(incl, jnp.exp(jnp.where(incl, gcol - grow, 0.0)), 0.0)
        eg = jnp.exp(gcol)
        kb = kc * beta
        lmat = jnp.where(strict, _bmm_nt(kb.astype(BF16), kc16) * decay, 0.0)
        tinv = _unit_tri_inverse(lmat, r, cc)
        sol = _bmm(tinv, jnp.concatenate([vc * beta, kb * eg], axis=2))
        a_intra = jnp.where(incl, qk * decay, 0.0).astype(BF16)
        w16 = sol[:, :, dk:].astype(BF16)
        qd16 = (qc * eg).astype(BF16)
        kdt16 = jnp.swapaxes(kc * jnp.exp(g_last - gcol), 1, 2).astype(BF16)
        gl = jnp.broadcast_to(jnp.exp(g_last), (nb, 1, dk))
        for h in range(hh):
            sl = slice(h * dk, (h + 1) * dk)
            hb = slice(h * nc, (h + 1) * nc)
            u_ref[:, sl] = sol[hb, :, :dk].reshape(tb, dk)
            wq_ref[:, 0:c, sl] = w16[hb]
            wq_ref[:, c:2 * c, sl] = qd16[hb]
            ak_ref[:, 0:c, h * c:(h + 1) * c] = a_intra[hb]
            ak_ref[:, c:c + dk, h * c:(h + 1) * c] = kdt16[hb]
            gl_ref[:, h:h + 1, :] = gl[hb]


def gdn_local(qkv, gcol, grow, tb):
    s = qkv.shape[0]
    c, d = GDN_CHUNK, GDN_HEADS * GDN_DK
    nc = tb // c
    row = lambda w: pl.BlockSpec((tb, w), lambda i: (i, 0))
    per_dir_specs = [row(d),
                     pl.BlockSpec((nc, 2 * c, d), lambda i: (i, 0, 0)),
                     pl.BlockSpec((nc, c + GDN_DK, GDN_HEADS * c), lambda i: (i, 0, 0)),
                     pl.BlockSpec((nc, GDN_HEADS, GDN_DK), lambda i: (i, 0, 0))]
    per_dir_shapes = [jax.ShapeDtypeStruct((s, d), F32),
                      jax.ShapeDtypeStruct((s // c, 2 * c, d), BF16),
                      jax.ShapeDtypeStruct((s // c, c + GDN_DK, GDN_HEADS * c), BF16),
                      jax.ShapeDtypeStruct((s // c, GDN_HEADS, GDN_DK), F32)]
    return pl.pallas_call(
        functools.partial(_gdn_local_body, tb=tb),
        grid=(s // tb,),
        in_specs=[row(3 * d), row(128), pl.BlockSpec((nc, GDN_GATE_LANES, c), lambda i: (i, 0, 0))],
        out_specs=per_dir_specs * 2,
        out_shape=per_dir_shapes * 2,
        compiler_params=_cparams("arbitrary"),
        name="gdn_local",
    )(qkv, gcol, grow)


def _gdn_scan_body(*refs, tb):
    c, dk, hh = GDN_CHUNK, GDN_DK, GDN_HEADS
    nc = tb // c
    ins, (of_ref, ob_ref, st_ref) = refs[:8], refs[8:]

    @pl.when(pl.program_id(0) == 0)
    def _():
        st_ref[...] = jnp.zeros_like(st_ref)

    chains = [(d, h) for d in range(2) for h in range(hh)]
    for step in range(nc):
        ch = (step, nc - 1 - step)

        def gather(idx, pick):
            return jnp.stack([pick(ins[4 * d + idx], ch[d], h) for d, h in chains], axis=0)

        u = gather(0, lambda r, cpos, h: r[cpos * c:(cpos + 1) * c, h * dk:(h + 1) * dk])
        wq = gather(1, lambda r, cpos, h: r[cpos, :, h * dk:(h + 1) * dk])
        ak = gather(2, lambda r, cpos, h: r[cpos, :, h * c:(h + 1) * c])
        gl = gather(3, lambda r, cpos, h: r[cpos, h:h + 1, :])
        state = st_ref[...]
        ws_qs = _bmm(wq, state.astype(BF16))
        v_new = u - ws_qs[:, :c]
        av_kv = _bmm(ak, v_new.astype(BF16))
        o = ws_qs[:, c:] + av_kv[:, :c]
        for n, (d, h) in enumerate(chains):
            (of_ref, ob_ref)[d][ch[d] * c:(ch[d] + 1) * c, h * dk:(h + 1) * dk] = o[n]
        st_ref[...] = state * gl + av_kv[:, c:]


def gdn_scan(local_out, tb):
    s, d = local_out[0].shape
    c = GDN_CHUNK
    nb = s // tb
    nc = tb // c

    def dir_specs(blk):
        return [pl.BlockSpec((tb, d), lambda t: (blk(t), 0)),
                pl.BlockSpec((nc, 2 * c, d), lambda t: (blk(t), 0, 0)),
                pl.BlockSpec((nc, c + GDN_DK, GDN_HEADS * c), lambda t: (blk(t), 0, 0)),
                pl.BlockSpec((nc, GDN_HEADS, GDN_DK), lambda t: (blk(t), 0, 0))]

    fwd = lambda t: t
    bwd = lambda t: nb - 1 - t
    return pl.pallas_call(
        functools.partial(_gdn_scan_body, tb=tb),
        grid=(nb,),
        in_specs=dir_specs(fwd) + dir_specs(bwd),
        out_specs=[pl.BlockSpec((tb, d), lambda t: (t, 0)), pl.BlockSpec((tb, d), lambda t: (nb - 1 - t, 0))],
        out_shape=[jax.ShapeDtypeStruct((s, d), F32)] * 2,
        scratch_shapes=[pltpu.VMEM((2 * GDN_HEADS, GDN_DK, GDN_DK), F32)],
        compiler_params=_cparams("arbitrary"),
        name="gdn_scan",
    )(*local_out)


def _gdn_post_body(of_ref, ob_ref, z_ref, on_ref, w_ref, r_ref, o_ref):
    on = on_ref[...]
    parts = []
    for h in range(GDN_HEADS):
        sl = slice(h * GDN_DK, (h + 1) * GDN_DK)
        o = of_ref[:, sl] + ob_ref[:, sl]
        z = z_ref[:, sl]
        parts.append((_rms(o, on) * (z * _sigmoid(z))).astype(BF16))
    y = jnp.concatenate(parts, axis=1)
    o_ref[...] = r_ref[...] + _dot(y, w_ref[...])


def gdn_post(o_f, o_b, proj, o_norm, w_o, res, tm):
    s, d = o_f.shape
    return pl.pallas_call(
        _gdn_post_body,
        grid=(s // tm,),
        in_specs=[pl.BlockSpec((tm, d), lambda i: (i, 0)),
                  pl.BlockSpec((tm, d), lambda i: (i, 0)),
                  pl.BlockSpec((tm, d), lambda i: (i, 3)),
                  pl.BlockSpec((1, GDN_DK), lambda i: (0, 0)),
                  pl.BlockSpec((d, d), lambda i: (0, 0)),
                  pl.BlockSpec((tm, d), lambda i: (i, 0))],
        out_specs=pl.BlockSpec((tm, d), lambda i: (i, 0)),
        out_shape=jax.ShapeDtypeStruct((s, d), F32),
        compiler_params=_cparams("arbitrary"),
        name="gdn_post",
    )(o_f, o_b, proj, o_norm.reshape(1, GDN_DK), w_o, res)


def _mla_pre_body(x_ref, g_ref, win_ref, qn_ref, kvn_ref, wuq_ref, wuk_ref, wuv_ref, vone_ref, ct_ref, sn_ref,
                  ctt_ref, snt_ref, q_ref, k_ref, v_ref):
    xn = _rms(x_ref[...], g_ref[...]).astype(BF16)
    proj = _dot(xn, win_ref[...])
    cq = _rms(proj[:, :MLA_Q_RANK], qn_ref[...]).astype(BF16)
    ckv = _rms(proj[:, MLA_Q_RANK:MLA_Q_RANK + MLA_KV_RANK], kvn_ref[...]).astype(BF16)
    pr = proj[:, MLA_Q_RANK + MLA_KV_RANK:]
    ct, sn = ct_ref[...], sn_ref[...]
    scale = (MLA_NOPE + MLA_ROPE) ** -0.5 * math.log2(math.e)
    ctt, snt = ctt_ref[...], snt_ref[...]
    q_t = _dot_nt(wuq_ref[...], cq)
    for h in range(MLA_HEADS):
        qh = q_t[h * 128:(h + 1) * 128, :]
        partner = jnp.concatenate([qh[64:, :], qh[:64, :]], axis=0)
        q_ref[h * 128:(h + 1) * 128, :] = ((qh * ctt + partner * snt) * scale).astype(BF16)
    kr = pr * ct + pltpu.roll(pr, 64, 1) * sn
    k = _dot(ckv, wuk_ref[...])
    for h in range(MLA_HEADS):
        sl = slice(h * 128, (h + 1) * 128)
        k_ref[:, sl] = (k[:, sl] + kr).astype(BF16)
    v_t = (_dot_nt(wuv_ref[...], ckv) + vone_ref[...]).astype(BF16)
    v_ref[...] = v_t.reshape(v_ref.shape)


def mla_pre(x, g, win_p, q_norm, kv_norm, wuq_t, wuk_p, wuv_t, vone, ct, sn, tm):
    s, d = x.shape
    hp = MLA_HEADS * 128
    npair = MLA_HEADS // 2
    full = lambda a: pl.BlockSpec(a.shape, lambda i: (0,) * a.ndim)
    g2, qn2, kvn2 = g.reshape(1, d), q_norm.reshape(1, -1), kv_norm.reshape(1, -1)
    ct_t, sn_t = jnp.transpose(ct), jnp.transpose(sn)
    return pl.pallas_call(
        _mla_pre_body,
        grid=(s // tm,),
        in_specs=[pl.BlockSpec((tm, d), lambda i: (i, 0)), full(g2), full(win_p), full(qn2), full(kvn2),
                  full(wuq_t), full(wuk_p), full(wuv_t), full(vone),
                  pl.BlockSpec((tm, 128), lambda i: (i, 0)), pl.BlockSpec((tm, 128), lambda i: (i, 0)),
                  pl.BlockSpec((128, tm), lambda i: (0, i)), pl.BlockSpec((128, tm), lambda i: (0, i))],
        out_specs=[pl.BlockSpec((hp, tm), lambda i: (0, i)), pl.BlockSpec((tm, hp), lambda i: (i, 0)),
                   pl.BlockSpec((npair, 1, 2 * MLA_VX, tm), lambda i: (0, i, 0, 0))],
        out_shape=[jax.ShapeDtypeStruct((hp, s), BF16), jax.ShapeDtypeStruct((s, hp), BF16),
                   jax.ShapeDtypeStruct((npair, s // tm, 2 * MLA_VX, tm), BF16)],
        compiler_params=_cparams("arbitrary"),
        name="mla_pre",
    )(x, g2, win_p, qn2, kvn2, wuq_t, wuk_p, wuv_t, vone, ct, sn, ct_t, sn_t)


def _mla_attn_body(qt_ref, k_ref, vt_ref, o_ref, acc_ref, s_ref, *, tq, tk, nk):
    acc_ref[...] = jnp.zeros_like(acc_ref)

    def scores(j, slot):
        off = pl.multiple_of(j * tk, tk)
        for hh in range(2):
            sl = slice(hh * 128, (hh + 1) * 128)
            s_ref[slot, hh] = _dot(k_ref[pl.ds(off, tk), sl], qt_ref[sl, :])

    def consume(j, slot, carry):
        new = []
        for hh in range(2):
            m_old = carry[hh]
            rows = slice(hh * MLA_VX, (hh + 1) * MLA_VX)
            s = s_ref[slot, hh]
            m_new = jnp.maximum(m_old, jnp.max(s, axis=0, keepdims=True))
            p = jnp.exp2(s - m_new).astype(BF16)
            acc_ref[rows, :] = acc_ref[rows, :] * jnp.exp2(m_old - m_new) + _dot(vt_ref[0, j, rows, :], p)
            new.append(m_new)
        return tuple(new)

    scores(0, 0)

    def body(jj, carry):
        j = 2 * jj
        scores(j + 1, 1)
        carry = consume(j, 0, carry)
        scores(jnp.minimum(j + 2, nk - 1), 0)
        return consume(j + 1, 1, carry)

    lax.fori_loop(0, nk // 2, body, (jnp.full((1, tq), -jnp.inf, F32),) * 2)
    o_t = jnp.concatenate([acc_ref[hh * MLA_VX:hh * MLA_VX + MLA_V, :]
                           / acc_ref[hh * MLA_VX + MLA_V:hh * MLA_VX + MLA_V + 1, :] for hh in range(2)], axis=0)
    o_ref[...] = o_t.T.astype(o_ref.dtype)


def mla_attention(q_t, k, v_t, tq):
    s = k.shape[0]
    npair, nk, _, tk = v_t.shape
    return pl.pallas_call(
        functools.partial(_mla_attn_body, tq=tq, tk=tk, nk=nk),
        grid=(npair, s // tq),
        in_specs=[pl.BlockSpec((256, tq), lambda p, i: (p, i)),
                  pl.BlockSpec((s, 256), lambda p, i: (0, p)),
                  pl.BlockSpec((1, nk, 2 * MLA_VX, tk), lambda p, i: (p, 0, 0, 0))],
        out_specs=pl.BlockSpec((tq, 128), lambda p, i: (i, p)),
        out_shape=jax.ShapeDtypeStruct((s, MLA_HEADS * MLA_V), BF16),
        scratch_shapes=[pltpu.VMEM((2 * MLA_VX, tq), F32), pltpu.VMEM((2, 2, tk, tq), F32)],
        compiler_params=_cparams("arbitrary", "arbitrary"),
        name="mla_attention",
    )(q_t, k, v_t)


def _ret_scan_body(qf, kf, vf, ctf, snf, qb, kb, vb, ctb, snb,
                   dmf, qdf, kdf, gcf, dmb, qdb, kdb, gcb, of_ref, ob_ref, st_ref):
    @pl.when(pl.program_id(0) == 0)
    def _():
        st_ref[...] = jnp.zeros_like(st_ref)

    hh = RET_HEADS
    c = qf.shape[0]

    def chains(ref_f, ref_b, width):
        return jnp.stack([r[:, h * width:(h + 1) * width] for r in (ref_f, ref_b) for h in range(hh)], axis=0)

    def both(a_f, a_b):
        return jnp.concatenate([jnp.broadcast_to(a[...][None], (hh, c, 128)) for a in (a_f, a_b)], axis=0)

    def tab(t_f, t_b):
        return jnp.concatenate([t_f[...], t_b[...]], axis=0)

    q, k = chains(qf, qb, RET_DK), chains(kf, kb, RET_DK)
    v16 = chains(vf, vb, RET_DV).astype(BF16)
    ct, sn = both(ctf, ctb), both(snf, snb)
    qr = q * ct + pltpu.roll(q, 64, 2) * sn
    kr = (k * ct + pltpu.roll(k, 64, 2) * sn) * RET_DK ** -0.5
    state = st_ref[...]
    inner = _bmm_nt(qr.astype(BF16), kr.astype(BF16)) * tab(dmf, dmb)
    o = _bmm(inner.astype(BF16), v16) + _bmm((qr * tab(qdf, qdb)).astype(BF16), state.astype(BF16))
    for h in range(hh):
        of_ref[:, h * RET_DV:(h + 1) * RET_DV] = o[h]
        ob_ref[:, h * RET_DV:(h + 1) * RET_DV] = o[hh + h]
    kd_t = jnp.swapaxes(kr * tab(kdf, kdb), 1, 2).astype(BF16)
    st_ref[...] = state * tab(gcf, gcb) + _bmm(kd_t, v16)


def ret_scan(proj, ct, sn, tabs_f, tabs_b):
    s = proj.shape[0]
    c = RET_CHUNK
    nb = s // c
    dq = RET_HEADS * RET_DK
    dv = RET_HEADS * RET_DV

    def dir_specs(blk):
        return [pl.BlockSpec((c, dq), lambda t: (blk(t), 0)),
                pl.BlockSpec((c, dq), lambda t: (blk(t), 1)),
                pl.BlockSpec((c, dv), lambda t: (blk(t), 1)),
                pl.BlockSpec((c, 128), lambda t: (blk(t), 0)),
                pl.BlockSpec((c, 128), lambda t: (blk(t), 0))]

    full = lambda a: pl.BlockSpec(a.shape, lambda t: (0,) * a.ndim)
    fwd = lambda t: t
    bwd = lambda t: nb - 1 - t
    return pl.pallas_call(
        _ret_scan_body,
        grid=(nb,),
        in_specs=dir_specs(fwd) + dir_specs(bwd) + [full(a) for a in tabs_f] + [full(a) for a in tabs_b],
        out_specs=[pl.BlockSpec((c, dv), lambda t: (t, 0)), pl.BlockSpec((c, dv), lambda t: (nb - 1 - t, 0))],
        out_shape=[jax.ShapeDtypeStruct((s, dv), F32)] * 2,
        scratch_shapes=[pltpu.VMEM((2 * RET_HEADS, RET_DK, RET_DV), F32)],
        compiler_params=_cparams("arbitrary"),
        name="ret_scan",
    )(proj, proj, proj, ct, sn, proj, proj, proj, ct, sn, *tabs_f, *tabs_b)


def _ret_post_body(of_ref, ob_ref, gate_ref, gn_ref, w_ref, r_ref, o_ref):
    parts = []
    for h in range(RET_HEADS):
        sv = slice(h * RET_DV, (h + 1) * RET_DV)
        o = of_ref[:, sv] + ob_ref[:, sv]
        mu = jnp.mean(o, axis=1, keepdims=True)
        oc = o - mu
        var = jnp.mean(oc * oc, axis=1, keepdims=True)
        gate = gate_ref[:, sv]
        parts.append((gate * _sigmoid(gate) * (oc * lax.rsqrt(var + EPS) * gn_ref[:, sv])).astype(BF16))
    y = jnp.concatenate(parts, axis=1)
    o_ref[...] = r_ref[...] + _dot(y, w_ref[...])


def ret_post(o_f, o_b, proj, gn_w, w_o, res, tm):
    s, dv = o_f.shape
    d = res.shape[1]
    return pl.pallas_call(
        _ret_post_body,
        grid=(s // tm,),
        in_specs=[pl.BlockSpec((tm, dv), lambda i: (i, 0)),
                  pl.BlockSpec((tm, dv), lambda i: (i, 0)),
                  pl.BlockSpec((tm, dv), lambda i: (i, 2)),
                  pl.BlockSpec((1, dv), lambda i: (0, 0)),
                  pl.BlockSpec((dv, d), lambda i: (0, 0)),
                  pl.BlockSpec((tm, d), lambda i: (i, 0))],
        out_specs=pl.BlockSpec((tm, d), lambda i: (i, 0)),
        out_shape=jax.ShapeDtypeStruct((s, d), F32),
        compiler_params=_cparams("arbitrary"),
        name="ret_post",
    )(o_f, o_b, proj, gn_w.reshape(1, dv), w_o, res)


def _na_layer(h, norm, w_qkv, rpb, w_o):
    s = h.shape[0]
    d = NA_HEADS * NA_DH
    col_scale = jnp.where(jnp.arange(3 * d) < d, NA_DH ** -0.5, 1.0).astype(F32)
    w16 = (w_qkv * col_scale[None, :]).astype(BF16)
    qkv = norm_matmul(h, norm, w16, BF16, tm=TM_DENSE, tn=3 * d, name="na_qkv")
    o = na_attention(qkv, _na_bias_table(rpb, s // GRID_W))
    return matmul_residual(o, w_o.astype(BF16), h, tm=min(TM_OUT, s), name="na_out")


def _gdn_layer(h, norm, w_in, conv_w, a_log_f, a_log_b, dt_bias_f, dt_bias_b, o_norm, w_o):
    d = GDN_HEADS * GDN_DK
    n_main = 4 * d
    w_all = jnp.pad(w_in, ((0, 0), (0, 128 - 4 * GDN_HEADS))).astype(BF16)
    proj = norm_matmul(h, norm, w_all, F32, tm=TM_DENSE, tn=n_main + 128, name="gdn_in")
    z8 = jnp.zeros((2 * GDN_HEADS,), F32)
    pad = jnp.zeros((128 - 4 * GDN_HEADS,), F32)
    alog_vec = jnp.concatenate([z8, a_log_f.astype(F32), a_log_b.astype(F32), pad]).reshape(1, 128)
    dtb_vec = jnp.concatenate([z8, dt_bias_f.astype(F32), dt_bias_b.astype(F32), pad]).reshape(1, 128)
    gcol, grow = gdn_gates(proj, n_main // 128, alog_vec, dtb_vec, tm=TM_DENSE)
    qkv = gdn_conv(proj, jnp.transpose(conv_w).astype(F32), tm=TM_DENSE)
    o_f, o_b = gdn_scan(gdn_local(qkv, gcol, grow, tb=TB_GDN), tb=TB_GDN)
    return gdn_post(o_f, o_b, proj, o_norm, w_o.astype(BF16), h, tm=TM_DENSE)


def _mla_lane_maps():
    lane = np.arange(128)
    rope1 = lane < 16
    rope2 = (lane >= 64) & (lane < 80)
    nope_a = (lane >= 16) & (lane < 64)
    nope_b = (lane >= 80) & (lane < 96)
    q_dim = np.where(rope1, MLA_NOPE + lane, np.where(nope_a, lane - 16, np.where(rope2, lane + 16, lane - 32)))
    q_ok = lane < 96
    nope_dim = np.where(nope_a, lane - 16, lane - 32)
    nope_ok = nope_a | nope_b
    rope_dim = np.where(rope1, lane, lane - 64 + 16)
    rope_ok = rope1 | rope2
    return q_dim, q_ok, nope_dim, nope_ok, rope_dim, rope_ok


def _mla_layer(h, norm, w_in, q_norm, w_uq, kv_norm, w_ukv, w_o):
    s = h.shape[0]
    hh = MLA_HEADS
    dqk = MLA_NOPE + MLA_ROPE
    q_dim, q_ok, nope_dim, nope_ok, rope_dim, rope_ok = _mla_lane_maps()
    head = np.arange(hh)[:, None]
    q_cols = (head * dqk + np.where(q_ok, q_dim, 0)[None, :]).reshape(-1)
    wuq_t = jnp.transpose(jnp.where(np.tile(q_ok, hh)[None, :], w_uq[:, q_cols], 0.0)).astype(BF16)
    k_cols = (head * (MLA_NOPE + MLA_V) + np.where(nope_ok, nope_dim, 0)[None, :]).reshape(-1)
    wuk_p = jnp.where(np.tile(nope_ok, hh)[None, :], w_ukv[:, k_cols], 0.0).astype(BF16)
    vx = np.arange(MLA_VX)[None, :]
    v_cols = (head * (MLA_NOPE + MLA_V) + MLA_NOPE + np.minimum(vx, MLA_V - 1)).reshape(-1)
    v_ok = np.broadcast_to(vx < MLA_V, (hh, MLA_VX)).reshape(-1)
    wuv_t = jnp.where(v_ok[:, None], jnp.transpose(w_ukv[:, v_cols]), 0.0).astype(BF16)
    vone = jnp.asarray(np.broadcast_to(vx == MLA_V, (hh, MLA_VX)).reshape(-1, 1), F32)
    n_c = MLA_Q_RANK + MLA_KV_RANK
    w_rope = jnp.where(rope_ok[None, :], w_in[:, n_c + np.where(rope_ok, rope_dim, 0)], 0.0)
    win_p = jnp.concatenate([w_in[:, :n_c], w_rope], axis=1).astype(BF16)
    half = MLA_ROPE // 2
    inv = 1.0 / (ROPE_THETA ** (jnp.arange(half, dtype=F32) / half))
    ang = jnp.arange(s, dtype=jnp.int32).astype(F32)[:, None] * inv[None, :]
    cos, sin = jnp.cos(ang), jnp.sin(ang)
    ones = jnp.ones((s, 48), F32)
    zeros = jnp.zeros((s, 48), F32)
    ct = jnp.concatenate([cos, ones, cos, ones], axis=1)
    sn = jnp.concatenate([-sin, zeros, sin, zeros], axis=1)
    q_t, k, v_t = mla_pre(h, norm, win_p, q_norm, kv_norm, wuq_t, wuk_p, wuv_t, vone, ct, sn, tm=TK_MLA)
    o = mla_attention(q_t, k, v_t, tq=min(TQ_MLA, s))
    return matmul_residual(o, w_o.astype(BF16), h, tm=min(TM_OUT, s), name="mla_out")


def _ret_tables(reverse):
    c = RET_CHUNK
    lg = jnp.log1p(-jnp.exp2(-5.0 - jnp.arange(RET_HEADS, dtype=F32)))
    if reverse:
        lg = lg[::-1]
    idx = np.arange(c)
    diff = idx[:, None] - idx[None, :]
    if reverse:
        keep, expo = diff < 0, -diff
        q_pow, k_pow = c - idx, idx
    else:
        keep, expo = diff >= 0, diff
        q_pow, k_pow = idx + 1, c - 1 - idx
    dmask = jnp.where(keep, jnp.exp(jnp.where(keep, expo, 0).astype(F32)[None] * lg[:, None, None]), 0.0)
    q_dec = jnp.exp(jnp.asarray(q_pow, F32)[None, :] * lg[:, None])
    k_dec = jnp.exp(jnp.asarray(k_pow, F32)[None, :] * lg[:, None])
    g_chunk = jnp.exp(c * lg)
    bc = lambda t: jnp.broadcast_to(t[:, :, None], (RET_HEADS, c, RET_DK))
    return [dmask, bc(q_dec), bc(k_dec), jnp.broadcast_to(g_chunk[:, None, None], (RET_HEADS, 1, RET_DV))]


def _ret_layer(h, norm, w_in, gn_w, w_o):
    s = h.shape[0]
    proj = norm_matmul(h, norm, w_in.astype(BF16), F32, tm=TM_WIDE, tn=w_in.shape[1], name="ret_in")
    half = RET_DK // 2
    inv = 1.0 / (ROPE_THETA ** (jnp.arange(half, dtype=F32) / half))
    ang = jnp.arange(s, dtype=jnp.int32).astype(F32)[:, None] * inv[None, :]
    cos, sin = jnp.cos(ang), jnp.sin(ang)
    ct = jnp.concatenate([cos, cos], axis=1)
    sn = jnp.concatenate([-sin, sin], axis=1)
    o_f, o_b = ret_scan(proj, ct, sn, _ret_tables(False), _ret_tables(True))
    return ret_post(o_f, o_b, proj, gn_w, w_o.astype(BF16), h, tm=TM_DENSE)


def kernel(x, na_norm, na_w_qkv, na_rpb, na_w_o, gdn_norm, gdn_w_in, gdn_conv, gdn_a_log_f, gdn_a_log_b, gdn_dt_bias_f, gdn_dt_bias_b, gdn_o_norm, gdn_w_o, mla_norm, mla_w_in, mla_q_norm, mla_w_uq, mla_kv_norm, mla_w_ukv, mla_w_o, ret_norm, ret_w_in, ret_gn, ret_w_o, mlp_norm, mlp_w1, mlp_w2, final_norm):
    b, s, d = x.shape
    depth = mlp_norm.shape[0]
    outs = []
    for bi in range(b):
        h = x[bi].astype(F32)
        for i in range(depth):
            m, j = i % 4, i // 4
            if m == 0:
                h = _na_layer(h, na_norm[j], na_w_qkv[j], na_rpb[j], na_w_o[j])
            elif m == 1:
                h = _gdn_layer(h, gdn_norm[j], gdn_w_in[j], gdn_conv[j], gdn_a_log_f[j], gdn_a_log_b[j],
                               gdn_dt_bias_f[j], gdn_dt_bias_b[j], gdn_o_norm[j], gdn_w_o[j])
            elif m == 2:
                h = _mla_layer(h, mla_norm[j], mla_w_in[j], mla_q_norm[j], mla_w_uq[j], mla_kv_norm[j],
                               mla_w_ukv[j], mla_w_o[j])
            else:
                h = _ret_layer(h, ret_norm[j], ret_w_in[j], ret_gn[j], ret_w_o[j])
            h = mlp_block(h, mlp_norm[i], mlp_w1[i].astype(BF16), mlp_w2[i].astype(BF16), final_norm,
                          final=(i == depth - 1), tm=TM_DENSE, tf=mlp_w1.shape[2], name=f"mlp_{i}")
        outs.append(h)
    return jnp.stack(outs).astype(x.dtype)
```

```python
import functools
import math

import numpy as np
import jax
import jax.numpy as jnp
from jax import lax
from jax.experimental import pallas as pl
from jax.experimental.pallas import tpu as pltpu

F32 = jnp.float32
BF16 = jnp.bfloat16
EPS = 1e-6
ROPE_THETA = 10000.0
GRID_W = 64
NEG_BIG = -1e30
HI = lax.Precision.HIGHEST

NA_HEADS, NA_DH, NA_WIN_ROWS, NA_WIN_COLS, NA_Q_ROWS = 16, 64, 8, 16, 2
NA_KBLK = 5
GDN_HEADS, GDN_DK, GDN_CONV, GDN_CHUNK = 8, 128, 5, 64
GDN_GATE_LANES = 32
MLA_HEADS, MLA_Q_RANK, MLA_KV_RANK, MLA_NOPE, MLA_ROPE, MLA_V = 16, 768, 256, 64, 32, 64
MLA_VX = 80
RET_HEADS, RET_DK, RET_DV, RET_CHUNK = 8, 128, 256, 128

VMEM_LIMIT = 52 * 1024 * 1024

TM_DENSE = 512
TM_WIDE = 256
TM_OUT = 1024
TB_GDN = 256
TQ_MLA, TK_MLA = 2048, 512


def _cparams(*sem):
    return pltpu.CompilerParams(dimension_semantics=sem, vmem_limit_bytes=VMEM_LIMIT)


def _rms(x, g):
    return x * lax.rsqrt(jnp.mean(x * x, axis=-1, keepdims=True) + EPS) * g


def _sigmoid(x):
    return 1.0 / (1.0 + jnp.exp(-x))


def _dot(a, b, **kw):
    return jnp.dot(a, b, preferred_element_type=F32, **kw)


def _dot_nt(a, b, **kw):
    return lax.dot_general(a, b, (((1,), (1,)), ((), ())), preferred_element_type=F32, **kw)


def _dot_tn(a, b, **kw):
    return lax.dot_general(a, b, (((0,), (0,)), ((), ())), preferred_element_type=F32, **kw)


def _norm_matmul_body(x_ref, g_ref, w_ref, o_ref, xn_ref):
    @pl.when(pl.program_id(1) == 0)
    def _():
        xn_ref[...] = _rms(x_ref[...], g_ref[...]).astype(BF16)

    o_ref[...] = _dot(xn_ref[...], w_ref[...]).astype(o_ref.dtype)


def norm_matmul(x, g, w, out_dtype, tm, tn, name):
    s, d = x.shape
    n = w.shape[1]
    return pl.pallas_call(
        _norm_matmul_body,
        grid=(s // tm, n // tn),
        in_specs=[pl.BlockSpec((tm, d), lambda i, j: (i, 0)),
                  pl.BlockSpec((1, d), lambda i, j: (0, 0)),
                  pl.BlockSpec((d, tn), lambda i, j: (0, j))],
        out_specs=pl.BlockSpec((tm, tn), lambda i, j: (i, j)),
        out_shape=jax.ShapeDtypeStruct((s, n), out_dtype),
        scratch_shapes=[pltpu.VMEM((tm, d), BF16)],
        compiler_params=_cparams("arbitrary", "arbitrary"),
        name=name,
    )(x, g.reshape(1, d), w)


def _matmul_res_body(a_ref, w_ref, r_ref, o_ref):
    o_ref[...] = r_ref[...] + _dot(a_ref[...], w_ref[...])


def matmul_residual(a, w, res, tm, name):
    s, k = a.shape
    n = w.shape[1]
    return pl.pallas_call(
        _matmul_res_body,
        grid=(s // tm,),
        in_specs=[pl.BlockSpec((tm, k), lambda i: (i, 0)),
                  pl.BlockSpec((k, n), lambda i: (0, 0)),
                  pl.BlockSpec((tm, n), lambda i: (i, 0))],
        out_specs=pl.BlockSpec((tm, n), lambda i: (i, 0)),
        out_shape=jax.ShapeDtypeStruct((s, n), F32),
        compiler_params=_cparams("arbitrary"),
        name=name,
    )(a, w, res)


def _mlp_body(x_ref, g_ref, w1_ref, w2_ref, fg_ref, o_ref, xn_ref, acc_ref, *, nk, final):
    k = pl.program_id(1)

    @pl.when(k == 0)
    def _():
        xn_ref[...] = _rms(x_ref[...], g_ref[...]).astype(BF16)
        acc_ref[...] = jnp.zeros_like(acc_ref)

    a = jnp.maximum(_dot(xn_ref[...], w1_ref[...]), 0.0)
    acc_ref[...] += _dot((a * a).astype(BF16), w2_ref[...])

    @pl.when(k == nk - 1)
    def _():
        y = x_ref[...] + acc_ref[...]
        if final:
            y = _rms(y, fg_ref[...])
        o_ref[...] = y


def mlp_block(x, g, w1, w2, final_g, final, tm, tf, name):
    s, d = x.shape
    f = w1.shape[1]
    nk = f // tf
    return pl.pallas_call(
        functools.partial(_mlp_body, nk=nk, final=final),
        grid=(s // tm, nk),
        in_specs=[pl.BlockSpec((tm, d), lambda i, k: (i, 0)),
                  pl.BlockSpec((1, d), lambda i, k: (0, 0)),
                  pl.BlockSpec((d, tf), lambda i, k: (0, k)),
                  pl.BlockSpec((tf, d), lambda i, k: (k, 0)),
                  pl.BlockSpec((1, d), lambda i, k: (0, 0))],
        out_specs=pl.BlockSpec((tm, d), lambda i, k: (i, 0)),
        out_shape=jax.ShapeDtypeStruct((s, d), F32),
        scratch_shapes=[pltpu.VMEM((tm, d), BF16), pltpu.VMEM((tm, d), F32)],
        compiler_params=_cparams("arbitrary", "arbitrary"),
        name=name,
    )(x, g.reshape(1, d), w1, w2, final_g.reshape(1, d))


def _na_bias_table(rpb, rows):
    n_qblk = rows // NA_Q_ROWS
    reps = {4: 0, 3: 1, 2: 2, 1: n_qblk - 2, 0: n_qblk - 1}
    w, wc, wr = GRID_W, NA_WIN_COLS, NA_WIN_ROWS
    head_order = np.r_[0:NA_HEADS:2, 1:NA_HEADS:2]
    prev = jnp.flip(jnp.pad(rpb.astype(F32)[head_order], ((0, 0), (0, 0), (w - wc, w - wc)), mode="edge"), axis=2)
    e = jnp.stack([prev[:, :, w - 1 - kc:2 * w - 1 - kc] for kc in range(w)], axis=2)
    qc = np.arange(w)
    qc0 = np.clip(qc - wc // 2, 0, w - wc)
    col_ok = (qc[:, None] >= qc0[None, :]) & (qc[:, None] < qc0[None, :] + wc)
    tabs, oks = [], []
    for t in range(5):
        i = reps[t]
        base = int(np.clip(i - 2, 0, n_qblk - NA_KBLK))
        per_qr, ok_qr = [], []
        for qr in range(NA_Q_ROWS):
            qrow = NA_Q_ROWS * i + qr
            qr0 = int(np.clip(qrow - wr // 2, 0, rows - wr))
            krows = NA_Q_ROWS * base + np.arange(NA_KBLK * NA_Q_ROWS)
            drow = np.clip(krows - qrow + wr - 1, 0, 2 * wr - 2)
            row_ok = (krows >= qr0) & (krows < qr0 + wr)
            per_qr.append(jnp.concatenate([e[:, int(d)] for d in drow], axis=1))
            ok_qr.append(np.concatenate([col_ok & bool(r) for r in row_ok], axis=0))
        tabs.append(jnp.concatenate(per_qr, axis=2))
        oks.append(np.concatenate(ok_qr, axis=1))
    return jnp.where(np.stack(oks)[:, None], jnp.stack(tabs), NEG_BIG)


def _na_body(q_ref, k0, k1, k2, k3, k4, v0, v1, v2, v3, v4, b_ref, wo_ref, h_ref, o_ref):
    krefs = (k0, k1, k2, k3, k4)
    vrefs = (v0, v1, v2, v3, v4)
    nkb = NA_Q_ROWS * GRID_W
    npair = NA_HEADS // 2
    pairs = [slice(pr * 128, (pr + 1) * 128) for pr in range(npair)]
    lane_k = lax.broadcasted_iota(jnp.int32, (NA_KBLK * nkb, 128), 1)
    ones_blk = jnp.where(lane_k == 0, 1.0, 0.0).astype(BF16)
    first = lax.broadcasted_iota(jnp.int32, (npair, nkb, 128), 2) < NA_DH
    k_all = jnp.stack([jnp.concatenate([kr[:, sl] for kr in krefs], axis=0) for sl in pairs], axis=0)
    v_aug = jnp.stack([jnp.concatenate([jnp.concatenate([vr[:, sl] for vr in vrefs], axis=0), ones_blk], axis=1)
                       for sl in pairs], axis=0)
    q2 = jnp.stack([q_ref[:, sl] for sl in pairs], axis=0)
    outs = []
    for hh in range(2):
        qh = jnp.where(first if hh == 0 else jnp.logical_not(first), q2, jnp.zeros_like(q2))
        s_t = _bmm_nt(k_all, qh) + b_ref[0, hh * npair:(hh + 1) * npair]
        m = jnp.max(s_t, axis=1, keepdims=True)
        p = jnp.swapaxes(jnp.exp(s_t - m), 1, 2).astype(BF16)
        o = _bmm(p, v_aug)
        outs.append(o[:, :, :128] / o[:, :, 128:129])
    res = jnp.where(first, outs[0], outs[1]).astype(BF16)
    o_all = jnp.concatenate([res[pr] for pr in range(npair)], axis=1)
    o_ref[...] = h_ref[...] + _dot(o_all, wo_ref[...])


def na_attention(qkv, bias_tab, w_o, res):
    s = qkv.shape[0]
    d = NA_HEADS * NA_DH
    tq = NA_Q_ROWS * GRID_W
    n_qblk = s // tq
    top = n_qblk - NA_KBLK

    def base(i):
        return jnp.clip(i - 2, 0, top)

    kv_specs = [pl.BlockSpec((tq, d), functools.partial(lambda i, dd, c: (base(i) + dd, c), dd=dd, c=c))
                for c in (1, 2) for dd in range(NA_KBLK)]
    return pl.pallas_call(
        _na_body,
        grid=(n_qblk,),
        in_specs=[pl.BlockSpec((tq, d), lambda i: (i, 0))] + kv_specs
                 + [pl.BlockSpec((1, NA_HEADS, NA_KBLK * tq, tq), lambda i: (base(i) - i + 4, 0, 0, 0)),
                    pl.BlockSpec(w_o.shape, lambda i: (0, 0)),
                    pl.BlockSpec((tq, res.shape[1]), lambda i: (i, 0))],
        out_specs=pl.BlockSpec((tq, res.shape[1]), lambda i: (i, 0)),
        out_shape=jax.ShapeDtypeStruct(res.shape, F32),
        compiler_params=_cparams("arbitrary"),
        name="na_attention",
    )(qkv, *([qkv] * (2 * NA_KBLK)), bias_tab, w_o, res)


def _gdn_conv_body(x_ref, xp_ref, xn_ref, w_ref, o_ref, xe_ref, *, tm, nt):
    i = pl.program_id(0)
    j = pl.program_id(1)
    xe_ref[0:8, :] = jnp.where(i > 0, xp_ref[...], 0.0)
    xe_ref[8:8 + tm, :] = x_ref[...]
    xe_ref[8 + tm:16 + tm, :] = jnp.where(i < nt - 1, xn_ref[...], 0.0)
    w = w_ref[...]
    half = GDN_CONV // 2
    acc = xe_ref[8 - half:8 - half + tm, :] * w[0:1]
    for t in range(1, GDN_CONV):
        acc = acc + xe_ref[8 - half + t:8 - half + t + tm, :] * w[t:t + 1]
    y = acc * _sigmoid(acc)
    for h in range(GDN_HEADS):
        sl = slice(h * GDN_DK, (h + 1) * GDN_DK)
        yh = y[:, sl]
        nrm = lax.rsqrt(jnp.sum(yh * yh, axis=1, keepdims=True) + EPS)
        fac = jnp.where(j == 0, nrm * GDN_DK ** -0.5, jnp.where(j == 1, nrm, 1.0))
        o_ref[:, sl] = (yh * fac).astype(o_ref.dtype)


def gdn_conv(proj, conv_w_t, tm):
    s = proj.shape[0]
    d = GDN_HEADS * GDN_DK
    nt = s // tm
    r8 = tm // 8
    return pl.pallas_call(
        functools.partial(_gdn_conv_body, tm=tm, nt=nt),
        grid=(nt, 3),
        in_specs=[pl.BlockSpec((tm, d), lambda i, j: (i, j)),
                  pl.BlockSpec((8, d), lambda i, j: (jnp.maximum(i * r8 - 1, 0), j)),
                  pl.BlockSpec((8, d), lambda i, j: (jnp.minimum((i + 1) * r8, nt * r8 - 1), j)),
                  pl.BlockSpec((GDN_CONV, d), lambda i, j: (0, j))],
        out_specs=pl.BlockSpec((tm, d), lambda i, j: (i, j)),
        out_shape=jax.ShapeDtypeStruct((s, 3 * d), BF16),
        scratch_shapes=[pltpu.VMEM((tm + 16, d), F32)],
        compiler_params=_cparams("arbitrary", "arbitrary"),
        name="gdn_conv",
    )(proj, proj, proj, conv_w_t)


def _gdn_gates_body(gp_ref, alog_ref, dtb_ref, col_ref, row_ref, *, tm):
    c = GDN_CHUNK
    gp = gp_ref[...]
    lane = lax.broadcasted_iota(jnp.int32, (tm, 128), 1)
    is_f = (lane >= 16) & (lane < 24)
    is_b = (lane >= 24) & (lane < 32)
    beta = _sigmoid(gp)
    z = gp + dtb_ref[...]
    softplus = jnp.maximum(z, 0.0) + jnp.log1p(jnp.exp(-jnp.abs(z)))
    g = jnp.where(is_f | is_b, -jnp.exp(alog_ref[...]) * softplus, 0.0)
    r = lax.broadcasted_iota(jnp.int32, (c, c), 0)
    cc = lax.broadcasted_iota(jnp.int32, (c, c), 1)
    tri_lo = (r >= cc).astype(F32)
    tri_up = (r <= cc).astype(F32)
    lane_c = lax.broadcasted_iota(jnp.int32, (c, 128), 1)
    is_f_c = (lane_c >= 16) & (lane_c < 24)
    is_g_c = (lane_c >= 16) & (lane_c < 32)
    for ch in range(tm // c):
        gc = g[ch * c:(ch + 1) * c]
        cum = jnp.where(is_f_c, _dot(tri_lo, gc, precision=HI), _dot(tri_up, gc, precision=HI))
        colc = jnp.where(is_g_c, cum, beta[ch * c:(ch + 1) * c])
        col_ref[ch * c:(ch + 1) * c, :] = colc
        row_ref[ch] = colc.T[:GDN_GATE_LANES, :]


def gdn_gates(gp, gate_blk, alog_vec, dtb_vec, tm):
    s = gp.shape[0]
    c = GDN_CHUNK
    return pl.pallas_call(
        functools.partial(_gdn_gates_body, tm=tm),
        grid=(s // tm,),
        in_specs=[pl.BlockSpec((tm, 128), lambda i: (i, gate_blk)),
                  pl.BlockSpec((1, 128), lambda i: (0, 0)),
                  pl.BlockSpec((1, 128), lambda i: (0, 0))],
        out_specs=[pl.BlockSpec((tm, 128), lambda i: (i, 0)),
                   pl.BlockSpec((tm // c, GDN_GATE_LANES, c), lambda i: (i, 0, 0))],
        out_shape=[jax.ShapeDtypeStruct((s, 128), F32), jax.ShapeDtypeStruct((s // c, GDN_GATE_LANES, c), F32)],
        compiler_params=_cparams("arbitrary"),
        name="gdn_gates",
    )(gp, alog_vec, dtb_vec)


def _bmm(a, b):
    return jnp.einsum("nik,nkj->nij", a, b, preferred_element_type=F32)


def _bmm_nt(a, b):
    return jnp.einsum("nik,njk->nij", a, b, preferred_element_type=F32)


def _unit_tri_inverse(lmat, r, cc):
    eye = (r == cc).astype(F32)
    diag_blk = (r // 16) == (cc // 16)
    ld = jnp.where(diag_blk, lmat, 0.0)
    lo = lmat - ld
    x = eye - ld
    p = _bmm(ld, ld)
    x = x + _bmm(x, p)
    p = _bmm(p, p)
    x = x + _bmm(x, p)
    p = _bmm(p, p)
    dinv = x + _bmm(x, p)
    n1 = _bmm(dinv, lo)
    n2 = _bmm(n1, n1)
    y = eye - n1 + n2 - _bmm(n1, n2)
    return _bmm(y, dinv)


def _gdn_local_body(qkv_ref, col_ref, row_ref, *out_refs, tb):
    c, dk, hh = GDN_CHUNK, GDN_DK, GDN_HEADS
    nc = tb // c
    nb = hh * nc
    r = lax.broadcasted_iota(jnp.int32, (nb, c, c), 1)
    cc = lax.broadcasted_iota(jnp.int32, (nb, c, c), 2)
    col3 = col_ref[...].reshape(nc, c, 128)

    def heads(base):
        return jnp.concatenate([qkv_ref[:, base + h * dk:base + (h + 1) * dk].astype(F32).reshape(nc, c, dk)
                                for h in range(hh)], axis=0)

    qc, kc, vc = heads(0), heads(hh * dk), heads(2 * hh * dk)
    kc16 = kc.astype(BF16)
    qk = _bmm_nt(qc.astype(BF16), kc16)
    for d in range(2):
        reverse = d == 1
        u_ref, wq_ref, ak_ref, gl_ref = out_refs[4 * d:4 * d + 4]
        incl = (r <= cc) if reverse else (r >= cc)
        strict = (r < cc) if reverse else (r > cc)
        b0 = 8 if reverse else 0
        g0 = 24 if reverse else 16
        beta = jnp.concatenate([col3[:, :, b0 + h:b0 + h + 1] for h in range(hh)], axis=0)
        gcol = jnp.concatenate([col3[:, :, g0 + h:g0 + h + 1] for h in range(hh)], axis=0)
        grow = jnp.concatenate([row_ref[:, g0 + h:g0 + h + 1, :] for h in range(hh)], axis=0)
        g_last = gcol[:, 0:1, :] if reverse else gcol[:, c - 1:c, :]
        decay = jnp.where(incl, jnp.exp(jnp.where(incl, gcol - grow, 0.0)), 0.0)
        eg = jnp.exp(gcol)
        kb = kc * beta
        lmat = jnp.where(strict, _bmm_nt(kb.astype(BF16), kc16) * decay, 0.0)
        tinv = _unit_tri_inverse(lmat, r, cc)
        sol = _bmm(tinv, jnp.concatenate([vc * beta, kb * eg], axis=2))
        a_intra = jnp.where(incl, qk * decay, 0.0).astype(BF16)
        w16 = sol[:, :, dk:].astype(BF16)
        qd16 = (qc * eg).astype(BF16)
        kdt16 = jnp.swapaxes(kc * jnp.exp(g_last - gcol), 1, 2).astype(BF16)
        gl = jnp.broadcast_to(jnp.exp(g_last), (nb, 1, dk))
        for h in range(hh):
            sl = slice(h * dk, (h + 1) * dk)
            hb = slice(h * nc, (h + 1) * nc)
            u_ref[:, sl] = sol[hb, :, :dk].reshape(tb, dk)
            wq_ref[:, 0:c, sl] = w16[hb]
            wq_ref[:, c:2 * c, sl] = qd16[hb]
            ak_ref[:, 0:c, h * c:(h + 1) * c] = a_intra[hb]
            ak_ref[:, c:c + dk, h * c:(h + 1) * c] = kdt16[hb]
            gl_ref[:, h:h + 1, :] = gl[hb]


def gdn_local(qkv, gcol, grow, tb):
    s = qkv.shape[0]
    c, d = GDN_CHUNK, GDN_HEADS * GDN_DK
    nc = tb // c
    row = lambda w: pl.BlockSpec((tb, w), lambda i: (i, 0))
    per_dir_specs = [row(d),
                     pl.BlockSpec((nc, 2 * c, d), lambda i: (i, 0, 0)),
                     pl.BlockSpec((nc, c + GDN_DK, GDN_HEADS * c), lambda i: (i, 0, 0)),
                     pl.BlockSpec((nc, GDN_HEADS, GDN_DK), lambda i: (i, 0, 0))]
    per_dir_shapes = [jax.ShapeDtypeStruct((s, d), F32),
                      jax.ShapeDtypeStruct((s // c, 2 * c, d), BF16),
                      jax.ShapeDtypeStruct((s // c, c + GDN_DK, GDN_HEADS * c), BF16),
                      jax.ShapeDtypeStruct((s // c, GDN_HEADS, GDN_DK), F32)]
    return pl.pallas_call(
        functools.partial(_gdn_local_body, tb=tb),
        grid=(s // tb,),
        in_specs=[row(3 * d), row(128), pl.BlockSpec((nc, GDN_GATE_LANES, c), lambda i: (i, 0, 0))],
        out_specs=per_dir_specs * 2,
        out_shape=per_dir_shapes * 2,
        compiler_params=_cparams("arbitrary"),
        name="gdn_local",
    )(qkv, gcol, grow)


def _gdn_scan_body(*refs, tb):
    c, dk, hh = GDN_CHUNK, GDN_DK, GDN_HEADS
    nc = tb // c
    ins, (of_ref, ob_ref, st_ref) = refs[:8], refs[8:]

    @pl.when(pl.program_id(0) == 0)
    def _():
        st_ref[...] = jnp.zeros_like(st_ref)

    chains = [(d, h) for d in range(2) for h in range(hh)]
    for step in range(nc):
        ch = (step, nc - 1 - step)

        def gather(idx, pick):
            return jnp.stack([pick(ins[4 * d + idx], ch[d], h) for d, h in chains], axis=0)

        u = gather(0, lambda r, cpos, h: r[cpos * c:(cpos + 1) * c, h * dk:(h + 1) * dk])
        wq = gather(1, lambda r, cpos, h: r[cpos, :, h * dk:(h + 1) * dk])
        ak = gather(2, lambda r, cpos, h: r[cpos, :, h * c:(h + 1) * c])
        gl = gather(3, lambda r, cpos, h: r[cpos, h:h + 1, :])
        state = st_ref[...]
        ws_qs = _bmm(wq, state.astype(BF16))
        v_new = u - ws_qs[:, :c]
        av_kv = _bmm(ak, v_new.astype(BF16))
        o = ws_qs[:, c:] + av_kv[:, :c]
        for n, (d, h) in enumerate(chains):
            (of_ref, ob_ref)[d][ch[d] * c:(ch[d] + 1) * c, h * dk:(h + 1) * dk] = o[n]
        st_ref[...] = state * gl + av_kv[:, c:]


def gdn_scan(local_out, tb):
    s, d = local_out[0].shape
    c = GDN_CHUNK
    nb = s // tb
    nc = tb // c

    def dir_specs(blk):
        return [pl.BlockSpec((tb, d), lambda t: (blk(t), 0)),
                pl.BlockSpec((nc, 2 * c, d), lambda t: (blk(t), 0, 0)),
                pl.BlockSpec((nc, c + GDN_DK, GDN_HEADS * c), lambda t: (blk(t), 0, 0)),
                pl.BlockSpec((nc, GDN_HEADS, GDN_DK), lambda t: (blk(t), 0, 0))]

    fwd = lambda t: t
    bwd = lambda t: nb - 1 - t
    return pl.pallas_call(
        functools.partial(_gdn_scan_body, tb=tb),
        grid=(nb,),
        in_specs=dir_specs(fwd) + dir_specs(bwd),
        out_specs=[pl.BlockSpec((tb, d), lambda t: (t, 0)), pl.BlockSpec((tb, d), lambda t: (nb - 1 - t, 0))],
        out_shape=[jax.ShapeDtypeStruct((s, d), F32)] * 2,
        scratch_shapes=[pltpu.VMEM((2 * GDN_HEADS, GDN_DK, GDN_DK), F32)],
        compiler_params=_cparams("arbitrary"),
        name="gdn_scan",
    )(*local_out)


def _gdn_post_body(of_ref, ob_ref, z_ref, on_ref, w_ref, r_ref, o_ref):
    on = on_ref[...]
    parts = []
    for h in range(GDN_HEADS):
        sl = slice(h * GDN_DK, (h + 1) * GDN_DK)
        o = of_ref[:, sl] + ob_ref[:, sl]
        z = z_ref[:, sl]
        parts.append((_rms(o, on) * (z * _sigmoid(z))).astype(BF16))
    y = jnp.concatenate(parts, axis=1)
    o_ref[...] = r_ref[...] + _dot(y, w_ref[...])


def gdn_post(o_f, o_b, proj, o_norm, w_o, res, tm):
    s, d = o_f.shape
    return pl.pallas_call(
        _gdn_post_body,
        grid=(s // tm,),
        in_specs=[pl.BlockSpec((tm, d), lambda i: (i, 0)),
                  pl.BlockSpec((tm, d), lambda i: (i, 0)),
                  pl.BlockSpec((tm, d), lambda i: (i, 3)),
                  pl.BlockSpec((1, GDN_DK), lambda i: (0, 0)),
                  pl.BlockSpec((d, d), lambda i: (0, 0)),
                  pl.BlockSpec((tm, d), lambda i: (i, 0))],
        out_specs=pl.BlockSpec((tm, d), lambda i: (i, 0)),
        out_shape=jax.ShapeDtypeStruct((s, d), F32),
        compiler_params=_cparams("arbitrary"),
        name="gdn_post",
    )(o_f, o_b, proj, o_norm.reshape(1, GDN_DK), w_o, res)


def _mla_pre_body(x_ref, g_ref, win_ref, qn_ref, kvn_ref, wuq_ref, wuk_ref, wuv_ref, vone_ref, ct_ref, sn_ref,
                  ctt_ref, snt_ref, q_ref, k_ref, v_ref):
    xn = _rms(x_ref[...], g_ref[...]).astype(BF16)
    proj = _dot(xn, win_ref[...])
    cq = _rms(proj[:, :MLA_Q_RANK], qn_ref[...]).astype(BF16)
    ckv = _rms(proj[:, MLA_Q_RANK:MLA_Q_RANK + MLA_KV_RANK], kvn_ref[...]).astype(BF16)
    pr = proj[:, MLA_Q_RANK + MLA_KV_RANK:]
    ct, sn = ct_ref[...], sn_ref[...]
    scale = (MLA_NOPE + MLA_ROPE) ** -0.5 * math.log2(math.e)
    ctt, snt = ctt_ref[...], snt_ref[...]
    q_t = _dot_nt(wuq_ref[...], cq)
    for h in range(MLA_HEADS):
        qh = q_t[h * 128:(h + 1) * 128, :]
        partner = jnp.concatenate([qh[64:, :], qh[:64, :]], axis=0)
        q_ref[h * 128:(h + 1) * 128, :] = ((qh * ctt + partner * snt) * scale).astype(BF16)
    kr = pr * ct + pltpu.roll(pr, 64, 1) * sn
    k = _dot(ckv, wuk_ref[...])
    for h in range(MLA_HEADS):
        sl = slice(h * 128, (h + 1) * 128)
        k_ref[:, sl] = (k[:, sl] + kr).astype(BF16)
    v_t = (_dot_nt(wuv_ref[...], ckv) + vone_ref[...]).astype(BF16)
    v_ref[...] = v_t.reshape(v_ref.shape)


def mla_pre(x, g, win_p, q_norm, kv_norm, wuq_t, wuk_p, wuv_t, vone, ct, sn, tm):
    s, d = x.shape
    hp = MLA_HEADS * 128
    npair = MLA_HEADS // 2
    full = lambda a: pl.BlockSpec(a.shape, lambda i: (0,) * a.ndim)
    g2, qn2, kvn2 = g.reshape(1, d), q_norm.reshape(1, -1), kv_norm.reshape(1, -1)
    ct_t, sn_t = jnp.transpose(ct), jnp.transpose(sn)
    return pl.pallas_call(
        _mla_pre_body,
        grid=(s // tm,),
        in_specs=[pl.BlockSpec((tm, d), lambda i: (i, 0)), full(g2), full(win_p), full(qn2), full(kvn2),
                  full(wuq_t), full(wuk_p), full(wuv_t), full(vone),
                  pl.BlockSpec((tm, 128), lambda i: (i, 0)), pl.BlockSpec((tm, 128), lambda i: (i, 0)),
                  pl.BlockSpec((128, tm), lambda i: (0, i)), pl.BlockSpec((128, tm), lambda i: (0, i))],
        out_specs=[pl.BlockSpec((hp, tm), lambda i: (0, i)), pl.BlockSpec((tm, hp), lambda i: (i, 0)),
                   pl.BlockSpec((npair, 1, 2 * MLA_VX, tm), lambda i: (0, i, 0, 0))],
        out_shape=[jax.ShapeDtypeStruct((hp, s), BF16), jax.ShapeDtypeStruct((s, hp), BF16),
                   jax.ShapeDtypeStruct((npair, s // tm, 2 * MLA_VX, tm), BF16)],
        compiler_params=_cparams("arbitrary"),
        name="mla_pre",
    )(x, g2, win_p, qn2, kvn2, wuq_t, wuk_p, wuv_t, vone, ct, sn, ct_t, sn_t)


def _mla_attn_body(qt_ref, k_ref, vt_ref, o_ref, acc_ref, s_ref, *, tq, tk, nk):
    acc_ref[...] = jnp.zeros_like(acc_ref)

    def scores(j, slot):
        off = pl.multiple_of(j * tk, tk)
        for hh in range(2):
            sl = slice(hh * 128, (hh + 1) * 128)
            s_ref[slot, hh] = _dot(k_ref[pl.ds(off, tk), sl], qt_ref[sl, :])

    def consume(j, slot, carry):
        new = []
        for hh in range(2):
            m_old = carry[hh]
            rows = slice(hh * MLA_VX, (hh + 1) * MLA_VX)
            s = s_ref[slot, hh]
            m_new = jnp.maximum(m_old, jnp.max(s, axis=0, keepdims=True))
            p = jnp.exp2(s - m_new).astype(BF16)
            acc_ref[rows, :] = acc_ref[rows, :] * jnp.exp2(m_old - m_new) + _dot(vt_ref[0, j, rows, :], p)
            new.append(m_new)
        return tuple(new)

    scores(0, 0)

    def body(jj, carry):
        j = 2 * jj
        scores(j + 1, 1)
        carry = consume(j, 0, carry)
        scores(jnp.minimum(j + 2, nk - 1), 0)
        return consume(j + 1, 1, carry)

    lax.fori_loop(0, nk // 2, body, (jnp.full((1, tq), -jnp.inf, F32),) * 2)
    o_t = jnp.concatenate([acc_ref[hh * MLA_VX:hh * MLA_VX + MLA_V, :]
                           / acc_ref[hh * MLA_VX + MLA_V:hh * MLA_VX + MLA_V + 1, :] for hh in range(2)], axis=0)
    o_ref[...] = o_t.T.astype(o_ref.dtype)


def mla_attention(q_t, k, v_t, tq):
    s = k.shape[0]
    npair, nk, _, tk = v_t.shape
    return pl.pallas_call(
        functools.partial(_mla_attn_body, tq=tq, tk=tk, nk=nk),
        grid=(npair, s // tq),
        in_specs=[pl.BlockSpec((256, tq), lambda p, i: (p, i)),
                  pl.BlockSpec((s, 256), lambda p, i: (0, p)),
                  pl.BlockSpec((1, nk, 2 * MLA_VX, tk), lambda p, i: (p, 0, 0, 0))],
        out_specs=pl.BlockSpec((tq, 128), lambda p, i: (i, p)),
        out_shape=jax.ShapeDtypeStruct((s, MLA_HEADS * MLA_V), BF16),
        scratch_shapes=[pltpu.VMEM((2 * MLA_VX, tq), F32), pltpu.VMEM((2, 2, tk, tq), F32)],
        compiler_params=_cparams("arbitrary", "arbitrary"),
        name="mla_attention",
    )(q_t, k, v_t)


def _ret_scan_body(qf, kf, vf, ctf, snf, qb, kb, vb, ctb, snb,
                   dmf, qdf, kdf, gcf, dmb, qdb, kdb, gcb, of_ref, ob_ref, st_ref):
    @pl.when(pl.program_id(0) == 0)
    def _():
        st_ref[...] = jnp.zeros_like(st_ref)

    hh = RET_HEADS
    c = qf.shape[0]

    def chains(ref_f, ref_b, width):
        return jnp.stack([r[:, h * width:(h + 1) * width] for r in (ref_f, ref_b) for h in range(hh)], axis=0)

    def both(a_f, a_b):
        return jnp.concatenate([jnp.broadcast_to(a[...][None], (hh, c, 128)) for a in (a_f, a_b)], axis=0)

    def tab(t_f, t_b):
        return jnp.concatenate([t_f[...], t_b[...]], axis=0)

    q, k = chains(qf, qb, RET_DK), chains(kf, kb, RET_DK)
    v16 = chains(vf, vb, RET_DV).astype(BF16)
    ct, sn = both(ctf, ctb), both(snf, snb)
    qr = q * ct + pltpu.roll(q, 64, 2) * sn
    kr = (k * ct + pltpu.roll(k, 64, 2) * sn) * RET_DK ** -0.5
    state = st_ref[...]
    inner = _bmm_nt(qr.astype(BF16), kr.astype(BF16)) * tab(dmf, dmb)
    o = _bmm(inner.astype(BF16), v16) + _bmm((qr * tab(qdf, qdb)).astype(BF16), state.astype(BF16))
    for h in range(hh):
        of_ref[:, h * RET_DV:(h + 1) * RET_DV] = o[h]
        ob_ref[:, h * RET_DV:(h + 1) * RET_DV] = o[hh + h]
    kd_t = jnp.swapaxes(kr * tab(kdf, kdb), 1, 2).astype(BF16)
    st_ref[...] = state * tab(gcf, gcb) + _bmm(kd_t, v16)


def ret_scan(proj, ct, sn, tabs_f, tabs_b):
    s = proj.shape[0]
    c = RET_CHUNK
    nb = s // c
    dq = RET_HEADS * RET_DK
    dv = RET_HEADS * RET_DV

    def dir_specs(blk):
        return [pl.BlockSpec((c, dq), lambda t: (blk(t), 0)),
                pl.BlockSpec((c, dq), lambda t: (blk(t), 1)),
                pl.BlockSpec((c, dv), lambda t: (blk(t), 1)),
                pl.BlockSpec((c, 128), lambda t: (blk(t), 0)),
                pl.BlockSpec((c, 128), lambda t: (blk(t), 0))]

    full = lambda a: pl.BlockSpec(a.shape, lambda t: (0,) * a.ndim)
    fwd = lambda t: t
    bwd = lambda t: nb - 1 - t
    return pl.pallas_call(
        _ret_scan_body,
        grid=(nb,),
        in_specs=dir_specs(fwd) + dir_specs(bwd) + [full(a) for a in tabs_f] + [full(a) for a in tabs_b],
        out_specs=[pl.BlockSpec((c, dv), lambda t: (t, 0)), pl.BlockSpec((c, dv), lambda t: (nb - 1 - t, 0))],
        out_shape=[jax.ShapeDtypeStruct((s, dv), F32)] * 2,
        scratch_shapes=[pltpu.VMEM((2 * RET_HEADS, RET_DK, RET_DV), F32)],
        compiler_params=_cparams("arbitrary"),
        name="ret_scan",
    )(proj, proj, proj, ct, sn, proj, proj, proj, ct, sn, *tabs_f, *tabs_b)


def _ret_post_body(of_ref, ob_ref, gate_ref, gn_ref, w_ref, r_ref, o_ref):
    parts = []
    for h in range(RET_HEADS):
        sv = slice(h * RET_DV, (h + 1) * RET_DV)
        o = of_ref[:, sv] + ob_ref[:, sv]
        mu = jnp.mean(o, axis=1, keepdims=True)
        oc = o - mu
        var = jnp.mean(oc * oc, axis=1, keepdims=True)
        gate = gate_ref[:, sv]
        parts.append((gate * _sigmoid(gate) * (oc * lax.rsqrt(var + EPS) * gn_ref[:, sv])).astype(BF16))
    y = jnp.concatenate(parts, axis=1)
    o_ref[...] = r_ref[...] + _dot(y, w_ref[...])


def ret_post(o_f, o_b, proj, gn_w, w_o, res, tm):
    s, dv = o_f.shape
    d = res.shape[1]
    return pl.pallas_call(
        _ret_post_body,
        grid=(s // tm,),
        in_specs=[pl.BlockSpec((tm, dv), lambda i: (i, 0)),
                  pl.BlockSpec((tm, dv), lambda i: (i, 0)),
                  pl.BlockSpec((tm, dv), lambda i: (i, 2)),
                  pl.BlockSpec((1, dv), lambda i: (0, 0)),
                  pl.BlockSpec((dv, d), lambda i: (0, 0)),
                  pl.BlockSpec((tm, d), lambda i: (i, 0))],
        out_specs=pl.BlockSpec((tm, d), lambda i: (i, 0)),
        out_shape=jax.ShapeDtypeStruct((s, d), F32),
        compiler_params=_cparams("arbitrary"),
        name="ret_post",
    )(o_f, o_b, proj, gn_w.reshape(1, dv), w_o, res)


def _na_layer(h, norm, w_qkv, rpb, w_o):
    s = h.shape[0]
    d = NA_HEADS * NA_DH
    col_scale = jnp.where(jnp.arange(3 * d) < d, NA_DH ** -0.5, 1.0).astype(F32)
    w16 = (w_qkv * col_scale[None, :]).astype(BF16)
    qkv = norm_matmul(h, norm, w16, BF16, tm=TM_DENSE, tn=3 * d, name="na_qkv")
    return na_attention(qkv, _na_bias_table(rpb, s // GRID_W), w_o.astype(BF16), h)


def _gdn_layer(h, norm, w_in, conv_w, a_log_f, a_log_b, dt_bias_f, dt_bias_b, o_norm, w_o):
    d = GDN_HEADS * GDN_DK
    n_main = 4 * d
    w_all = jnp.pad(w_in, ((0, 0), (0, 128 - 4 * GDN_HEADS))).astype(BF16)
    proj = norm_matmul(h, norm, w_all, F32, tm=TM_DENSE, tn=n_main + 128, name="gdn_in")
    z8 = jnp.zeros((2 * GDN_HEADS,), F32)
    pad = jnp.zeros((128 - 4 * GDN_HEADS,), F32)
    alog_vec = jnp.concatenate([z8, a_log_f.astype(F32), a_log_b.astype(F32), pad]).reshape(1, 128)
    dtb_vec = jnp.concatenate([z8, dt_bias_f.astype(F32), dt_bias_b.astype(F32), pad]).reshape(1, 128)
    gcol, grow = gdn_gates(proj, n_main // 128, alog_vec, dtb_vec, tm=TM_DENSE)
    qkv = gdn_conv(proj, jnp.transpose(conv_w).astype(F32), tm=TM_DENSE)
    o_f, o_b = gdn_scan(gdn_local(qkv, gcol, grow, tb=TB_GDN), tb=TB_GDN)
    return gdn_post(o_f, o_b, proj, o_norm, w_o.astype(BF16), h, tm=TM_DENSE)


def _mla_lane_maps():
    lane = np.arange(128)
    rope1 = lane < 16
    rope2 = (lane >= 64) & (lane < 80)
    nope_a = (lane >= 16) & (lane < 64)
    nope_b = (lane >= 80) & (lane < 96)
    q_dim = np.where(rope1, MLA_NOPE + lane, np.where(nope_a, lane - 16, np.where(rope2, lane + 16, lane - 32)))
    q_ok = lane < 96
    nope_dim = np.where(nope_a, lane - 16, lane - 32)
    nope_ok = nope_a | nope_b
    rope_dim = np.where(rope1, lane, lane - 64 + 16)
    rope_ok = rope1 | rope2
    return q_dim, q_ok, nope_dim, nope_ok, rope_dim, rope_ok


def _mla_layer(h, norm, w_in, q_norm, w_uq, kv_norm, w_ukv, w_o):
    s = h.shape[0]
    hh = MLA_HEADS
    dqk = MLA_NOPE + MLA_ROPE
    q_dim, q_ok, nope_dim, nope_ok, rope_dim, rope_ok = _mla_lane_maps()
    head = np.arange(hh)[:, None]
    q_cols = (head * dqk + np.where(q_ok, q_dim, 0)[None, :]).reshape(-1)
    wuq_t = jnp.transpose(jnp.where(np.tile(q_ok, hh)[None, :], w_uq[:, q_cols], 0.0)).astype(BF16)
    k_cols = (head * (MLA_NOPE + MLA_V) + np.where(nope_ok, nope_dim, 0)[None, :]).reshape(-1)
    wuk_p = jnp.where(np.tile(nope_ok, hh)[None, :], w_ukv[:, k_cols], 0.0).astype(BF16)
    vx = np.arange(MLA_VX)[None, :]
    v_cols = (head * (MLA_NOPE + MLA_V) + MLA_NOPE + np.minimum(vx, MLA_V - 1)).reshape(-1)
    v_ok = np.broadcast_to(vx < MLA_V, (hh, MLA_VX)).reshape(-1)
    wuv_t = jnp.where(v_ok[:, None], jnp.transpose(w_ukv[:, v_cols]), 0.0).astype(BF16)
    vone = jnp.asarray(np.broadcast_to(vx == MLA_V, (hh, MLA_VX)).reshape(-1, 1), F32)
    n_c = MLA_Q_RANK + MLA_KV_RANK
    w_rope = jnp.where(rope_ok[None, :], w_in[:, n_c + np.where(rope_ok, rope_dim, 0)], 0.0)
    win_p = jnp.concatenate([w_in[:, :n_c], w_rope], axis=1).astype(BF16)
    half = MLA_ROPE // 2
    inv = 1.0 / (ROPE_THETA ** (jnp.arange(half, dtype=F32) / half))
    ang = jnp.arange(s, dtype=jnp.int32).astype(F32)[:, None] * inv[None, :]
    cos, sin = jnp.cos(ang), jnp.sin(ang)
    ones = jnp.ones((s, 48), F32)
    zeros = jnp.zeros((s, 48), F32)
    ct = jnp.concatenate([cos, ones, cos, ones], axis=1)
    sn = jnp.concatenate([-sin, zeros, sin, zeros], axis=1)
    q_t, k, v_t = mla_pre(h, norm, win_p, q_norm, kv_norm, wuq_t, wuk_p, wuv_t, vone, ct, sn, tm=TK_MLA)
    o = mla_attention(q_t, k, v_t, tq=min(TQ_MLA, s))
    return matmul_residual(o, w_o.astype(BF16), h, tm=min(TM_OUT, s), name="mla_out")


def _ret_tables(reverse):
    c = RET_CHUNK
    lg = jnp.log1p(-jnp.exp2(-5.0 - jnp.arange(RET_HEADS, dtype=F32)))
    if reverse:
        lg = lg[::-1]
    idx = np.arange(c)
    diff = idx[:, None] - idx[None, :]
    if reverse:
        keep, expo = diff < 0, -diff
        q_pow, k_pow = c - idx, idx
    else:
        keep, expo = diff >= 0, diff
        q_pow, k_pow = idx + 1, c - 1 - idx
    dmask = jnp.where(keep, jnp.exp(jnp.where(keep, expo, 0).astype(F32)[None] * lg[:, None, None]), 0.0)
    q_dec = jnp.exp(jnp.asarray(q_pow, F32)[None, :] * lg[:, None])
    k_dec = jnp.exp(jnp.asarray(k_pow, F32)[None, :] * lg[:, None])
    g_chunk = jnp.exp(c * lg)
    bc = lambda t: jnp.broadcast_to(t[:, :, None], (RET_HEADS, c, RET_DK))
    return [dmask, bc(q_dec), bc(k_dec), jnp.broadcast_to(g_chunk[:, None, None], (RET_HEADS, 1, RET_DV))]


def _ret_layer(h, norm, w_in, gn_w, w_o):
    s = h.shape[0]
    proj = norm_matmul(h, norm, w_in.astype(BF16), F32, tm=TM_WIDE, tn=w_in.shape[1], name="ret_in")
    half = RET_DK // 2
    inv = 1.0 / (ROPE_THETA ** (jnp.arange(half, dtype=F32) / half))
    ang = jnp.arange(s, dtype=jnp.int32).astype(F32)[:, None] * inv[None, :]
    cos, sin = jnp.cos(ang), jnp.sin(ang)
    ct = jnp.concatenate([cos, cos], axis=1)
    sn = jnp.concatenate([-sin, sin], axis=1)
    o_f, o_b = ret_scan(proj, ct, sn, _ret_tables(False), _ret_tables(True))
    return ret_post(o_f, o_b, proj, gn_w, w_o.astype(BF16), h, tm=TM_DENSE)


def kernel(x, na_norm, na_w_qkv, na_rpb, na_w_o, gdn_norm, gdn_w_in, gdn_conv, gdn_a_log_f, gdn_a_log_b, gdn_dt_bias_f, gdn_dt_bias_b, gdn_o_norm, gdn_w_o, mla_norm, mla_w_in, mla_q_norm, mla_w_uq, mla_kv_norm, mla_w_ukv, mla_w_o, ret_norm, ret_w_in, ret_gn, ret_w_o, mlp_norm, mlp_w1, mlp_w2, final_norm):
    b, s, d = x.shape
    depth = mlp_norm.shape[0]
    outs = []
    for bi in range(b):
        h = x[bi].astype(F32)
        for i in range(depth):
            m, j = i % 4, i // 4
            if m == 0:
                h = _na_layer(h, na_norm[j], na_w_qkv[j], na_rpb[j], na_w_o[j])
            elif m == 1:
                h = _gdn_layer(h, gdn_norm[j], gdn_w_in[j], gdn_conv[j], gdn_a_log_f[j], gdn_a_log_b[j],
                               gdn_dt_bias_f[j], gdn_dt_bias_b[j], gdn_o_norm[j], gdn_w_o[j])
            elif m == 2:
                h = _mla_layer(h, mla_norm[j], mla_w_in[j], mla_q_norm[j], mla_w_uq[j], mla_kv_norm[j],
                               mla_w_ukv[j], mla_w_o[j])
            else:
                h = _ret_layer(h, ret_norm[j], ret_w_in[j], ret_gn[j], ret_w_o[j])
            h = mlp_block(h, mlp_norm[i], mlp_w1[i].astype(BF16), mlp_w2[i].astype(BF16), final_norm,
                          final=(i == depth - 1), tm=TM_DENSE, tf=mlp_w1.shape[2], name=f"mlp_{i}")
        outs.append(h)
    return jnp.stack(outs).astype(x.dtype)
```
